```python
import math
import jax, jax.numpy as jnp
from jax import lax
import numpy as np

D_MODEL = 1024
BATCH = 8
SEQ = 2048
DEPTH = 1
DEC_BATCH = 128
DEC_SEQ = 4
PAST_LEN = 16384
PAGE_SIZE = 128

N_META = 16
D_HG = D_MODEL // 2
HG_HEADS = 4
HG_DK = D_HG // HG_HEADS
HG_DV = D_HG // HG_HEADS
HG_CHUNK = 16
D_SSM = D_MODEL - D_HG
SSM_GROUP = 16
SSM_GROUPS = D_SSM // SSM_GROUP
SSM_STATE = 64
D_IN = 4 * D_HG + D_SSM
PEER_HEADS = 8
PEER_NKEYS = 128
PEER_EXPERTS = PEER_NKEYS * PEER_NKEYS
PEER_TOPK = 16
PEER_DHALF = 128
PEER_BLOCK = 256
DN_ALPHA = (2.0 * DEPTH) ** 0.25
DN_BETA = (8.0 * DEPTH) ** -0.25
LN_EPS = 1e-5
RMS_EPS = 1e-6

kernel_name = 'hymba_hgrn2_s5_peer_step'

F32 = jnp.float32


def _layer_norm(x, g, b):
    mu = jnp.mean(x, axis=-1, keepdims=True)
    var = jnp.mean(jnp.square(x - mu), axis=-1, keepdims=True)
    return (x - mu) * lax.rsqrt(var + LN_EPS) * g.astype(F32) + b.astype(F32)


def _hgrn2_chunked(q, k, v, logf, s0):
    bsz, L, H, DK = q.shape
    C = HG_CHUNK if L % HG_CHUNK == 0 else L
    N = L // C

    def to_chunks(a):
        return jnp.moveaxis(a.reshape((bsz, N, C) + a.shape[2:]), 1, 0)

    mask = jnp.tril(jnp.ones((C, C), dtype=bool))[None, :, :, None, None]

    def step(S, inp):
        qc, kc, vc, lfc = inp
        bcum = jnp.cumsum(lfc, axis=1)
        rel = bcum[:, :, None] - bcum[:, None]
        decay = jnp.exp(jnp.where(mask, rel, -jnp.inf))
        att = jnp.einsum('bthk,bshk,btshk->bhts', qc, kc, decay)
        o = (jnp.einsum('bhts,bshv->bthv', att, vc)
             + jnp.einsum('bthk,bhkv->bthv', qc * jnp.exp(bcum), S))
        blast = bcum[:, -1]
        S = (jnp.exp(blast)[..., None] * S
             + jnp.einsum('bshk,bshv->bhkv', kc * jnp.exp(blast[:, None] - bcum), vc))
        return S, o

    S, o = lax.scan(step, s0, (to_chunks(q), to_chunks(k), to_chunks(v), to_chunks(logf)))
    o = jnp.moveaxis(o, 0, 1).reshape(bsz, L, H, v.shape[-1])
    return o, S


def _s5_scan(u, a_re, a_im, log_dt, b_re, b_im, c_re, c_im, d, h0_re, h0_im):
    L = u.shape[1]
    a_re = a_re.astype(F32)
    a_im = a_im.astype(F32)
    dt = jnp.exp(log_dt.astype(F32))[:, None]
    zr, zi = a_re * dt, a_im * dt
    mag = jnp.exp(zr)
    ab_re, ab_im = mag * jnp.cos(zi), mag * jnp.sin(zi)
    den = a_re * a_re + a_im * a_im
    nr = ab_re - 1.0
    coef_re = (nr * a_re + ab_im * a_im) / den
    coef_im = (ab_im * a_re - nr * a_im) / den
    b_re = b_re.astype(F32)
    b_im = b_im.astype(F32)
    bb_re = coef_re[..., None] * b_re - coef_im[..., None] * b_im
    bb_im = coef_re[..., None] * b_im + coef_im[..., None] * b_re
    bu_re = jnp.einsum('blgc,gpc->blgp', u, bb_re)
    bu_im = jnp.einsum('blgc,gpc->blgp', u, bb_im)
    a_r = jnp.broadcast_to(ab_re, bu_re.shape)
    a_i = jnp.broadcast_to(ab_im, bu_re.shape)

    def combine(e1, e2):
        a1r, a1i, b1r, b1i = e1
        a2r, a2i, b2r, b2i = e2
        return (a2r * a1r - a2i * a1i, a2r * a1i + a2i * a1r,
                a2r * b1r - a2i * b1i + b2r, a2r * b1i + a2i * b1r + b2i)

    _, _, hr, hi = lax.associative_scan(combine, (a_r, a_i, bu_re, bu_im), axis=1)
    t = jnp.arange(1, L + 1, dtype=F32)[:, None, None]
    pm = jnp.exp(zr * t)
    pr, pim = pm * jnp.cos(zi * t), pm * jnp.sin(zi * t)
    hr = hr + pr * h0_re[:, None] - pim * h0_im[:, None]
    hi = hi + pr * h0_im[:, None] + pim * h0_re[:, None]
    y = (jnp.einsum('blgp,gcp->blgc', hr, c_re.astype(F32))
         - jnp.einsum('blgp,gcp->blgc', hi, c_im.astype(F32))
         + d.astype(F32).reshape(SSM_GROUPS, SSM_GROUP) * u)
    return y, hr[:, -1], hi[:, -1]


def _peer(x2, w_q, keys, u_tab, v_tab):
    T = x2.shape[0]
    nb = -(-T // PEER_BLOCK)
    xb = jnp.pad(x2, ((0, nb * PEER_BLOCK - T), (0, 0))).reshape(nb, PEER_BLOCK, D_MODEL)
    keys = keys.astype(F32)

    def block(xt):
        q = jnp.dot(xt, w_q).reshape(PEER_BLOCK, PEER_HEADS, 2, PEER_DHALF)
        s = jnp.einsum('thcd,hcnd->thcn', q, keys)
        s1, i1 = lax.top_k(s[:, :, 0], PEER_TOPK)
        s2, i2 = lax.top_k(s[:, :, 1], PEER_TOPK)
        cand = (s1[..., :, None] + s2[..., None, :]).reshape(PEER_BLOCK, PEER_HEADS, PEER_TOPK * PEER_TOPK)
        sc, ci = lax.top_k(cand, PEER_TOPK)
        e = (jnp.take_along_axis(i1, ci // PEER_TOPK, axis=-1) * PEER_NKEYS
             + jnp.take_along_axis(i2, ci % PEER_TOPK, axis=-1))
        g = jax.nn.softmax(sc, axis=-1)
        a = jax.nn.gelu(jnp.einsum('thkd,td->thk', u_tab[e].astype(F32), xt))
        return jnp.einsum('thk,thkd->td', g * a, v_tab[e].astype(F32))

    return lax.map(block, xb).reshape(nb * PEER_BLOCK, D_MODEL)[:T]


def _layer(x, s_hg, s_re, s_im, lb, w_in, b_in, hg_norm_g, a_re, a_im, log_dt, b_re, b_im,
           c_re, c_im, d, w_glu, b_glu, w_out, ln1_g, ln1_b, pw_q, pkeys, pu, pv, ln2_g, ln2_b):
    bsz, L, _ = x.shape
    proj = jnp.einsum('bld,de->ble', x, w_in) + b_in.astype(F32)
    q = proj[..., :D_HG]
    fpre = proj[..., D_HG:2 * D_HG]
    vin = proj[..., 2 * D_HG:3 * D_HG]
    gate = proj[..., 3 * D_HG:4 * D_HG]
    u = proj[..., 4 * D_HG:]
    f = lb + (1.0 - lb) * jax.nn.sigmoid(fpre)
    logf = jnp.log(f)
    kk = (1.0 - lb) * jax.nn.sigmoid(-fpre)
    heads = lambda a: a.reshape(bsz, L, HG_HEADS, HG_DK)
    o, s_hg_new = _hgrn2_chunked(heads(q), heads(kk), vin.reshape(bsz, L, HG_HEADS, HG_DV), heads(logf), s_hg)
    o = o * lax.rsqrt(jnp.mean(o * o, axis=-1, keepdims=True) + RMS_EPS)
    o_hg = o.reshape(bsz, L, D_HG) * hg_norm_g.astype(F32) * jax.nn.silu(gate)
    y, s_re_new, s_im_new = _s5_scan(u.reshape(bsz, L, SSM_GROUPS, SSM_GROUP), a_re, a_im, log_dt,
                                     b_re, b_im, c_re, c_im, d, s_re, s_im)
    z = jax.nn.gelu(y.reshape(bsz, L, D_SSM))
    o_ssm = z * jax.nn.sigmoid(jnp.einsum('bld,de->ble', z, w_glu) + b_glu.astype(F32))
    mix = jnp.einsum('bld,de->ble', jnp.concatenate([o_hg, o_ssm], axis=-1), w_out)
    x = _layer_norm(DN_ALPHA * x + mix, ln1_g, ln1_b)
    ffn = _peer(x.reshape(bsz * L, D_MODEL), pw_q, pkeys, pu, pv).reshape(bsz, L, D_MODEL)
    x = _layer_norm(DN_ALPHA * x + ffn, ln2_g, ln2_b)
    return x, s_hg_new, s_re_new, s_im_new


def setup_inputs(seed: int = 0) -> dict:
    key = jax.random.key(seed)
    ks = jax.random.split(key, 40)
    nrm = lambda i, shape, s: s * jax.random.normal(ks[i], shape, F32)
    a_im = jnp.broadcast_to(math.pi * jnp.arange(SSM_STATE, dtype=F32), (DEPTH, SSM_GROUPS, SSM_STATE))
    return {
        'x_prompt': nrm(0, (BATCH, SEQ, D_MODEL), 1.0),
        'x_sample': nrm(1, (DEC_BATCH, DEC_SEQ, D_MODEL), 1.0),
        'state_hgrn': nrm(2, (DEPTH, DEC_BATCH, HG_HEADS, HG_DK, HG_DV), 0.5),
        'state_ssm_re': nrm(3, (DEPTH, DEC_BATCH, SSM_GROUPS, SSM_STATE), 0.5),
        'state_ssm_im': nrm(4, (DEPTH, DEC_BATCH, SSM_GROUPS, SSM_STATE), 0.5),
        'meta_tokens': nrm(5, (N_META, D_MODEL), 1.0),
        'ln_emb_g': 1.0 + nrm(6, (D_MODEL,), 0.02),
        'ln_emb_b': nrm(7, (D_MODEL,), 0.02),
        'lb_logits': nrm(8, (DEPTH + 1, D_HG), 0.1),
        'w_in': nrm(9, (DEPTH, D_MODEL, D_IN), D_MODEL ** -0.5),
        'b_in': nrm(10, (DEPTH, D_IN), 0.02),
        'hg_norm_g': 1.0 + nrm(11, (DEPTH, D_HG), 0.02),
        'ssm_a_re': -0.5 + nrm(12, (DEPTH, SSM_GROUPS, SSM_STATE), 0.01),
        'ssm_a_im': a_im + nrm(13, (DEPTH, SSM_GROUPS, SSM_STATE), 0.01),
        'ssm_log_dt': jax.random.uniform(ks[14], (DEPTH, SSM_GROUPS), F32, math.log(1e-3), math.log(1e-1)),
        'ssm_b_re': nrm(15, (DEPTH, SSM_GROUPS, SSM_STATE, SSM_GROUP), (2.0 * SSM_GROUP) ** -0.5),
        'ssm_b_im': nrm(16, (DEPTH, SSM_GROUPS, SSM_STATE, SSM_GROUP), (2.0 * SSM_GROUP) ** -0.5),
        'ssm_c_re': nrm(17, (DEPTH, SSM_GROUPS, SSM_GROUP, SSM_STATE), (2.0 * SSM_STATE) ** -0.5),
        'ssm_c_im': nrm(18, (DEPTH, SSM_GROUPS, SSM_GROUP, SSM_STATE), (2.0 * SSM_STATE) ** -0.5),
        'ssm_d': nrm(19, (DEPTH, D_SSM), 0.5),
        'w_glu': nrm(20, (DEPTH, D_SSM, D_SSM), D_SSM ** -0.5),
        'b_glu': nrm(21, (DEPTH, D_SSM), 0.02),
        'w_out': nrm(22, (DEPTH, D_MODEL, D_MODEL), DN_BETA * D_MODEL ** -0.5),
        'ln1_g': 1.0 + nrm(23, (DEPTH, D_MODEL), 0.02),
        'ln1_b': nrm(24, (DEPTH, D_MODEL), 0.02),
        'peer_w_q': nrm(25, (DEPTH, D_MODEL, PEER_HEADS * 2 * PEER_DHALF), D_MODEL ** -0.5),
        'peer_keys': nrm(26, (DEPTH, PEER_HEADS, 2, PEER_NKEYS, PEER_DHALF), PEER_DHALF ** -0.5),
        'peer_u': nrm(27, (DEPTH, PEER_EXPERTS, D_MODEL), D_MODEL ** -0.5),
        'peer_v': nrm(28, (DEPTH, PEER_EXPERTS, D_MODEL), DN_BETA),
        'ln2_g': 1.0 + nrm(29, (DEPTH, D_MODEL), 0.02),
        'ln2_b': nrm(30, (DEPTH, D_MODEL), 0.02),
    }


def reference(x_prompt, x_sample, state_hgrn, state_ssm_re, state_ssm_im, meta_tokens, ln_emb_g, ln_emb_b,
              lb_logits, w_in, b_in, hg_norm_g, ssm_a_re, ssm_a_im, ssm_log_dt, ssm_b_re, ssm_b_im,
              ssm_c_re, ssm_c_im, ssm_d, w_glu, b_glu, w_out, ln1_g, ln1_b, peer_w_q, peer_keys,
              peer_u, peer_v, ln2_g, ln2_b):
    lbs = jnp.cumsum(jax.nn.softmax(lb_logits.astype(F32), axis=0), axis=0)
    bp = x_prompt.shape[0]
    meta = jnp.broadcast_to(meta_tokens.astype(F32)[None], (bp, N_META, D_MODEL))
    xp = _layer_norm(jnp.concatenate([meta, x_prompt.astype(F32)], axis=1), ln_emb_g, ln_emb_b)
    xs = _layer_norm(x_sample.astype(F32), ln_emb_g, ln_emb_b)
    hg_p, re_p, im_p, hg_s, re_s, im_s = [], [], [], [], [], []
    for l in range(DEPTH):
        lw = (lbs[l], w_in[l], b_in[l], hg_norm_g[l], ssm_a_re[l], ssm_a_im[l], ssm_log_dt[l],
              ssm_b_re[l], ssm_b_im[l], ssm_c_re[l], ssm_c_im[l], ssm_d[l], w_glu[l], b_glu[l],
              w_out[l], ln1_g[l], ln1_b[l], peer_w_q[l], peer_keys[l], peer_u[l], peer_v[l],
              ln2_g[l], ln2_b[l])
        xp, shg, sre, sim = _layer(xp, jnp.zeros((bp, HG_HEADS, HG_DK, HG_DV), F32),
                                   jnp.zeros((bp, SSM_GROUPS, SSM_STATE), F32),
                                   jnp.zeros((bp, SSM_GROUPS, SSM_STATE), F32), *lw)
        hg_p.append(shg)
        re_p.append(sre)
        im_p.append(sim)
        xs, shg, sre, sim = _layer(xs, state_hgrn[l].astype(F32), state_ssm_re[l].astype(F32),
                                   state_ssm_im[l].astype(F32), *lw)
        hg_s.append(shg)
        re_s.append(sre)
        im_s.append(sim)
    y_prompt = xp[:, N_META:].astype(x_prompt.dtype)
    y_sample = xs.astype(x_sample.dtype)
    return (y_prompt, y_sample, jnp.stack(hg_p), jnp.stack(re_p), jnp.stack(im_p),
            jnp.stack(hg_s), jnp.stack(re_s), jnp.stack(im_s))
```

```python
import functools
import math

import jax
import jax.numpy as jnp
from jax import lax
from jax.experimental import pallas as pl
from jax.experimental.pallas import tpu as pltpu

F32 = jnp.float32
BF16 = jnp.bfloat16

LN_EPS = 1e-5
RMS_EPS = 1e-6
HG_CHUNK = 16
SSM_GROUP = 16
S5_CHUNK = 16
PEER_TOPK = 16
LANES = 128
SUBLANES = 8
VMEM_LIMIT = 56 * 1024 * 1024
PEER_EXPERT_ROWS = 8


def _cparams(*sem):
    return pltpu.CompilerParams(dimension_semantics=sem, vmem_limit_bytes=VMEM_LIMIT)


def _layer_norm(x, g, b):
    mu = jnp.mean(x, axis=-1, keepdims=True)
    xc = x - mu
    var = jnp.mean(xc * xc, axis=-1, keepdims=True)
    return xc * lax.rsqrt(var + LN_EPS) * g + b


def _sigmoid(x):
    return 1.0 / (1.0 + jnp.exp(-x))


def _gelu(x):
    return 0.5 * x * (1.0 + jnp.tanh(math.sqrt(2.0 / math.pi) * (x + 0.044715 * (x * x * x))))


def _row_tile(t, target):
    best = None
    for cand in range(SUBLANES, min(t, target) + 1, SUBLANES):
        if t % cand == 0:
            best = cand
    return best if best is not None else t


def _inproj_kernel(x_ref, g_ref, b_ref, w_ref, bi_ref, x0_ref, proj_ref, *, apply_ln):
    x = x_ref[...]
    if apply_ln:
        x = _layer_norm(x, g_ref[...], b_ref[...])
    x0_ref[...] = x
    proj_ref[...] = jnp.dot(x.astype(BF16), w_ref[...], preferred_element_type=F32) + bi_ref[...]


def _inproj(x, g, b, w_bf16, bias, *, apply_ln):
    t, d = x.shape
    d_in = w_bf16.shape[1]
    tm = _row_tile(t, 512)
    return pl.pallas_call(
        functools.partial(_inproj_kernel, apply_ln=apply_ln),
        grid=(t // tm,),
        in_specs=[
            pl.BlockSpec((tm, d), lambda i: (i, 0)),
            pl.BlockSpec((1, d), lambda i: (0, 0)),
            pl.BlockSpec((1, d), lambda i: (0, 0)),
            pl.BlockSpec((d, d_in), lambda i: (0, 0)),
            pl.BlockSpec((1, d_in), lambda i: (0, 0)),
        ],
        out_specs=[
            pl.BlockSpec((tm, d), lambda i: (i, 0)),
            pl.BlockSpec((tm, d_in), lambda i: (i, 0)),
        ],
        out_shape=[jax.ShapeDtypeStruct((t, d), F32), jax.ShapeDtypeStruct((t, d_in), F32)],
        compiler_params=_cparams("parallel"),
        name="inproj",
    )(x, g.reshape(1, d), b.reshape(1, d), w_bf16, bias.reshape(1, d_in))


def _hgrn_kernel(q_ref, f_ref, v_ref, lb_ref, s0_ref, o_ref, s_ref, *, chunk, n_chunks, valid):
    lb = lb_ref[...]
    ones = jnp.ones((LANES, LANES), BF16)
    row = lax.broadcasted_iota(jnp.int32, (chunk, LANES), 0)

    def step(c, st):
        r0 = pl.multiple_of(c * chunk, chunk)
        q = q_ref[pl.ds(r0, chunk), :]
        fp = f_ref[pl.ds(r0, chunk), :]
        v = v_ref[pl.ds(r0, chunk), :]
        f = lb + (1.0 - lb) * _sigmoid(fp)
        logf = jnp.log(f)
        kk = (1.0 - lb) * _sigmoid(-fp)
        if valid < chunk:
            live = row < valid
            logf = jnp.where(live, logf, 0.0)
            kk = jnp.where(live, kk, 0.0)
            q = jnp.where(live, q, 0.0)
            v = jnp.where(live, v, 0.0)
        bc = jnp.zeros((chunk, LANES), F32)
        for s in range(chunk):
            bc = bc + jnp.where(row >= s, jnp.broadcast_to(logf[s:s + 1, :], (chunk, LANES)), 0.0)
        prods = []
        for s in range(chunk):
            d = bc - jnp.broadcast_to(bc[s:s + 1, :], (chunk, LANES))
            e = jnp.where(row >= s, jnp.exp(d), 0.0)
            prods.append(q * e * jnp.broadcast_to(kk[s:s + 1, :], (chunk, LANES)))
        p_all = jnp.concatenate(prods, axis=0).astype(BF16)
        att = jnp.dot(p_all, ones, preferred_element_type=F32)
        o = jnp.zeros((chunk, LANES), F32)
        for s in range(chunk):
            o = o + att[s * chunk:(s + 1) * chunk, :] * jnp.broadcast_to(v[s:s + 1, :], (chunk, LANES))
        qs = (q * jnp.exp(bc)).astype(BF16)
        o = o + lax.dot_general(qs, st.astype(BF16), (((1,), (1,)), ((), ())), preferred_element_type=F32)
        blast = jnp.broadcast_to(bc[chunk - 1:chunk, :], (chunk, LANES))
        kt = (kk * jnp.exp(blast - bc)).astype(BF16)
        ds = lax.dot_general(v.astype(BF16), kt, (((0,), (0,)), ((), ())), preferred_element_type=F32)
        st = st * jnp.exp(bc[chunk - 1:chunk, :]) + ds
        o = o * lax.rsqrt(jnp.mean(o * o, axis=-1, keepdims=True) + RMS_EPS)
        o_ref[pl.ds(r0, chunk), :] = o
        return st

    st = lax.fori_loop(0, n_chunks, step, s0_ref[...].T)
    s_ref[...] = st.T


def _hgrn(proj3, lb, s0, *, valid):
    bsz, length, _ = proj3.shape
    heads = s0.shape[1]
    chunk = HG_CHUNK if length % HG_CHUNK == 0 else length
    n_chunks = length // chunk
    seq_spec = lambda off: pl.BlockSpec((None, length, LANES), lambda b, h: (b, 0, off + h))
    return pl.pallas_call(
        functools.partial(_hgrn_kernel, chunk=chunk, n_chunks=n_chunks, valid=min(valid, chunk)),
        grid=(bsz, heads),
        in_specs=[
            seq_spec(0), seq_spec(heads), seq_spec(2 * heads),
            pl.BlockSpec((1, LANES), lambda b, h: (0, h)),
            pl.BlockSpec((None, None, LANES, LANES), lambda b, h: (b, h, 0, 0)),
        ],
        out_specs=[
            pl.BlockSpec((None, length, LANES), lambda b, h: (b, 0, h)),
            pl.BlockSpec((None, None, LANES, LANES), lambda b, h: (b, h, 0, 0)),
        ],
        out_shape=[jax.ShapeDtypeStruct((bsz, length, heads * LANES), F32),
                   jax.ShapeDtypeStruct(s0.shape, F32)],
        compiler_params=_cparams("parallel", "parallel"),
        name="hgrn2",
    )(proj3, proj3, proj3, lb, s0)


def _s5_kernel(u_ref, m_ref, bre_ref, bim_ref, cre_ref, cim_ref, are_ref, aim_ref, h0re_ref, h0im_ref,
               y_ref, hre_ref, him_ref, hpre_ref, hpim_ref, *, n_chunks, bsz):
    u = u_ref[...].astype(BF16)
    hl_re = jnp.dot(u, bre_ref[...], preferred_element_type=F32)
    hl_im = jnp.dot(u, bim_ref[...], preferred_element_type=F32)
    hpre_ref[...] = hl_re
    hpim_ref[...] = hl_im
    a_re = are_ref[...]
    a_im = aim_ref[...]

    def step(c, carry):
        h_re, h_im = carry
        r0 = pl.multiple_of(c * bsz, bsz)
        l_re = hpre_ref[pl.ds(r0, bsz), :]
        l_im = hpim_ref[pl.ds(r0, bsz), :]
        hpre_ref[pl.ds(r0, bsz), :] = h_re
        hpim_ref[pl.ds(r0, bsz), :] = h_im
        return (a_re * h_re - a_im * h_im + l_re, a_re * h_im + a_im * h_re + l_im)

    h_re, h_im = lax.fori_loop(0, n_chunks, step, (h0re_ref[...], h0im_ref[...]))
    hre_ref[...] = h_re
    him_ref[...] = h_im
    y = jnp.dot(u, m_ref[...], preferred_element_type=F32)
    y = y + jnp.dot(hpre_ref[...].astype(BF16), cre_ref[...], preferred_element_type=F32)
    y = y + jnp.dot(hpim_ref[...].astype(BF16), cim_ref[...], preferred_element_type=F32)
    y_ref[...] = y


def _s5_params(a_re, a_im, log_dt, b_re, b_im, c_re, c_im, d, lc):
    groups, pstate = a_re.shape
    dt = jnp.exp(log_dt.astype(F32))[:, None]
    a_re = a_re.astype(F32)
    a_im = a_im.astype(F32)
    zr, zi = a_re * dt, a_im * dt
    mag = jnp.exp(zr)
    ab_re, ab_im = mag * jnp.cos(zi), mag * jnp.sin(zi)
    den = a_re * a_re + a_im * a_im
    nr = ab_re - 1.0
    coef_re = (nr * a_re + ab_im * a_im) / den
    coef_im = (ab_im * a_re - nr * a_im) / den
    b_re = b_re.astype(F32)
    b_im = b_im.astype(F32)
    bb_re = coef_re[..., None] * b_re - coef_im[..., None] * b_im
    bb_im = coef_re[..., None] * b_im + coef_im[..., None] * b_re
    j = jnp.arange(lc + 1, dtype=F32)[:, None, None]
    pm = jnp.exp(zr * j)
    pw_re, pw_im = pm * jnp.cos(zi * j), pm * jnp.sin(zi * j)
    c_re = c_re.astype(F32)
    c_im = c_im.astype(F32)
    ca_re = c_re[None] * pw_re[:, :, None, :] - c_im[None] * pw_im[:, :, None, :]
    ca_im = c_re[None] * pw_im[:, :, None, :] + c_im[None] * pw_re[:, :, None, :]
    kj = (jnp.einsum('jgcp,gpe->jgce', ca_re[:lc], bb_re) - jnp.einsum('jgcp,gpe->jgce', ca_im[:lc], bb_im))
    lag = jnp.arange(lc)[None, :] - jnp.arange(lc)[:, None]
    m = kj[jnp.clip(lag, 0, lc - 1)]
    m = jnp.where((lag >= 0)[:, :, None, None, None], m, 0.0)
    m = jnp.transpose(m, (2, 0, 4, 1, 3))
    eye_t = jnp.eye(lc, dtype=F32)
    eye_c = jnp.eye(SSM_GROUP, dtype=F32)
    dmat = d.astype(F32).reshape(groups, SSM_GROUP)
    m = m + (eye_t[None, :, None, :, None] * eye_c[None, None, :, None, :] * dmat[:, None, None, None, :])
    m = m.reshape(groups, lc * SSM_GROUP, lc * SSM_GROUP)
    rev_re, rev_im = pw_re[:lc][::-1], pw_im[:lc][::-1]
    bst_re = rev_re[:, :, :, None] * bb_re[None] - rev_im[:, :, :, None] * bb_im[None]
    bst_im = rev_re[:, :, :, None] * bb_im[None] + rev_im[:, :, :, None] * bb_re[None]
    bst_re = jnp.transpose(bst_re, (1, 0, 3, 2)).reshape(groups, lc * SSM_GROUP, pstate)
    bst_im = jnp.transpose(bst_im, (1, 0, 3, 2)).reshape(groups, lc * SSM_GROUP, pstate)
    cst_re = jnp.transpose(ca_re[1:], (1, 3, 0, 2)).reshape(groups, pstate, lc * SSM_GROUP)
    cst_im = -jnp.transpose(ca_im[1:], (1, 3, 0, 2)).reshape(groups, pstate, lc * SSM_GROUP)
    gm = _s5_merge(lc)
    bd = lambda x: _block_diag(x, gm).astype(BF16)
    cat = lambda x: x.reshape(groups // gm, 1, gm * pstate)
    return dict(m=bd(m), bre=bd(bst_re), bim=bd(bst_im), cre=bd(cst_re), cim=bd(cst_im),
                are=cat(pw_re[lc]), aim=cat(pw_im[lc]))


def _s5_merge(lc):
    return max(1, LANES // (lc * SSM_GROUP))


def _block_diag(x, gm):
    if gm == 1:
        return x
    g, a, b = x.shape
    x = x.reshape(g // gm, gm, a, b)
    eye = jnp.eye(gm, dtype=x.dtype)
    return (x[:, :, :, None, :] * eye[None, :, None, :, None]).reshape(g // gm, gm * a, gm * b)


def _s5(u3, h0_re, h0_im, prm, *, lc):
    bsz, length, dssm = u3.shape
    n_groups = dssm // SSM_GROUP
    gm = _s5_merge(lc)
    groups = n_groups // gm
    pstate = h0_re.shape[-1] * gm
    n_chunks = length // lc
    width = gm * lc * SSM_GROUP
    rows = n_chunks * bsz
    u_t = jnp.transpose(u3.reshape(bsz, n_chunks, lc, n_groups, SSM_GROUP), (1, 0, 3, 2, 4)).reshape(rows, groups * width)
    h0r = jnp.transpose(h0_re.reshape(bsz, groups, pstate), (1, 0, 2))
    h0i = jnp.transpose(h0_im.reshape(bsz, groups, pstate), (1, 0, 2))
    gspec = lambda a, b: pl.BlockSpec((None, a, b), lambda g: (g, 0, 0))
    y_t, hre, him = pl.pallas_call(
        functools.partial(_s5_kernel, n_chunks=n_chunks, bsz=bsz),
        grid=(groups,),
        in_specs=[
            pl.BlockSpec((rows, width), lambda g: (0, g)),
            gspec(width, width), gspec(width, pstate), gspec(width, pstate),
            gspec(pstate, width), gspec(pstate, width),
            gspec(1, pstate), gspec(1, pstate),
            gspec(bsz, pstate), gspec(bsz, pstate),
        ],
        out_specs=[
            pl.BlockSpec((rows, width), lambda g: (0, g)),
            gspec(bsz, pstate), gspec(bsz, pstate),
        ],
        out_shape=[jax.ShapeDtypeStruct((rows, groups * width), F32),
                   jax.ShapeDtypeStruct((groups, bsz, pstate), F32),
                   jax.ShapeDtypeStruct((groups, bsz, pstate), F32)],
        scratch_shapes=[pltpu.VMEM((rows, pstate), F32), pltpu.VMEM((rows, pstate), F32)],
        compiler_params=_cparams("parallel"),
        name="s5",
    )(u_t, prm['m'], prm['bre'], prm['bim'], prm['cre'], prm['cim'], prm['are'], prm['aim'], h0r, h0i)
    y = jnp.transpose(y_t.reshape(n_chunks, bsz, n_groups, lc, SSM_GROUP), (1, 0, 3, 2, 4)).reshape(bsz, length, dssm)
    unmerge = lambda h: jnp.transpose(h, (1, 0, 2)).reshape(bsz, n_groups, pstate // gm)
    return y, unmerge(hre), unmerge(him)


def _mix_kernel(x_ref, o_ref, gate_ref, y_ref, hg_g_ref, wglu_ref, bglu_ref, wo_hg_ref, wo_ssm_ref,
                g_ref, b_ref, out_ref, *, alpha):
    gate = gate_ref[...]
    o_hg = o_ref[...] * hg_g_ref[...] * (gate * _sigmoid(gate))
    z = _gelu(y_ref[...])
    glu = jnp.dot(z.astype(BF16), wglu_ref[...], preferred_element_type=F32) + bglu_ref[...]
    o_ssm = z * _sigmoid(glu)
    mix = (jnp.dot(o_hg.astype(BF16), wo_hg_ref[...], preferred_element_type=F32)
           + jnp.dot(o_ssm.astype(BF16), wo_ssm_ref[...], preferred_element_type=F32))
    out_ref[...] = _layer_norm(alpha * x_ref[...] + mix, g_ref[...], b_ref[...])


def _mix(x0, o_hg, proj, y_ssm, hg_g, wglu, bglu, wo_hg, wo_ssm, g, b, *, alpha):
    t, d = x0.shape
    d_hg = o_hg.shape[1]
    d_ssm = y_ssm.shape[1]
    tm = _row_tile(t, 512)
    gate_block = (3 * d_hg) // d_hg
    row = lambda w: pl.BlockSpec((tm, w), lambda i: (i, 0))
    full = lambda a, bb: pl.BlockSpec((a, bb), lambda i: (0, 0))
    return pl.pallas_call(
        functools.partial(_mix_kernel, alpha=alpha),
        grid=(t // tm,),
        in_specs=[
            row(d), row(d_hg),
            pl.BlockSpec((tm, d_hg), lambda i: (i, gate_block)),
            row(d_ssm),
            full(1, d_hg), full(d_ssm, d_ssm), full(1, d_ssm), full(d_hg, d), full(d_ssm, d),
            full(1, d), full(1, d),
        ],
        out_specs=row(d),
        out_shape=jax.ShapeDtypeStruct((t, d), F32),
        compiler_params=_cparams("parallel"),
        name="mix",
    )(x0, o_hg, proj, y_ssm, hg_g.reshape(1, d_hg), wglu, bglu.reshape(1, d_ssm), wo_hg, wo_ssm,
      g.reshape(1, d), b.reshape(1, d))


def _top_rows(s, k):
    n, t = s.shape
    iota = lax.broadcasted_iota(jnp.int32, (n, t), 0)
    krow = lax.broadcasted_iota(jnp.int32, (k, t), 0)
    rank = jnp.full((n, t), float(k), F32)
    vals = jnp.zeros((k, t), F32)
    for a in range(k):
        m = jnp.max(s, axis=0, keepdims=True)
        idx = jnp.min(jnp.where(s == m, iota, n), axis=0, keepdims=True)
        hit = iota == idx
        rank = jnp.where(hit, float(a), rank)
        s = jnp.where(hit, -jnp.inf, s)
        vals = jnp.where(krow == a, jnp.broadcast_to(m, (k, t)), vals)
    return vals, rank


def _candidate_pieces(t1, t2):
    k = PEER_TOPK
    t = t1.shape[1]
    bc = lambda r, n: jnp.broadcast_to(r, (n, t))
    pieces = [(bc(t1[0:1], k), t2, [b for b in range(k)], [True] * k)]
    half = k // 2
    for a in range(1, half):
        nb = k // (a + 1)
        pieces.append((bc(t1[a:a + 1], half), t2[0:half], [a * k + b for b in range(half)],
                       [b < nb for b in range(half)]))
    pieces.append((t1[half:k], bc(t2[0:1], half), [(half + r) * k for r in range(half)], [True] * half))
    return pieces


def _candidate_index_rows(t):
    k = PEER_TOPK
    dummy = jnp.zeros((k, 1), F32)
    vals = [i if ok else k * k for _, _, idx, val in _candidate_pieces(dummy, dummy) for i, ok in zip(idx, val)]
    return jnp.broadcast_to(jnp.asarray(vals, jnp.int32)[:, None], (len(vals), t))


def _peer_select(s1, s2, cidx):
    k = PEER_TOPK
    t = s1.shape[1]
    big = k * k
    top1, rank1 = _top_rows(s1, k)
    top2, rank2 = _top_rows(s2, k)
    pieces = _candidate_pieces(top1, top2)
    cand = jnp.concatenate([x + y for x, y, _, _ in pieces], axis=0)
    cand = jnp.where(cidx < big, cand, -jnp.inf)
    chosen = jnp.zeros(cand.shape, F32)
    for _ in range(k):
        m = jnp.max(cand, axis=0, keepdims=True)
        idx = jnp.min(jnp.where(cand == m, cidx, big), axis=0, keepdims=True)
        hit = cidx == idx
        chosen = jnp.where(hit, 1.0, chosen)
        cand = jnp.where(hit, -jnp.inf, cand)
    e1 = jnp.exp(top1 - jnp.broadcast_to(top1[0:1], (k, t)))
    e2 = jnp.exp(top2 - jnp.broadcast_to(top2[0:1], (k, t)))
    ecand = jnp.concatenate([x * y for x, y, _, _ in _candidate_pieces(e1, e2)], axis=0)
    z = jnp.sum(chosen * ecand, axis=0, keepdims=True)
    counts = []
    off = 0
    half = k // 2
    for pi, (x, _, _, _) in enumerate(pieces):
        rows = x.shape[0]
        blk = chosen[off:off + rows]
        off += rows
        if pi < len(pieces) - 1:
            counts.append(jnp.sum(blk, axis=0, keepdims=True))
        else:
            counts.extend(blk[r:r + 1] for r in range(half))
    nk = s1.shape[0]
    c1 = jnp.zeros((nk, t), F32)
    for a in range(k):
        c1 = jnp.where(rank1 == float(a), jnp.broadcast_to(counts[a], (nk, t)), c1)
    phi = jnp.exp(s1 - jnp.broadcast_to(top1[0:1], (nk, t))) * jnp.broadcast_to(1.0 / z, (nk, t))
    psi = jnp.exp(s2 - jnp.broadcast_to(top2[0:1], (nk, t)))
    return c1, phi, rank2, psi


def _peer_kernel(x_ref, wq_ref, keys_ref, cidx_ref, u_ref, vt_ref, g_ref, b_ref, out_ref,
                 xb_ref, c1_ref, phi_ref, r2_ref, psi_ref, w_ref, acc_ref, *, alpha, heads, nk, rows_per_step):
    j = pl.program_id(1)
    tm = x_ref.shape[0]
    nt = (((1,), (1,)), ((), ()))

    @pl.when(j == 0)
    def _():
        xb = x_ref[...].astype(BF16)
        xb_ref[...] = xb
        acc_ref[...] = jnp.zeros_like(acc_ref)
        for h in range(heads):
            sc = []
            for c in range(2):
                hc = h * 2 + c
                qt = lax.dot_general(wq_ref[hc * LANES:(hc + 1) * LANES, :], xb, nt, preferred_element_type=F32)
                sc.append(jnp.dot(keys_ref[hc], qt.astype(BF16), preferred_element_type=F32))
            c1, phi, r2, psi = _peer_select(sc[0], sc[1], cidx_ref[...])
            c1_ref[h] = c1
            phi_ref[h] = phi
            r2_ref[h] = r2
            psi_ref[h] = psi

    xb = xb_ref[...]

    def row_step(r, carry):
        n1 = j * rows_per_step + r
        e0 = pl.multiple_of(r * nk, nk)
        ut = lax.dot_general(u_ref[pl.ds(e0, nk), :], xb, nt, preferred_element_type=F32)
        gsum = jnp.zeros((nk, tm), F32)
        for h in range(heads):
            c1row = jnp.broadcast_to(c1_ref[h, pl.ds(n1, 1), :], (nk, tm))
            phirow = jnp.broadcast_to(phi_ref[h, pl.ds(n1, 1), :], (nk, tm))
            gsum = gsum + jnp.where(r2_ref[h] < c1row, phirow * psi_ref[h], 0.0)
        w_ref[pl.ds(e0, nk), :] = (gsum * _gelu(ut)).astype(BF16)
        return carry

    lax.fori_loop(0, rows_per_step, row_step, 0)
    acc_ref[...] += jnp.dot(vt_ref[...], w_ref[...], preferred_element_type=F32)

    @pl.when(j == pl.num_programs(1) - 1)
    def _():
        ffn = acc_ref[...].T
        out_ref[...] = _layer_norm(alpha * x_ref[...] + ffn, g_ref[...], b_ref[...])


def _peer(x, wq_t, keys, u_tab, v_tab_t, g, b, *, alpha):
    t, d = x.shape
    heads, _, nk, dh = keys.shape
    n_exp = u_tab.shape[0]
    tm = 512 if t % 512 == 0 else _row_tile(t, 512)
    rows = PEER_EXPERT_ROWS
    eb = rows * nk
    keys2 = keys.reshape(heads * 2, nk, dh)
    cidx = _candidate_index_rows(tm)
    return pl.pallas_call(
        functools.partial(_peer_kernel, alpha=alpha, heads=heads, nk=nk, rows_per_step=rows),
        grid=(t // tm, n_exp // eb),
        in_specs=[
            pl.BlockSpec((tm, d), lambda i, j: (i, 0)),
            pl.BlockSpec(wq_t.shape, lambda i, j: (0, 0)),
            pl.BlockSpec(keys2.shape, lambda i, j: (0, 0, 0)),
            pl.BlockSpec(cidx.shape, lambda i, j: (0, 0)),
            pl.BlockSpec((eb, d), lambda i, j: (j, 0)),
            pl.BlockSpec((d, eb), lambda i, j: (0, j)),
            pl.BlockSpec((1, d), lambda i, j: (0, 0)),
            pl.BlockSpec((1, d), lambda i, j: (0, 0)),
        ],
        out_specs=pl.BlockSpec((tm, d), lambda i, j: (i, 0)),
        out_shape=jax.ShapeDtypeStruct((t, d), F32),
        scratch_shapes=[
            pltpu.VMEM((tm, d), BF16),
            pltpu.VMEM((heads, nk, tm), F32), pltpu.VMEM((heads, nk, tm), F32),
            pltpu.VMEM((heads, nk, tm), F32), pltpu.VMEM((heads, nk, tm), F32),
            pltpu.VMEM((eb, tm), BF16),
            pltpu.VMEM((d, tm), F32),
        ],
        compiler_params=_cparams("parallel", "arbitrary"),
        name="peer",
    )(x, wq_t, keys2, cidx, u_tab, v_tab_t, g.reshape(1, d), b.reshape(1, d))


def _pad_tokens(a, length):
    return jnp.pad(a, ((0, 0), (0, length - a.shape[1]), (0, 0)))


def _sequence_mixers(proj3, lb, s_hg, s_re, s_im, s5_prm, d_hg):
    bsz, length, _ = proj3.shape
    lpad = -(-length // SUBLANES) * SUBLANES
    o, s_hg_new = _hgrn(_pad_tokens(proj3, lpad) if lpad != length else proj3, lb, s_hg, valid=length)
    lc = S5_CHUNK if length % S5_CHUNK == 0 else length
    y, s_re_new, s_im_new = _s5(proj3[:, :, 4 * d_hg:], s_re, s_im, s5_prm[lc], lc=lc)
    return o[:, :length], y, s_hg_new, s_re_new, s_im_new


def kernel(x_prompt, x_sample, state_hgrn, state_ssm_re, state_ssm_im, meta_tokens, ln_emb_g, ln_emb_b,
           lb_logits, w_in, b_in, hg_norm_g, ssm_a_re, ssm_a_im, ssm_log_dt, ssm_b_re, ssm_b_im,
           ssm_c_re, ssm_c_im, ssm_d, w_glu, b_glu, w_out, ln1_g, ln1_b, peer_w_q, peer_keys,
           peer_u, peer_v, ln2_g, ln2_b):
    depth = w_in.shape[0]
    alpha = (2.0 * depth) ** 0.25
    bp, seq, d = x_prompt.shape
    bs, dseq, _ = x_sample.shape
    n_meta = meta_tokens.shape[0]
    heads = state_hgrn.shape[2]
    d_hg = heads * LANES
    groups, pstate = state_ssm_re.shape[2], state_ssm_re.shape[3]
    lbs = jnp.cumsum(jax.nn.softmax(lb_logits.astype(F32), axis=0), axis=0)

    xp = x_prompt.astype(F32).reshape(bp * seq, d)
    xm = meta_tokens.astype(F32)
    xs = x_sample.astype(F32).reshape(bs * dseq, d)
    n_s = bs * dseq
    hg_p, re_p, im_p, hg_s, re_s, im_s = [], [], [], [], [], []
    for l in range(depth):
        last = l == depth - 1
        w_in_b = w_in[l].astype(BF16)
        lb = lbs[l].reshape(1, d_hg)
        s5_prm = {lc: _s5_params(ssm_a_re[l], ssm_a_im[l], ssm_log_dt[l], ssm_b_re[l], ssm_b_im[l],
                                 ssm_c_re[l], ssm_c_im[l], ssm_d[l], lc)
                  for lc in {S5_CHUNK if n % S5_CHUNK == 0 else n for n in (n_meta, seq, dseq)}}
        wglu = w_glu[l].astype(BF16)
        wo_hg = w_out[l, :d_hg].astype(BF16)
        wo_ssm = w_out[l, d_hg:].astype(BF16)
        wq_t = peer_w_q[l].T.astype(BF16)
        keys = peer_keys[l].astype(BF16)
        u_tab = peer_u[l].astype(BF16)
        v_tab_t = peer_v[l].T.astype(BF16)

        xsm = jnp.concatenate([xs, xm], axis=0)
        x0_p, proj_p = _inproj(xp, ln_emb_g, ln_emb_b, w_in_b, b_in[l], apply_ln=(l == 0))
        x0_sm, proj_sm = _inproj(xsm, ln_emb_g, ln_emb_b, w_in_b, b_in[l], apply_ln=(l == 0))
        d_in = proj_p.shape[1]

        proj_m = proj_sm[n_s:].reshape(1, n_meta, d_in)
        proj_m8 = jnp.broadcast_to(proj_m, (SUBLANES, n_meta, d_in))
        zero_hg = jnp.zeros((SUBLANES, heads, LANES, LANES), F32)
        zero_ss = jnp.zeros((SUBLANES, groups, pstate), F32)
        o_m, y_m, hg_m, re_m, im_m = _sequence_mixers(proj_m8, lb, zero_hg, zero_ss, zero_ss, s5_prm, d_hg)

        o_p, y_p, shg, sre, sim = _sequence_mixers(
            proj_p.reshape(bp, seq, d_in), lb,
            jnp.broadcast_to(hg_m[:1], (bp,) + hg_m.shape[1:]),
            jnp.broadcast_to(re_m[:1], (bp,) + re_m.shape[1:]),
            jnp.broadcast_to(im_m[:1], (bp,) + im_m.shape[1:]), s5_prm, d_hg)
        hg_p.append(shg)
        re_p.append(sre)
        im_p.append(sim)

        o_s, y_s, shg, sre, sim = _sequence_mixers(
            proj_sm[:n_s].reshape(bs, dseq, d_in), lb, state_hgrn[l].astype(F32),
            state_ssm_re[l].astype(F32), state_ssm_im[l].astype(F32), s5_prm, d_hg)
        hg_s.append(shg)
        re_s.append(sre)
        im_s.append(sim)

        mix_args = (hg_norm_g[l], wglu, b_glu[l], wo_hg, wo_ssm, ln1_g[l], ln1_b[l])
        peer_args = (wq_t, keys, u_tab, v_tab_t, ln2_g[l], ln2_b[l])
        x1_p = _mix(x0_p, o_p.reshape(bp * seq, d_hg), proj_p, y_p.reshape(bp * seq, -1), *mix_args, alpha=alpha)
        xp = _peer(x1_p, *peer_args, alpha=alpha)
        x1_s = _mix(x0_sm[:n_s], o_s.reshape(n_s, d_hg), proj_sm[:n_s], y_s.reshape(n_s, -1), *mix_args, alpha=alpha)
        xs = _peer(x1_s, *peer_args, alpha=alpha)
        if not last:
            x1_m = _mix(jnp.broadcast_to(x0_sm[n_s:], (n_meta, d)), o_m[0], proj_sm[n_s:], y_m[0], *mix_args, alpha=alpha)
            xm = _peer(x1_m, *peer_args, alpha=alpha)

    y_prompt = xp.reshape(bp, seq, d).astype(x_prompt.dtype)
    y_sample = xs.reshape(bs, dseq, d).astype(x_sample.dtype)
    return (y_prompt, y_sample, jnp.stack(hg_p), jnp.stack(re_p), jnp.stack(im_p),
            jnp.stack(hg_s), jnp.stack(re_s), jnp.stack(im_s))
```

```python
import functools
import math

import jax
import jax.numpy as jnp
from jax import lax
from jax.experimental import pallas as pl
from jax.experimental.pallas import tpu as pltpu

F32 = jnp.float32
BF16 = jnp.bfloat16

LN_EPS = 1e-5
RMS_EPS = 1e-6
HG_CHUNK = 16
HG_CHUNKS_PER_ITER = 8
HG_SEQS_PER_STEP = 16
SSM_GROUP = 16
S5_CHUNK = 16
S5_GROUP_TILE = 8
S5_SEQS_PER_STEP = 4
PEER_TOPK = 16
LANES = 128
SUBLANES = 8
VMEM_LIMIT = 56 * 1024 * 1024
PEER_EXPERT_ROWS = 8


def _cparams(*sem):
    return pltpu.CompilerParams(dimension_semantics=sem, vmem_limit_bytes=VMEM_LIMIT)


def _layer_norm(x, g, b):
    mu = jnp.mean(x, axis=-1, keepdims=True)
    xc = x - mu
    var = jnp.mean(xc * xc, axis=-1, keepdims=True)
    return xc * lax.rsqrt(var + LN_EPS) * g + b


def _sigmoid(x):
    return 1.0 / (1.0 + jnp.exp(-x))


def _gelu(x):
    return 0.5 * x * (1.0 + jnp.tanh(math.sqrt(2.0 / math.pi) * (x + 0.044715 * (x * x * x))))


def _row_tile(t, target):
    best = None
    for cand in range(SUBLANES, min(t, target) + 1, SUBLANES):
        if t % cand == 0:
            best = cand
    return best if best is not None else t


def _inproj_kernel(x_ref, g_ref, b_ref, w_ref, bi_ref, x0_ref, proj_ref, *, apply_ln):
    x = x_ref[...]
    if apply_ln:
        x = _layer_norm(x, g_ref[...], b_ref[...])
    x0_ref[...] = x
    proj_ref[...] = jnp.dot(x.astype(BF16), w_ref[...], preferred_element_type=F32) + bi_ref[...]


def _inproj(x, g, b, w_bf16, bias, *, apply_ln):
    t, d = x.shape
    d_in = w_bf16.shape[1]
    tm = _row_tile(t, 512)
    return pl.pallas_call(
        functools.partial(_inproj_kernel, apply_ln=apply_ln),
        grid=(t // tm,),
        in_specs=[
            pl.BlockSpec((tm, d), lambda i: (i, 0)),
            pl.BlockSpec((1, d), lambda i: (0, 0)),
            pl.BlockSpec((1, d), lambda i: (0, 0)),
            pl.BlockSpec((d, d_in), lambda i: (0, 0)),
            pl.BlockSpec((1, d_in), lambda i: (0, 0)),
        ],
        out_specs=[
            pl.BlockSpec((tm, d), lambda i: (i, 0)),
            pl.BlockSpec((tm, d_in), lambda i: (i, 0)),
        ],
        out_shape=[jax.ShapeDtypeStruct((t, d), F32), jax.ShapeDtypeStruct((t, d_in), F32)],
        compiler_params=_cparams("parallel"),
        name="inproj",
    )(x, g.reshape(1, d), b.reshape(1, d), w_bf16, bias.reshape(1, d_in))


def _hgrn_kernel(q_ref, f_ref, v_ref, lb_ref, s0_ref, o_ref, s_ref, st_ref,
                 *, chunk, n_iters, per_iter, seqs, valid):
    lb = lb_ref[...]
    ones = jnp.ones((LANES, LANES), BF16)
    groups = seqs * per_iter
    shape = (groups, chunk, LANES)
    row = lax.broadcasted_iota(jnp.int32, shape, 1)
    span = per_iter * chunk
    nt = (((1,), (1,)), ((), ()))
    tn = (((0,), (0,)), ((), ()))

    for b in range(seqs):
        st_ref[b] = s0_ref[b].T

    def take(ref, r0):
        return jnp.concatenate([ref[b, pl.ds(r0, span), :].reshape(per_iter, chunk, LANES)
                                for b in range(seqs)], axis=0)

    def step(i, carry):
        r0 = pl.multiple_of(i * span, span)
        q = take(q_ref, r0)
        fp = take(f_ref, r0)
        v = take(v_ref, r0)
        f = lb + (1.0 - lb) * _sigmoid(fp)
        logf = jnp.log(f)
        kk = (1.0 - lb) * _sigmoid(-fp)
        if valid < chunk:
            live = row < valid
            logf = jnp.where(live, logf, 0.0)
            kk = jnp.where(live, kk, 0.0)
            q = jnp.where(live, q, 0.0)
            v = jnp.where(live, v, 0.0)
        pick = lambda a, s: jnp.broadcast_to(a[:, s:s + 1, :], shape)
        bc = jnp.zeros(shape, F32)
        for s in range(chunk):
            bc = bc + jnp.where(row >= s, pick(logf, s), 0.0)
        prods = []
        for s in range(chunk):
            e = jnp.where(row >= s, jnp.exp(bc - pick(bc, s)), 0.0)
            prods.append((q * e * pick(kk, s)).reshape(groups * chunk, LANES))
        p_all = jnp.concatenate(prods, axis=0).astype(BF16)
        att = jnp.dot(p_all, ones, preferred_element_type=F32)
        o = jnp.zeros(shape, F32)
        for s in range(chunk):
            att_s = att[s * groups * chunk:(s + 1) * groups * chunk, :].reshape(shape)
            o = o + att_s * pick(v, s)
        qs = (q * jnp.exp(bc)).astype(BF16)
        kt = (kk * jnp.exp(pick(bc, chunk - 1) - bc)).astype(BF16)
        vb = v.astype(BF16)
        decay = jnp.exp(bc[:, chunk - 1:chunk, :])
        outs = []
        for b in range(seqs):
            st = st_ref[b]
            for n in range(per_iter):
                g = b * per_iter + n
                outs.append(o[g] + lax.dot_general(qs[g], st.astype(BF16), nt, preferred_element_type=F32))
                st = st * decay[g] + lax.dot_general(vb[g], kt[g], tn, preferred_element_type=F32)
            st_ref[b] = st
        for b in range(seqs):
            ob = jnp.concatenate(outs[b * per_iter:(b + 1) * per_iter], axis=0)
            ob = ob * lax.rsqrt(jnp.mean(ob * ob, axis=-1, keepdims=True) + RMS_EPS)
            o_ref[b, pl.ds(r0, span), :] = ob
        return carry

    lax.fori_loop(0, n_iters, step, 0)
    for b in range(seqs):
        s_ref[b] = st_ref[b].T


def _hgrn(proj3, lb, s0, *, valid):
    bsz, length, _ = proj3.shape
    heads = s0.shape[1]
    chunk = HG_CHUNK if length % HG_CHUNK == 0 else length
    n_chunks = length // chunk
    divisor = lambda n, cap: max(c for c in range(1, cap + 1) if n % c == 0)
    if n_chunks > 1:
        seqs, per_iter = 1, divisor(n_chunks, HG_CHUNKS_PER_ITER)
    else:
        seqs, per_iter = divisor(bsz, HG_SEQS_PER_STEP), 1
    seq_spec = lambda off: pl.BlockSpec((seqs, length, LANES), lambda b, h: (b, 0, off + h))
    state_spec = pl.BlockSpec((seqs, None, LANES, LANES), lambda b, h: (b, h, 0, 0))
    return pl.pallas_call(
        functools.partial(_hgrn_kernel, chunk=chunk, n_iters=n_chunks // per_iter, per_iter=per_iter,
                          seqs=seqs, valid=min(valid, chunk)),
        grid=(bsz // seqs, heads),
        in_specs=[
            seq_spec(0), seq_spec(heads), seq_spec(2 * heads),
            pl.BlockSpec((1, LANES), lambda b, h: (0, h)),
            state_spec,
        ],
        out_specs=[
            pl.BlockSpec((seqs, length, LANES), lambda b, h: (b, 0, h)),
            state_spec,
        ],
        out_shape=[jax.ShapeDtypeStruct((bsz, length, heads * LANES), F32),
                   jax.ShapeDtypeStruct(s0.shape, F32)],
        scratch_shapes=[pltpu.VMEM((seqs, LANES, LANES), F32)],
        compiler_params=_cparams("parallel", "parallel"),
        name="hgrn2",
    )(proj3, proj3, proj3, lb, s0)


def _s5_kernel(u_ref, bb_ref, c_ref, are_ref, aim_ref, alre_ref, alim_ref, d_ref, h0re_ref, h0im_ref,
               y_ref, hre_ref, him_ref, hpre_ref, hpim_ref, *, lc, n_chunks, seqs):
    rows = seqs * n_chunks
    half = are_ref.shape[-1]
    a_re = are_ref[...]
    a_im = aim_ref[...]
    bb = bb_ref[...]

    def drive(t):
        u_t = u_ref[pl.ds(t, rows, stride=lc), :]
        return u_t, jnp.dot(u_t.astype(BF16), bb, preferred_element_type=F32)

    def advance(h_re, h_im, bu):
        return (a_re * h_re - a_im * h_im + bu[:, :half], a_re * h_im + a_im * h_re + bu[:, half:])

    h_re = jnp.zeros((rows, half), F32)
    h_im = jnp.zeros((rows, half), F32)
    for t in range(lc):
        h_re, h_im = advance(h_re, h_im, drive(t)[1])

    n_tiles = half // LANES

    def put(ref, idx, val):
        for k in range(n_tiles):
            ref[k, idx, :] = val[:, k * LANES:(k + 1) * LANES]

    def get(ref, idx):
        return jnp.concatenate([ref[k, idx, :] for k in range(n_tiles)], axis=-1)

    put(hpre_ref, slice(None), h_re)
    put(hpim_ref, slice(None), h_im)

    al_re = alre_ref[...]
    al_im = alim_ref[...]

    def carry_step(c, carry):
        c_re, c_im = carry
        idx = pl.ds(c, seqs, stride=n_chunks)
        l_re = get(hpre_ref, idx)
        l_im = get(hpim_ref, idx)
        put(hpre_ref, idx, c_re)
        put(hpim_ref, idx, c_im)
        return (al_re * c_re - al_im * c_im + l_re, al_re * c_im + al_im * c_re + l_im)

    c_re, c_im = lax.fori_loop(0, n_chunks, carry_step, (h0re_ref[...], h0im_ref[...]))
    hre_ref[...] = c_re
    him_ref[...] = c_im

    h_re = get(hpre_ref, slice(None))
    h_im = get(hpim_ref, slice(None))
    c_w = c_ref[...]
    d_vec = d_ref[...]
    for t in range(lc):
        u_t, bu = drive(t)
        h_re, h_im = advance(h_re, h_im, bu)
        y = (jnp.dot(h_re.astype(BF16), c_w[:half], preferred_element_type=F32)
             + jnp.dot(h_im.astype(BF16), c_w[half:], preferred_element_type=F32) + d_vec * u_t)
        y_ref[pl.ds(t, rows, stride=lc), :] = y


def _s5_params(a_re, a_im, log_dt, b_re, b_im, c_re, c_im, d, lc):
    groups, pstate = a_re.shape
    dt = jnp.exp(log_dt.astype(F32))[:, None]
    a_re = a_re.astype(F32)
    a_im = a_im.astype(F32)
    zr, zi = a_re * dt, a_im * dt
    mag = jnp.exp(zr)
    ab_re, ab_im = mag * jnp.cos(zi), mag * jnp.sin(zi)
    den = a_re * a_re + a_im * a_im
    nr = ab_re - 1.0
    coef_re = (nr * a_re + ab_im * a_im) / den
    coef_im = (ab_im * a_re - nr * a_im) / den
    b_re = b_re.astype(F32)
    b_im = b_im.astype(F32)
    bb_re = coef_re[..., None] * b_re - coef_im[..., None] * b_im
    bb_im = coef_re[..., None] * b_im + coef_im[..., None] * b_re
    pm = jnp.exp(zr * lc)
    al_re, al_im = pm * jnp.cos(zi * lc), pm * jnp.sin(zi * lc)
    gt = S5_GROUP_TILE
    tiles = groups // gt
    swap = lambda x: jnp.transpose(x, (0, 2, 1))
    bb = jnp.concatenate([_block_diag(swap(bb_re), gt), _block_diag(swap(bb_im), gt)], axis=-1)
    cw = jnp.concatenate([_block_diag(swap(c_re.astype(F32)), gt), -_block_diag(swap(c_im.astype(F32)), gt)], axis=1)
    lane = lambda x: x.reshape(tiles, 1, gt * pstate)
    return dict(bb=bb.astype(BF16), cw=cw.astype(BF16), are=lane(ab_re), aim=lane(ab_im),
                alre=lane(al_re), alim=lane(al_im), d=d.astype(F32).reshape(1, groups * SSM_GROUP))


def _block_diag(x, gm):
    if gm == 1:
        return x
    g, a, b = x.shape
    x = x.reshape(g // gm, gm, a, b)
    eye = jnp.eye(gm, dtype=x.dtype)
    return (x[:, :, :, None, :] * eye[None, :, None, :, None]).reshape(g // gm, gm * a, gm * b)


def _s5(proj, h0_re, h0_im, prm, *, length, lc, u_col):
    bsz, n_groups, pstate = h0_re.shape
    d_ssm = n_groups * SSM_GROUP
    tiles = n_groups // S5_GROUP_TILE
    half = S5_GROUP_TILE * pstate
    n_chunks = length // lc
    seqs = bsz if n_chunks == 1 else min(bsz, S5_SEQS_PER_STEP)
    steps = bsz // seqs
    rows = seqs * n_chunks
    h0r = h0_re.reshape(steps, seqs, n_groups * pstate)
    h0i = h0_im.reshape(steps, seqs, n_groups * pstate)
    tspec = lambda a, b: pl.BlockSpec((None, a, b), lambda i, g: (g, 0, 0))
    sspec = pl.BlockSpec((None, seqs, half), lambda i, g: (i, 0, g))
    y, hre, him = pl.pallas_call(
        functools.partial(_s5_kernel, lc=lc, n_chunks=n_chunks, seqs=seqs),
        grid=(steps, tiles),
        in_specs=[
            pl.BlockSpec((seqs * length, LANES), lambda i, g: (i, u_col // LANES + g)),
            tspec(LANES, 2 * half), tspec(2 * half, LANES),
            tspec(1, half), tspec(1, half), tspec(1, half), tspec(1, half),
            pl.BlockSpec((1, LANES), lambda i, g: (0, g)),
            sspec, sspec,
        ],
        out_specs=[
            pl.BlockSpec((seqs * length, LANES), lambda i, g: (i, g)),
            sspec, sspec,
        ],
        out_shape=[jax.ShapeDtypeStruct((bsz * length, d_ssm), F32),
                   jax.ShapeDtypeStruct(h0r.shape, F32),
                   jax.ShapeDtypeStruct(h0r.shape, F32)],
        scratch_shapes=[pltpu.VMEM((half // LANES, rows, LANES), F32),
                        pltpu.VMEM((half // LANES, rows, LANES), F32)],
        compiler_params=_cparams("parallel", "parallel"),
        name="s5",
    )(proj, prm['bb'], prm['cw'], prm['are'], prm['aim'], prm['alre'], prm['alim'], prm['d'], h0r, h0i)
    return y, hre.reshape(h0_re.shape), him.reshape(h0_re.shape)


def _mix_kernel(x_ref, o_ref, gate_ref, y_ref, hg_g_ref, wglu_ref, bglu_ref, wo_hg_ref, wo_ssm_ref,
                g_ref, b_ref, out_ref, *, alpha):
    gate = gate_ref[...]
    o_hg = o_ref[...] * hg_g_ref[...] * (gate * _sigmoid(gate))
    z = _gelu(y_ref[...])
    glu = jnp.dot(z.astype(BF16), wglu_ref[...], preferred_element_type=F32) + bglu_ref[...]
    o_ssm = z * _sigmoid(glu)
    mix = (jnp.dot(o_hg.astype(BF16), wo_hg_ref[...], preferred_element_type=F32)
           + jnp.dot(o_ssm.astype(BF16), wo_ssm_ref[...], preferred_element_type=F32))
    out_ref[...] = _layer_norm(alpha * x_ref[...] + mix, g_ref[...], b_ref[...])


def _mix(x0, o_hg, proj, y_ssm, hg_g, wglu, bglu, wo_hg, wo_ssm, g, b, *, alpha):
    t, d = x0.shape
    d_hg = o_hg.shape[1]
    d_ssm = y_ssm.shape[1]
    tm = _row_tile(t, 512)
    gate_block = (3 * d_hg) // d_hg
    row = lambda w: pl.BlockSpec((tm, w), lambda i: (i, 0))
    full = lambda a, bb: pl.BlockSpec((a, bb), lambda i: (0, 0))
    return pl.pallas_call(
        functools.partial(_mix_kernel, alpha=alpha),
        grid=(t // tm,),
        in_specs=[
            row(d), row(d_hg),
            pl.BlockSpec((tm, d_hg), lambda i: (i, gate_block)),
            row(d_ssm),
            full(1, d_hg), full(d_ssm, d_ssm), full(1, d_ssm), full(d_hg, d), full(d_ssm, d),
            full(1, d), full(1, d),
        ],
        out_specs=row(d),
        out_shape=jax.ShapeDtypeStruct((t, d), F32),
        compiler_params=_cparams("parallel"),
        name="mix",
    )(x0, o_hg, proj, y_ssm, hg_g.reshape(1, d_hg), wglu, bglu.reshape(1, d_ssm), wo_hg, wo_ssm,
      g.reshape(1, d), b.reshape(1, d))


def _top_rows(s, k):
    n, t = s.shape
    iota = lax.broadcasted_iota(jnp.int32, (n, t), 0)
    krow = lax.broadcasted_iota(jnp.int32, (k, t), 0)
    rank = jnp.full((n, t), float(k), F32)
    vals = jnp.zeros((k, t), F32)
    for a in range(k):
        m = jnp.max(s, axis=0, keepdims=True)
        idx = jnp.min(jnp.where(s == m, iota, n), axis=0, keepdims=True)
        hit = iota == idx
        rank = jnp.where(hit, float(a), rank)
        s = jnp.where(hit, -jnp.inf, s)
        vals = jnp.where(krow == a, jnp.broadcast_to(m, (k, t)), vals)
    return vals, rank


def _candidate_pieces(t1, t2):
    k = PEER_TOPK
    t = t1.shape[1]
    bc = lambda r, n: jnp.broadcast_to(r, (n, t))
    pieces = [(bc(t1[0:1], k), t2, [b for b in range(k)], [True] * k)]
    half = k // 2
    for a in range(1, half):
        nb = k // (a + 1)
        pieces.append((bc(t1[a:a + 1], half), t2[0:half], [a * k + b for b in range(half)],
                       [b < nb for b in range(half)]))
    pieces.append((t1[half:k], bc(t2[0:1], half), [(half + r) * k for r in range(half)], [True] * half))
    return pieces


def _candidate_index_rows(t):
    k = PEER_TOPK
    dummy = jnp.zeros((k, 1), F32)
    vals = [i if ok else k * k for _, _, idx, val in _candidate_pieces(dummy, dummy) for i, ok in zip(idx, val)]
    return jnp.broadcast_to(jnp.asarray(vals, jnp.int32)[:, None], (len(vals), t))


def _peer_select(s1, s2, cidx):
    k = PEER_TOPK
    t = s1.shape[1]
    big = k * k
    top1, rank1 = _top_rows(s1, k)
    top2, rank2 = _top_rows(s2, k)
    pieces = _candidate_pieces(top1, top2)
    cand = jnp.concatenate([x + y for x, y, _, _ in pieces], axis=0)
    cand = jnp.where(cidx < big, cand, -jnp.inf)
    chosen = jnp.zeros(cand.shape, F32)
    for _ in range(k):
        m = jnp.max(cand, axis=0, keepdims=True)
        idx = jnp.min(jnp.where(cand == m, cidx, big), axis=0, keepdims=True)
        hit = cidx == idx
        chosen = jnp.where(hit, 1.0, chosen)
        cand = jnp.where(hit, -jnp.inf, cand)
    e1 = jnp.exp(top1 - jnp.broadcast_to(top1[0:1], (k, t)))
    e2 = jnp.exp(top2 - jnp.broadcast_to(top2[0:1], (k, t)))
    ecand = jnp.concatenate([x * y for x, y, _, _ in _candidate_pieces(e1, e2)], axis=0)
    z = jnp.sum(chosen * ecand, axis=0, keepdims=True)
    counts = []
    off = 0
    half = k // 2
    for pi, (x, _, _, _) in enumerate(pieces):
        rows = x.shape[0]
        blk = chosen[off:off + rows]
        off += rows
        if pi < len(pieces) - 1:
            counts.append(jnp.sum(blk, axis=0, keepdims=True))
        else:
            counts.extend(blk[r:r + 1] for r in range(half))
    nk = s1.shape[0]
    c1 = jnp.zeros((nk, t), F32)
    for a in range(k):
        c1 = jnp.where(rank1 == float(a), jnp.broadcast_to(counts[a], (nk, t)), c1)
    phi = jnp.exp(s1 - jnp.broadcast_to(top1[0:1], (nk, t))) * jnp.broadcast_to(1.0 / z, (nk, t))
    psi = jnp.exp(s2 - jnp.broadcast_to(top2[0:1], (nk, t)))
    return c1, phi, rank2, psi


def _rows_bf16(row, n):
    pack = 2 * SUBLANES
    tile = jnp.broadcast_to(row, (pack, row.shape[1])).astype(BF16)
    return jnp.concatenate([tile] * (n // pack), axis=0)


def _peer_kernel(x_ref, wq_ref, keys_ref, cidx_ref, u0_ref, u1_ref, u2_ref, vt_ref, g_ref, b_ref, out_ref,
                 xb_ref, c1_ref, phi_ref, r2_ref, psi_ref, uta_ref, utb_ref, wa_ref, wb_ref, acc_ref,
                 *, alpha, heads, nk, rows_per_step):
    j = pl.program_id(1)
    nt = (((1,), (1,)), ((), ()))
    eb = rows_per_step * nk

    @pl.when(j == 0)
    def _():
        xb = x_ref[...].astype(BF16)
        xb_ref[...] = xb
        acc_ref[...] = jnp.zeros_like(acc_ref)
        for h in range(heads):
            sc = []
            for c in range(2):
                hc = h * 2 + c
                qt = lax.dot_general(wq_ref[hc * LANES:(hc + 1) * LANES, :], xb, nt, preferred_element_type=F32)
                sc.append(jnp.dot(keys_ref[hc], qt.astype(BF16), preferred_element_type=F32))
            c1, phi, r2, psi = _peer_select(sc[0], sc[1], cidx_ref[...])
            c1_ref[h] = c1
            phi_ref[h] = phi
            r2_ref[h] = r2.astype(BF16)
            psi_ref[h] = psi.astype(BF16)
        uta_ref[...] = lax.dot_general(u0_ref[...], xb, nt, preferred_element_type=F32)

    xb = xb_ref[...]

    def mix_block(block, ut_ref, w_ref):
        for r in range(rows_per_step):
            n1 = block * rows_per_step + r
            gsum = None
            for h in range(heads):
                c1row = _rows_bf16(c1_ref[h, pl.ds(n1, 1), :], nk)
                phirow = _rows_bf16(phi_ref[h, pl.ds(n1, 1), :], nk)
                term = jnp.where(r2_ref[h] < c1row, phirow * psi_ref[h], jnp.zeros((), BF16))
                gsum = term if gsum is None else gsum + term
            act = _gelu(ut_ref[r * nk:(r + 1) * nk, :]).astype(BF16)
            w_ref[r * nk:(r + 1) * nk, :] = gsum * act

    utb_ref[...] = lax.dot_general(u1_ref[...], xb, nt, preferred_element_type=F32)
    mix_block(2 * j, uta_ref, wa_ref)
    acc_ref[...] += jnp.dot(vt_ref[:, :eb], wa_ref[...], preferred_element_type=F32)
    uta_ref[...] = lax.dot_general(u2_ref[...], xb, nt, preferred_element_type=F32)
    mix_block(2 * j + 1, utb_ref, wb_ref)
    acc_ref[...] += jnp.dot(vt_ref[:, eb:], wb_ref[...], preferred_element_type=F32)

    @pl.when(j == pl.num_programs(1) - 1)
    def _():
        ffn = acc_ref[...].T
        out_ref[...] = _layer_norm(alpha * x_ref[...] + ffn, g_ref[...], b_ref[...])


def _peer(x, wq_t, keys, u_tab, v_tab_t, g, b, *, alpha):
    t, d = x.shape
    heads, _, nk, dh = keys.shape
    n_exp = u_tab.shape[0]
    tm = 512 if t % 512 == 0 else _row_tile(t, 512)
    rows = PEER_EXPERT_ROWS
    eb = rows * nk
    keys2 = keys.reshape(heads * 2, nk, dh)
    cidx = _candidate_index_rows(tm)
    n_blocks = n_exp // eb
    return pl.pallas_call(
        functools.partial(_peer_kernel, alpha=alpha, heads=heads, nk=nk, rows_per_step=rows),
        grid=(t // tm, n_blocks // 2),
        in_specs=[
            pl.BlockSpec((tm, d), lambda i, j: (i, 0)),
            pl.BlockSpec(wq_t.shape, lambda i, j: (0, 0)),
            pl.BlockSpec(keys2.shape, lambda i, j: (0, 0, 0)),
            pl.BlockSpec(cidx.shape, lambda i, j: (0, 0)),
            pl.BlockSpec((eb, d), lambda i, j: (0, 0)),
            pl.BlockSpec((eb, d), lambda i, j: (2 * j + 1, 0)),
            pl.BlockSpec((eb, d), lambda i, j: ((2 * j + 2) % n_blocks, 0)),
            pl.BlockSpec((d, 2 * eb), lambda i, j: (0, j)),
            pl.BlockSpec((1, d), lambda i, j: (0, 0)),
            pl.BlockSpec((1, d), lambda i, j: (0, 0)),
        ],
        out_specs=pl.BlockSpec((tm, d), lambda i, j: (i, 0)),
        out_shape=jax.ShapeDtypeStruct((t, d), F32),
        scratch_shapes=[
            pltpu.VMEM((tm, d), BF16),
            pltpu.VMEM((heads, nk, tm), F32), pltpu.VMEM((heads, nk, tm), F32),
            pltpu.VMEM((heads, nk, tm), BF16), pltpu.VMEM((heads, nk, tm), BF16),
            pltpu.VMEM((eb, tm), F32), pltpu.VMEM((eb, tm), F32),
            pltpu.VMEM((eb, tm), BF16), pltpu.VMEM((eb, tm), BF16),
            pltpu.VMEM((d, tm), F32),
        ],
        compiler_params=_cparams("parallel", "arbitrary"),
        name="peer",
    )(x, wq_t, keys2, cidx, u_tab, u_tab, u_tab, v_tab_t, g.reshape(1, d), b.reshape(1, d))


def _pad_tokens(a, length):
    return jnp.pad(a, ((0, 0), (0, length - a.shape[1]), (0, 0)))


def _sequence_mixers(proj3, lb, s_hg, s_re, s_im, s5_prm, d_hg):
    bsz, length, _ = proj3.shape
    lpad = -(-length // SUBLANES) * SUBLANES
    o, s_hg_new = _hgrn(_pad_tokens(proj3, lpad) if lpad != length else proj3, lb, s_hg, valid=length)
    lc = S5_CHUNK if length % S5_CHUNK == 0 else length
    y, s_re_new, s_im_new = _s5(proj3.reshape(bsz * length, -1), s_re, s_im, s5_prm[lc],
                                length=length, lc=lc, u_col=4 * d_hg)
    return o[:, :length], y.reshape(bsz, length, -1), s_hg_new, s_re_new, s_im_new


def kernel(x_prompt, x_sample, state_hgrn, state_ssm_re, state_ssm_im, meta_tokens, ln_emb_g, ln_emb_b,
           lb_logits, w_in, b_in, hg_norm_g, ssm_a_re, ssm_a_im, ssm_log_dt, ssm_b_re, ssm_b_im,
           ssm_c_re, ssm_c_im, ssm_d, w_glu, b_glu, w_out, ln1_g, ln1_b, peer_w_q, peer_keys,
           peer_u, peer_v, ln2_g, ln2_b):
    depth = w_in.shape[0]
    alpha = (2.0 * depth) ** 0.25
    bp, seq, d = x_prompt.shape
    bs, dseq, _ = x_sample.shape
    n_meta = meta_tokens.shape[0]
    heads = state_hgrn.shape[2]
    d_hg = heads * LANES
    groups, pstate = state_ssm_re.shape[2], state_ssm_re.shape[3]
    lbs = jnp.cumsum(jax.nn.softmax(lb_logits.astype(F32), axis=0), axis=0)

    xp = x_prompt.astype(F32).reshape(bp * seq, d)
    xm = meta_tokens.astype(F32)
    xs = x_sample.astype(F32).reshape(bs * dseq, d)
    n_s = bs * dseq
    hg_p, re_p, im_p, hg_s, re_s, im_s = [], [], [], [], [], []
    for l in range(depth):
        last = l == depth - 1
        w_in_b = w_in[l].astype(BF16)
        lb = lbs[l].reshape(1, d_hg)
        s5_prm = {lc: _s5_params(ssm_a_re[l], ssm_a_im[l], ssm_log_dt[l], ssm_b_re[l], ssm_b_im[l],
                                 ssm_c_re[l], ssm_c_im[l], ssm_d[l], lc)
                  for lc in {S5_CHUNK if n % S5_CHUNK == 0 else n for n in (n_meta, seq, dseq)}}
        wglu = w_glu[l].astype(BF16)
        wo_hg = w_out[l, :d_hg].astype(BF16)
        wo_ssm = w_out[l, d_hg:].astype(BF16)
        wq_t = peer_w_q[l].T.astype(BF16)
        keys = peer_keys[l].astype(BF16)
        u_tab = peer_u[l].astype(BF16)
        v_tab_t = peer_v[l].T.astype(BF16)

        xsm = jnp.concatenate([xs, xm], axis=0)
        x0_p, proj_p = _inproj(xp, ln_emb_g, ln_emb_b, w_in_b, b_in[l], apply_ln=(l == 0))
        x0_sm, proj_sm = _inproj(xsm, ln_emb_g, ln_emb_b, w_in_b, b_in[l], apply_ln=(l == 0))
        d_in = proj_p.shape[1]

        proj_m = proj_sm[n_s:].reshape(1, n_meta, d_in)
        proj_m8 = jnp.broadcast_to(proj_m, (SUBLANES, n_meta, d_in))
        zero_hg = jnp.zeros((SUBLANES, heads, LANES, LANES), F32)
        zero_ss = jnp.zeros((SUBLANES, groups, pstate), F32)
        o_m, y_m, hg_m, re_m, im_m = _sequence_mixers(proj_m8, lb, zero_hg, zero_ss, zero_ss, s5_prm, d_hg)

        o_p, y_p, shg, sre, sim = _sequence_mixers(
            proj_p.reshape(bp, seq, d_in), lb,
            jnp.broadcast_to(hg_m[:1], (bp,) + hg_m.shape[1:]),
            jnp.broadcast_to(re_m[:1], (bp,) + re_m.shape[1:]),
            jnp.broadcast_to(im_m[:1], (bp,) + im_m.shape[1:]), s5_prm, d_hg)
        hg_p.append(shg)
        re_p.append(sre)
        im_p.append(sim)

        o_s, y_s, shg, sre, sim = _sequence_mixers(
            proj_sm[:n_s].reshape(bs, dseq, d_in), lb, state_hgrn[l].astype(F32),
            state_ssm_re[l].astype(F32), state_ssm_im[l].astype(F32), s5_prm, d_hg)
        hg_s.append(shg)
        re_s.append(sre)
        im_s.append(sim)

        mix_args = (hg_norm_g[l], wglu, b_glu[l], wo_hg, wo_ssm, ln1_g[l], ln1_b[l])
        peer_args = (wq_t, keys, u_tab, v_tab_t, ln2_g[l], ln2_b[l])
        x1_p = _mix(x0_p, o_p.reshape(bp * seq, d_hg), proj_p, y_p.reshape(bp * seq, -1), *mix_args, alpha=alpha)
        xp = _peer(x1_p, *peer_args, alpha=alpha)
        x1_s = _mix(x0_sm[:n_s], o_s.reshape(n_s, d_hg), proj_sm[:n_s], y_s.reshape(n_s, -1), *mix_args, alpha=alpha)
        xs = _peer(x1_s, *peer_args, alpha=alpha)
        if not last:
            x1_m = _mix(jnp.broadcast_to(x0_sm[n_s:], (n_meta, d)), o_m[0], proj_sm[n_s:], y_m[0], *mix_args, alpha=alpha)
            xm = _peer(x1_m, *peer_args, alpha=alpha)

    y_prompt = xp.reshape(bp, seq, d).astype(x_prompt.dtype)
    y_sample = xs.reshape(bs, dseq, d).astype(x_sample.dtype)
    return (y_prompt, y_sample, jnp.stack(hg_p), jnp.stack(re_p), jnp.stack(im_p),
            jnp.stack(hg_s), jnp.stack(re_s), jnp.stack(im_s))
```

```python
import functools
import math

import jax
import jax.numpy as jnp
from jax import lax
from jax.experimental import pallas as pl
from jax.experimental.pallas import tpu as pltpu

F32 = jnp.float32
BF16 = jnp.bfloat16

LN_EPS = 1e-5
RMS_EPS = 1e-6
HG_CHUNK = 16
HG_CHUNKS_PER_ITER = 8
HG_SEQS_PER_STEP = 16
SSM_GROUP = 16
S5_CHUNK = 16
S5_GROUP_TILE = 8
S5_SEQS_PER_STEP = 4
PEER_TOPK = 16
LANES = 128
SUBLANES = 8
VMEM_LIMIT = 56 * 1024 * 1024
PEER_EXPERT_ROWS = 8


def _cparams(*sem):
    return pltpu.CompilerParams(dimension_semantics=sem, vmem_limit_bytes=VMEM_LIMIT)


def _layer_norm(x, g, b):
    mu = jnp.mean(x, axis=-1, keepdims=True)
    xc = x - mu
    var = jnp.mean(xc * xc, axis=-1, keepdims=True)
    return xc * lax.rsqrt(var + LN_EPS) * g + b


def _sigmoid(x):
    return 1.0 / (1.0 + jnp.exp(-x))


def _gelu(x):
    return 0.5 * x * (1.0 + jnp.tanh(math.sqrt(2.0 / math.pi) * (x + 0.044715 * (x * x * x))))


def _row_tile(t, target):
    best = None
    for cand in range(SUBLANES, min(t, target) + 1, SUBLANES):
        if t % cand == 0:
            best = cand
    return best if best is not None else t


def _inproj_kernel(x_ref, g_ref, b_ref, w_ref, bi_ref, x0_ref, proj_ref, *, apply_ln):
    x = x_ref[...]
    if apply_ln:
        x = _layer_norm(x, g_ref[...], b_ref[...])
    x0_ref[...] = x
    proj_ref[...] = jnp.dot(x.astype(BF16), w_ref[...], preferred_element_type=F32) + bi_ref[...]


def _inproj(x, g, b, w_bf16, bias, *, apply_ln):
    t, d = x.shape
    d_in = w_bf16.shape[1]
    tm = _row_tile(t, 512)
    return pl.pallas_call(
        functools.partial(_inproj_kernel, apply_ln=apply_ln),
        grid=(t // tm,),
        in_specs=[
            pl.BlockSpec((tm, d), lambda i: (i, 0)),
            pl.BlockSpec((1, d), lambda i: (0, 0)),
            pl.BlockSpec((1, d), lambda i: (0, 0)),
            pl.BlockSpec((d, d_in), lambda i: (0, 0)),
            pl.BlockSpec((1, d_in), lambda i: (0, 0)),
        ],
        out_specs=[
            pl.BlockSpec((tm, d), lambda i: (i, 0)),
            pl.BlockSpec((tm, d_in), lambda i: (i, 0)),
        ],
        out_shape=[jax.ShapeDtypeStruct((t, d), F32), jax.ShapeDtypeStruct((t, d_in), F32)],
        compiler_params=_cparams("parallel"),
        name="inproj",
    )(x, g.reshape(1, d), b.reshape(1, d), w_bf16, bias.reshape(1, d_in))


def _hgrn_kernel(q_ref, f_ref, v_ref, lb_ref, s0_ref, o_ref, s_ref, st_ref,
                 *, chunk, n_iters, per_iter, seqs, valid):
    lb = lb_ref[...]
    ones = jnp.ones((LANES, LANES), BF16)
    groups = seqs * per_iter
    shape = (groups, chunk, LANES)
    row = lax.broadcasted_iota(jnp.int32, shape, 1)
    span = per_iter * chunk
    nt = (((1,), (1,)), ((), ()))
    tn = (((0,), (0,)), ((), ()))

    for b in range(seqs):
        st_ref[b] = s0_ref[b].T

    def take(ref, r0):
        return jnp.concatenate([ref[b, pl.ds(r0, span), :].reshape(per_iter, chunk, LANES)
                                for b in range(seqs)], axis=0)

    def step(i, carry):
        r0 = pl.multiple_of(i * span, span)
        q = take(q_ref, r0)
        fp = take(f_ref, r0)
        v = take(v_ref, r0)
        f = lb + (1.0 - lb) * _sigmoid(fp)
        logf = jnp.log(f)
        kk = (1.0 - lb) * _sigmoid(-fp)
        if valid < chunk:
            live = row < valid
            logf = jnp.where(live, logf, 0.0)
            kk = jnp.where(live, kk, 0.0)
            q = jnp.where(live, q, 0.0)
            v = jnp.where(live, v, 0.0)
        pick = lambda a, s: jnp.broadcast_to(a[:, s:s + 1, :], shape)
        bc = jnp.zeros(shape, F32)
        for s in range(chunk):
            bc = bc + jnp.where(row >= s, pick(logf, s), 0.0)
        prods = []
        for s in range(chunk):
            e = jnp.where(row >= s, jnp.exp(bc - pick(bc, s)), 0.0)
            prods.append((q * e * pick(kk, s)).reshape(groups * chunk, LANES))
        p_all = jnp.concatenate(prods, axis=0).astype(BF16)
        att = jnp.dot(p_all, ones, preferred_element_type=F32)
        o = jnp.zeros(shape, F32)
        for s in range(chunk):
            att_s = att[s * groups * chunk:(s + 1) * groups * chunk, :].reshape(shape)
            o = o + att_s * pick(v, s)
        qs = (q * jnp.exp(bc)).astype(BF16)
        kt = (kk * jnp.exp(pick(bc, chunk - 1) - bc)).astype(BF16)
        vb = v.astype(BF16)
        decay = jnp.exp(bc[:, chunk - 1:chunk, :])
        ds = [lax.dot_general(vb[g], kt[g], tn, preferred_element_type=F32) for g in range(groups)]
        outs = []
        for b in range(seqs):
            st = st_ref[b]
            before = []
            for n in range(per_iter):
                g = b * per_iter + n
                before.append(st.astype(BF16))
                st = st * decay[g] + ds[g]
            st_ref[b] = st
            for n in range(per_iter):
                g = b * per_iter + n
                outs.append(o[g] + lax.dot_general(qs[g], before[n], nt, preferred_element_type=F32))
        for b in range(seqs):
            ob = jnp.concatenate(outs[b * per_iter:(b + 1) * per_iter], axis=0)
            ob = ob * lax.rsqrt(jnp.mean(ob * ob, axis=-1, keepdims=True) + RMS_EPS)
            o_ref[b, pl.ds(r0, span), :] = ob
        return carry

    lax.fori_loop(0, n_iters, step, 0)
    for b in range(seqs):
        s_ref[b] = st_ref[b].T


def _hgrn(proj3, lb, s0, *, valid):
    bsz, length, _ = proj3.shape
    heads = s0.shape[1]
    chunk = HG_CHUNK if length % HG_CHUNK == 0 else length
    n_chunks = length // chunk
    divisor = lambda n, cap: max(c for c in range(1, cap + 1) if n % c == 0)
    if n_chunks > 1:
        seqs, per_iter = 1, divisor(n_chunks, HG_CHUNKS_PER_ITER)
    else:
        seqs, per_iter = divisor(bsz, HG_SEQS_PER_STEP), 1
    seq_spec = lambda off: pl.BlockSpec((seqs, length, LANES), lambda b, h: (b, 0, off + h))
    state_spec = pl.BlockSpec((seqs, None, LANES, LANES), lambda b, h: (b, h, 0, 0))
    return pl.pallas_call(
        functools.partial(_hgrn_kernel, chunk=chunk, n_iters=n_chunks // per_iter, per_iter=per_iter,
                          seqs=seqs, valid=min(valid, chunk)),
        grid=(bsz // seqs, heads),
        in_specs=[
            seq_spec(0), seq_spec(heads), seq_spec(2 * heads),
            pl.BlockSpec((1, LANES), lambda b, h: (0, h)),
            state_spec,
        ],
        out_specs=[
            pl.BlockSpec((seqs, length, LANES), lambda b, h: (b, 0, h)),
            state_spec,
        ],
        out_shape=[jax.ShapeDtypeStruct((bsz, length, heads * LANES), F32),
                   jax.ShapeDtypeStruct(s0.shape, F32)],
        scratch_shapes=[pltpu.VMEM((seqs, LANES, LANES), F32)],
        compiler_params=_cparams("parallel", "parallel"),
        name="hgrn2",
    )(proj3, proj3, proj3, lb, s0)


def _s5_kernel(u_ref, bb_ref, c_ref, are_ref, aim_ref, alre_ref, alim_ref, d_ref, h0re_ref, h0im_ref,
               y_ref, hre_ref, him_ref, hpre_ref, hpim_ref, *, lc, n_chunks, seqs):
    rows = seqs * n_chunks
    half = are_ref.shape[-1]
    a_re = are_ref[...]
    a_im = aim_ref[...]
    bb = bb_ref[...]

    def drive(t):
        u_t = u_ref[pl.ds(t, rows, stride=lc), :]
        return u_t, jnp.dot(u_t.astype(BF16), bb, preferred_element_type=F32)

    def advance(h_re, h_im, bu):
        return (a_re * h_re - a_im * h_im + bu[:, :half], a_re * h_im + a_im * h_re + bu[:, half:])

    zeros = jnp.zeros((rows, half), F32)
    h_re, h_im = lax.fori_loop(0, lc, lambda t, h: advance(h[0], h[1], drive(t)[1]), (zeros, zeros))

    n_tiles = half // LANES

    def put(ref, idx, val):
        for k in range(n_tiles):
            ref[k, idx, :] = val[:, k * LANES:(k + 1) * LANES]

    def get(ref, idx):
        return jnp.concatenate([ref[k, idx, :] for k in range(n_tiles)], axis=-1)

    put(hpre_ref, slice(None), h_re)
    put(hpim_ref, slice(None), h_im)

    al_re = alre_ref[...]
    al_im = alim_ref[...]

    def carry_step(c, carry):
        c_re, c_im = carry
        idx = pl.ds(c, seqs, stride=n_chunks)
        l_re = get(hpre_ref, idx)
        l_im = get(hpim_ref, idx)
        put(hpre_ref, idx, c_re)
        put(hpim_ref, idx, c_im)
        return (al_re * c_re - al_im * c_im + l_re, al_re * c_im + al_im * c_re + l_im)

    c_re, c_im = lax.fori_loop(0, n_chunks, carry_step, (h0re_ref[...], h0im_ref[...]))
    hre_ref[...] = c_re
    him_ref[...] = c_im

    h_re = get(hpre_ref, slice(None))
    h_im = get(hpim_ref, slice(None))
    c_w = c_ref[...]
    d_vec = d_ref[...]

    def emit(t, h):
        u_t, bu = drive(t)
        h_re, h_im = advance(h[0], h[1], bu)
        y = (jnp.dot(h_re.astype(BF16), c_w[:half], preferred_element_type=F32)
             + jnp.dot(h_im.astype(BF16), c_w[half:], preferred_element_type=F32) + d_vec * u_t)
        y_ref[pl.ds(t, rows, stride=lc), :] = y
        return h_re, h_im

    lax.fori_loop(0, lc, emit, (h_re, h_im))


def _s5_params(a_re, a_im, log_dt, b_re, b_im, c_re, c_im, d, lc):
    groups, pstate = a_re.shape
    dt = jnp.exp(log_dt.astype(F32))[:, None]
    a_re = a_re.astype(F32)
    a_im = a_im.astype(F32)
    zr, zi = a_re * dt, a_im * dt
    mag = jnp.exp(zr)
    ab_re, ab_im = mag * jnp.cos(zi), mag * jnp.sin(zi)
    den = a_re * a_re + a_im * a_im
    nr = ab_re - 1.0
    coef_re = (nr * a_re + ab_im * a_im) / den
    coef_im = (ab_im * a_re - nr * a_im) / den
    b_re = b_re.astype(F32)
    b_im = b_im.astype(F32)
    bb_re = coef_re[..., None] * b_re - coef_im[..., None] * b_im
    bb_im = coef_re[..., None] * b_im + coef_im[..., None] * b_re
    pm = jnp.exp(zr * lc)
    al_re, al_im = pm * jnp.cos(zi * lc), pm * jnp.sin(zi * lc)
    gt = S5_GROUP_TILE
    tiles = groups // gt
    swap = lambda x: jnp.transpose(x, (0, 2, 1))
    bb = jnp.concatenate([_block_diag(swap(bb_re), gt), _block_diag(swap(bb_im), gt)], axis=-1)
    cw = jnp.concatenate([_block_diag(swap(c_re.astype(F32)), gt), -_block_diag(swap(c_im.astype(F32)), gt)], axis=1)
    lane = lambda x: x.reshape(tiles, 1, gt * pstate)
    return dict(bb=bb.astype(BF16), cw=cw.astype(BF16), are=lane(ab_re), aim=lane(ab_im),
                alre=lane(al_re), alim=lane(al_im), d=d.astype(F32).reshape(1, groups * SSM_GROUP))


def _block_diag(x, gm):
    if gm == 1:
        return x
    g, a, b = x.shape
    x = x.reshape(g // gm, gm, a, b)
    eye = jnp.eye(gm, dtype=x.dtype)
    return (x[:, :, :, None, :] * eye[None, :, None, :, None]).reshape(g // gm, gm * a, gm * b)


def _s5(proj, h0_re, h0_im, prm, *, length, lc, u_col):
    bsz, n_groups, pstate = h0_re.shape
    d_ssm = n_groups * SSM_GROUP
    tiles = n_groups // S5_GROUP_TILE
    half = S5_GROUP_TILE * pstate
    n_chunks = length // lc
    seqs = bsz if n_chunks == 1 else min(bsz, S5_SEQS_PER_STEP)
    steps = bsz // seqs
    rows = seqs * n_chunks
    h0r = h0_re.reshape(steps, seqs, n_groups * pstate)
    h0i = h0_im.reshape(steps, seqs, n_groups * pstate)
    tspec = lambda a, b: pl.BlockSpec((None, a, b), lambda i, g: (g, 0, 0))
    sspec = pl.BlockSpec((None, seqs, half), lambda i, g: (i, 0, g))
    y, hre, him = pl.pallas_call(
        functools.partial(_s5_kernel, lc=lc, n_chunks=n_chunks, seqs=seqs),
        grid=(steps, tiles),
        in_specs=[
            pl.BlockSpec((seqs * length, LANES), lambda i, g: (i, u_col // LANES + g)),
            tspec(LANES, 2 * half), tspec(2 * half, LANES),
            tspec(1, half), tspec(1, half), tspec(1, half), tspec(1, half),
            pl.BlockSpec((1, LANES), lambda i, g: (0, g)),
            sspec, sspec,
        ],
        out_specs=[
            pl.BlockSpec((seqs * length, LANES), lambda i, g: (i, g)),
            sspec, sspec,
        ],
        out_shape=[jax.ShapeDtypeStruct((bsz * length, d_ssm), F32),
                   jax.ShapeDtypeStruct(h0r.shape, F32),
                   jax.ShapeDtypeStruct(h0r.shape, F32)],
        scratch_shapes=[pltpu.VMEM((half // LANES, rows, LANES), F32),
                        pltpu.VMEM((half // LANES, rows, LANES), F32)],
        compiler_params=_cparams("parallel", "parallel"),
        name="s5",
    )(proj, prm['bb'], prm['cw'], prm['are'], prm['aim'], prm['alre'], prm['alim'], prm['d'], h0r, h0i)
    return y, hre.reshape(h0_re.shape), him.reshape(h0_re.shape)


def _mix_kernel(x_ref, o_ref, gate_ref, y_ref, hg_g_ref, wglu_ref, bglu_ref, wo_hg_ref, wo_ssm_ref,
                g_ref, b_ref, out_ref, *, alpha):
    gate = gate_ref[...]
    o_hg = o_ref[...] * hg_g_ref[...] * (gate * _sigmoid(gate))
    z = _gelu(y_ref[...])
    glu = jnp.dot(z.astype(BF16), wglu_ref[...], preferred_element_type=F32) + bglu_ref[...]
    o_ssm = z * _sigmoid(glu)
    mix = (jnp.dot(o_hg.astype(BF16), wo_hg_ref[...], preferred_element_type=F32)
           + jnp.dot(o_ssm.astype(BF16), wo_ssm_ref[...], preferred_element_type=F32))
    out_ref[...] = _layer_norm(alpha * x_ref[...] + mix, g_ref[...], b_ref[...])


def _mix(x0, o_hg, proj, y_ssm, hg_g, wglu, bglu, wo_hg, wo_ssm, g, b, *, alpha):
    t, d = x0.shape
    d_hg = o_hg.shape[1]
    d_ssm = y_ssm.shape[1]
    tm = _row_tile(t, 512)
    gate_block = (3 * d_hg) // d_hg
    row = lambda w: pl.BlockSpec((tm, w), lambda i: (i, 0))
    full = lambda a, bb: pl.BlockSpec((a, bb), lambda i: (0, 0))
    return pl.pallas_call(
        functools.partial(_mix_kernel, alpha=alpha),
        grid=(t // tm,),
        in_specs=[
            row(d), row(d_hg),
            pl.BlockSpec((tm, d_hg), lambda i: (i, gate_block)),
            row(d_ssm),
            full(1, d_hg), full(d_ssm, d_ssm), full(1, d_ssm), full(d_hg, d), full(d_ssm, d),
            full(1, d), full(1, d),
        ],
        out_specs=row(d),
        out_shape=jax.ShapeDtypeStruct((t, d), F32),
        compiler_params=_cparams("parallel"),
        name="mix",
    )(x0, o_hg, proj, y_ssm, hg_g.reshape(1, d_hg), wglu, bglu.reshape(1, d_ssm), wo_hg, wo_ssm,
      g.reshape(1, d), b.reshape(1, d))


def _top_rows(s, k):
    n, t = s.shape
    iota = lax.broadcasted_iota(jnp.int32, (n, t), 0).astype(F32)
    krow = lax.broadcasted_iota(jnp.int32, (k, t), 0)
    rank = jnp.full((n, t), float(k), F32)
    vals = jnp.zeros((k, t), F32)
    for a in range(k):
        m = jnp.max(s, axis=0, keepdims=True)
        idx = jnp.min(jnp.where(s == m, iota, float(n)), axis=0, keepdims=True)
        hit = iota == idx
        rank = jnp.where(hit, float(a), rank)
        s = jnp.where(hit, -jnp.inf, s)
        vals = jnp.where(krow == a, jnp.broadcast_to(m, (k, t)), vals)
    return vals, rank


def _candidate_pieces(t1, t2):
    k = PEER_TOPK
    t = t1.shape[1]
    bc = lambda r, n: jnp.broadcast_to(r, (n, t))
    pieces = [(bc(t1[0:1], k), t2, [b for b in range(k)], [True] * k)]
    half = k // 2
    for a in range(1, half):
        nb = k // (a + 1)
        pieces.append((bc(t1[a:a + 1], half), t2[0:half], [a * k + b for b in range(half)],
                       [b < nb for b in range(half)]))
    pieces.append((t1[half:k], bc(t2[0:1], half), [(half + r) * k for r in range(half)], [True] * half))
    return pieces


def _candidate_index_rows(t):
    k = PEER_TOPK
    dummy = jnp.zeros((k, 1), F32)
    vals = [i if ok else k * k for _, _, idx, val in _candidate_pieces(dummy, dummy) for i, ok in zip(idx, val)]
    return jnp.broadcast_to(jnp.asarray(vals, F32)[:, None], (len(vals), t))


def _peer_select(s1, s2, cidx):
    k = PEER_TOPK
    t = s1.shape[1]
    big = float(k * k)
    top1, rank1 = _top_rows(s1, k)
    top2, rank2 = _top_rows(s2, k)
    pieces = _candidate_pieces(top1, top2)
    cand = jnp.concatenate([x + y for x, y, _, _ in pieces], axis=0)
    cand = jnp.where(cidx < big, cand, -jnp.inf)
    chosen = jnp.zeros(cand.shape, F32)
    for _ in range(k):
        m = jnp.max(cand, axis=0, keepdims=True)
        idx = jnp.min(jnp.where(cand == m, cidx, big), axis=0, keepdims=True)
        hit = cidx == idx
        chosen = jnp.where(hit, 1.0, chosen)
        cand = jnp.where(hit, -jnp.inf, cand)
    e1 = jnp.exp(top1 - jnp.broadcast_to(top1[0:1], (k, t)))
    e2 = jnp.exp(top2 - jnp.broadcast_to(top2[0:1], (k, t)))
    ecand = jnp.concatenate([x * y for x, y, _, _ in _candidate_pieces(e1, e2)], axis=0)
    z = jnp.sum(chosen * ecand, axis=0, keepdims=True)
    counts = []
    off = 0
    half = k // 2
    for pi, (x, _, _, _) in enumerate(pieces):
        rows = x.shape[0]
        blk = chosen[off:off + rows]
        off += rows
        if pi < len(pieces) - 1:
            counts.append(jnp.sum(blk, axis=0, keepdims=True))
        else:
            counts.extend(blk[r:r + 1] for r in range(half))
    nk = s1.shape[0]
    c1 = jnp.zeros((nk, t), F32)
    for a in range(k):
        c1 = jnp.where(rank1 == float(a), jnp.broadcast_to(counts[a], (nk, t)), c1)
    phi = jnp.exp(s1 - jnp.broadcast_to(top1[0:1], (nk, t))) * jnp.broadcast_to(1.0 / z, (nk, t))
    psi = jnp.exp(s2 - jnp.broadcast_to(top2[0:1], (nk, t)))
    return c1, phi, rank2, psi


def _sort_network(n):
    out, p = [], 1
    while p < n:
        k = p
        while k >= 1:
            for j in range(k % p, n - k, 2 * k):
                for i in range(min(k, n - j - k)):
                    if (i + j) // (2 * p) == (i + j + k) // (2 * p):
                        out.append((i + j, i + j + k))
            k //= 2
        p *= 2
    return out


def _sorted_top(s):
    n = s.shape[0] // SUBLANES
    x = [s[i * SUBLANES:(i + 1) * SUBLANES, :] for i in range(n)]
    for i, j in _sort_network(n):
        x[i], x[j] = jnp.maximum(x[i], x[j]), jnp.minimum(x[i], x[j])
    shift = SUBLANES // 2
    while shift >= 1:
        r = [pltpu.roll(v, shift, axis=0) for v in x]
        x = [jnp.maximum(x[i], r[n - 1 - i]) for i in range(n)]
        d = n // 2
        while d >= 1:
            for i in range(n):
                if (i & d) == 0:
                    x[i], x[i + d] = jnp.maximum(x[i], x[i + d]), jnp.minimum(x[i], x[i + d])
            d //= 2
        shift //= 2
    return x


def _peer_select_fast(s1, s2, cidx):
    k = PEER_TOPK
    nk, t = s1.shape
    nblk = nk // SUBLANES
    big = float(k * k)
    l1 = _sorted_top(s1)
    l2 = _sorted_top(s2)
    sub = lax.broadcasted_iota(jnp.int32, (SUBLANES, t), 0)

    def stack(lst):
        halves = []
        for base in (0, SUBLANES):
            blk = lst[base]
            for a in range(1, SUBLANES):
                blk = jnp.where(sub == a, lst[base + a], blk)
            halves.append(blk)
        return jnp.concatenate(halves, axis=0)

    top1 = stack(l1)
    top2 = stack(l2)
    pieces = _candidate_pieces(top1, top2)
    cand = jnp.concatenate([x + y for x, y, _, _ in pieces], axis=0)
    cand = jnp.where(cidx < big, cand, -jnp.inf)
    chosen = jnp.zeros(cand.shape, F32)
    for _ in range(k):
        hit = cand == jnp.max(cand, axis=0, keepdims=True)
        chosen = jnp.where(hit, 1.0, chosen)
        cand = jnp.where(hit, -jnp.inf, cand)
    e1 = jnp.exp(top1 - jnp.broadcast_to(top1[0:1], (k, t)))
    e2 = jnp.exp(top2 - jnp.broadcast_to(top2[0:1], (k, t)))
    ecand = jnp.concatenate([x * y for x, y, _, _ in _candidate_pieces(e1, e2)], axis=0)
    z = jnp.sum(chosen * ecand, axis=0, keepdims=True)
    counts = []
    off = 0
    half = k // 2
    for pi, (x, _, _, _) in enumerate(pieces):
        rows = x.shape[0]
        blk = chosen[off:off + rows]
        off += rows
        if pi < len(pieces) - 1:
            counts.append(jnp.sum(blk, axis=0, keepdims=True))
        else:
            counts.extend(blk[r:r + 1] for r in range(half))
    c1_blocks, r2_blocks = [], []
    cnt1 = jnp.zeros((SUBLANES, t), F32)
    cnt2 = jnp.zeros((SUBLANES, t), F32)
    for i in range(nblk):
        x1 = s1[i * SUBLANES:(i + 1) * SUBLANES, :]
        x2 = s2[i * SUBLANES:(i + 1) * SUBLANES, :]
        c1b = jnp.zeros((SUBLANES, t), F32)
        r2b = jnp.zeros((SUBLANES, t), F32)
        for a in range(k):
            c1b = jnp.where(x1 == l1[a], jnp.broadcast_to(counts[a], (SUBLANES, t)), c1b)
            r2b = jnp.where(x2 < l2[a], float(a + 1), r2b)
        c1_blocks.append(c1b)
        r2_blocks.append(r2b)
        cnt1 = cnt1 + jnp.where(x1 >= l1[k - 1], 1.0, 0.0)
        cnt2 = cnt2 + jnp.where(x2 >= l2[k - 1], 1.0, 0.0)
    c1 = jnp.concatenate(c1_blocks, axis=0)
    r2 = jnp.concatenate(r2_blocks, axis=0)
    phi = jnp.exp(s1 - jnp.broadcast_to(top1[0:1], (nk, t))) * jnp.broadcast_to(1.0 / z, (nk, t))
    psi = jnp.exp(s2 - jnp.broadcast_to(top2[0:1], (nk, t)))
    bad = jnp.zeros((SUBLANES, t), F32)
    for a in range(k - 1):
        bad = jnp.where(l1[a] <= l1[a + 1], 1.0, bad)
        bad = jnp.where(l2[a] <= l2[a + 1], 1.0, bad)
    fk = float(k)
    bad = jnp.max(bad, axis=0, keepdims=True)
    bad = jnp.where(jnp.sum(cnt1, axis=0, keepdims=True) != fk, 1.0, bad)
    bad = jnp.where(jnp.sum(cnt2, axis=0, keepdims=True) != fk, 1.0, bad)
    bad = jnp.where(jnp.sum(chosen, axis=0, keepdims=True) != fk, 1.0, bad)
    return c1, phi, r2, psi, bad


def _rows_bf16(row, n):
    pack = 2 * SUBLANES
    tile = jnp.broadcast_to(row, (pack, row.shape[1])).astype(BF16)
    return jnp.concatenate([tile] * (n // pack), axis=0)


def _peer_kernel(x_ref, wq_ref, keys_ref, cidx_ref, u0_ref, u1_ref, u2_ref, vt_ref, g_ref, b_ref, out_ref,
                 xb_ref, c1_ref, phi_ref, r2_ref, psi_ref, uta_ref, utb_ref, wa_ref, wb_ref, acc_ref,
                 *, alpha, heads, nk, rows_per_step):
    j = pl.program_id(1)
    nt = (((1,), (1,)), ((), ()))
    eb = rows_per_step * nk

    @pl.when(j == 0)
    def _():
        xb = x_ref[...].astype(BF16)
        xb_ref[...] = xb
        acc_ref[...] = jnp.zeros_like(acc_ref)

        def select_head(h, carry):
            sc = []
            for c in range(2):
                hc = h * 2 + c
                w_rows = wq_ref[pl.ds(pl.multiple_of(hc * LANES, LANES), LANES), :]
                qt = lax.dot_general(w_rows, xb_ref[...], nt, preferred_element_type=F32)
                sc.append(jnp.dot(keys_ref[hc], qt.astype(BF16), preferred_element_type=F32))
            def put(c1, phi, r2, psi):
                c1_ref[h] = c1
                phi_ref[h] = phi
                r2_ref[h] = r2.astype(BF16)
                psi_ref[h] = psi.astype(BF16)

            if nk // SUBLANES == PEER_TOPK:
                *sel, bad = _peer_select_fast(sc[0], sc[1], cidx_ref[...])
                put(*sel)

                @pl.when(jnp.max(bad) > 0.0)
                def _():
                    put(*_peer_select(sc[0], sc[1], cidx_ref[...]))
            else:
                put(*_peer_select(sc[0], sc[1], cidx_ref[...]))
            return carry

        lax.fori_loop(0, heads, select_head, 0)
        uta_ref[...] = lax.dot_general(u0_ref[...], xb, nt, preferred_element_type=F32)

    xb = xb_ref[...]

    def mix_block(block, ut_ref, w_ref):
        for r in range(rows_per_step):
            n1 = block * rows_per_step + r
            gsum = None
            for h in range(heads):
                c1row = _rows_bf16(c1_ref[h, pl.ds(n1, 1), :], nk)
                phirow = _rows_bf16(phi_ref[h, pl.ds(n1, 1), :], nk)
                term = jnp.where(r2_ref[h] < c1row, phirow * psi_ref[h], jnp.zeros((), BF16))
                gsum = term if gsum is None else gsum + term
            act = _gelu(ut_ref[r * nk:(r + 1) * nk, :]).astype(BF16)
            w_ref[r * nk:(r + 1) * nk, :] = gsum * act

    utb_ref[...] = lax.dot_general(u1_ref[...], xb, nt, preferred_element_type=F32)
    mix_block(2 * j, uta_ref, wa_ref)
    acc_ref[...] += jnp.dot(vt_ref[:, :eb], wa_ref[...], preferred_element_type=F32)
    uta_ref[...] = lax.dot_general(u2_ref[...], xb, nt, preferred_element_type=F32)
    mix_block(2 * j + 1, utb_ref, wb_ref)
    acc_ref[...] += jnp.dot(vt_ref[:, eb:], wb_ref[...], preferred_element_type=F32)

    @pl.when(j == pl.num_programs(1) - 1)
    def _():
        ffn = acc_ref[...].T
        out_ref[...] = _layer_norm(alpha * x_ref[...] + ffn, g_ref[...], b_ref[...])


def _peer(x, wq_t, keys, u_tab, v_tab_t, g, b, *, alpha):
    t, d = x.shape
    heads, _, nk, dh = keys.shape
    n_exp = u_tab.shape[0]
    tm = 512 if t % 512 == 0 else _row_tile(t, 512)
    rows = PEER_EXPERT_ROWS
    eb = rows * nk
    keys2 = keys.reshape(heads * 2, nk, dh)
    cidx = _candidate_index_rows(tm)
    n_blocks = n_exp // eb
    return pl.pallas_call(
        functools.partial(_peer_kernel, alpha=alpha, heads=heads, nk=nk, rows_per_step=rows),
        grid=(t // tm, n_blocks // 2),
        in_specs=[
            pl.BlockSpec((tm, d), lambda i, j: (i, 0)),
            pl.BlockSpec(wq_t.shape, lambda i, j: (0, 0)),
            pl.BlockSpec(keys2.shape, lambda i, j: (0, 0, 0)),
            pl.BlockSpec(cidx.shape, lambda i, j: (0, 0)),
            pl.BlockSpec((eb, d), lambda i, j: (0, 0)),
            pl.BlockSpec((eb, d), lambda i, j: (2 * j + 1, 0)),
            pl.BlockSpec((eb, d), lambda i, j: ((2 * j + 2) % n_blocks, 0)),
            pl.BlockSpec((d, 2 * eb), lambda i, j: (0, j)),
            pl.BlockSpec((1, d), lambda i, j: (0, 0)),
            pl.BlockSpec((1, d), lambda i, j: (0, 0)),
        ],
        out_specs=pl.BlockSpec((tm, d), lambda i, j: (i, 0)),
        out_shape=jax.ShapeDtypeStruct((t, d), F32),
        scratch_shapes=[
            pltpu.VMEM((tm, d), BF16),
            pltpu.VMEM((heads, nk, tm), F32), pltpu.VMEM((heads, nk, tm), F32),
            pltpu.VMEM((heads, nk, tm), BF16), pltpu.VMEM((heads, nk, tm), BF16),
            pltpu.VMEM((eb, tm), F32), pltpu.VMEM((eb, tm), F32),
            pltpu.VMEM((eb, tm), BF16), pltpu.VMEM((eb, tm), BF16),
            pltpu.VMEM((d, tm), F32),
        ],
        compiler_params=_cparams("parallel", "arbitrary"),
        name="peer",
    )(x, wq_t, keys2, cidx, u_tab, u_tab, u_tab, v_tab_t, g.reshape(1, d), b.reshape(1, d))


def _pad_tokens(a, length):
    return jnp.pad(a, ((0, 0), (0, length - a.shape[1]), (0, 0)))


def _sequence_mixers(proj3, lb, s_hg, s_re, s_im, s5_prm, d_hg):
    bsz, length, _ = proj3.shape
    lpad = -(-length // SUBLANES) * SUBLANES
    o, s_hg_new = _hgrn(_pad_tokens(proj3, lpad) if lpad != length else proj3, lb, s_hg, valid=length)
    lc = S5_CHUNK if length % S5_CHUNK == 0 else length
    y, s_re_new, s_im_new = _s5(proj3.reshape(bsz * length, -1), s_re, s_im, s5_prm[lc],
                                length=length, lc=lc, u_col=4 * d_hg)
    return o[:, :length], y.reshape(bsz, length, -1), s_hg_new, s_re_new, s_im_new


def kernel(x_prompt, x_sample, state_hgrn, state_ssm_re, state_ssm_im, meta_tokens, ln_emb_g, ln_emb_b,
           lb_logits, w_in, b_in, hg_norm_g, ssm_a_re, ssm_a_im, ssm_log_dt, ssm_b_re, ssm_b_im,
           ssm_c_re, ssm_c_im, ssm_d, w_glu, b_glu, w_out, ln1_g, ln1_b, peer_w_q, peer_keys,
           peer_u, peer_v, ln2_g, ln2_b):
    depth = w_in.shape[0]
    alpha = (2.0 * depth) ** 0.25
    bp, seq, d = x_prompt.shape
    bs, dseq, _ = x_sample.shape
    n_meta = meta_tokens.shape[0]
    heads = state_hgrn.shape[2]
    d_hg = heads * LANES
    groups, pstate = state_ssm_re.shape[2], state_ssm_re.shape[3]
    lbs = jnp.cumsum(jax.nn.softmax(lb_logits.astype(F32), axis=0), axis=0)

    xp = x_prompt.astype(F32).reshape(bp * seq, d)
    xm = meta_tokens.astype(F32)
    xs = x_sample.astype(F32).reshape(bs * dseq, d)
    n_s = bs * dseq
    hg_p, re_p, im_p, hg_s, re_s, im_s = [], [], [], [], [], []
    for l in range(depth):
        last = l == depth - 1
        w_in_b = w_in[l].astype(BF16)
        lb = lbs[l].reshape(1, d_hg)
        s5_prm = {lc: _s5_params(ssm_a_re[l], ssm_a_im[l], ssm_log_dt[l], ssm_b_re[l], ssm_b_im[l],
                                 ssm_c_re[l], ssm_c_im[l], ssm_d[l], lc)
                  for lc in {S5_CHUNK if n % S5_CHUNK == 0 else n for n in (n_meta, seq, dseq)}}
        wglu = w_glu[l].astype(BF16)
        wo_hg = w_out[l, :d_hg].astype(BF16)
        wo_ssm = w_out[l, d_hg:].astype(BF16)
        wq_t = peer_w_q[l].T.astype(BF16)
        keys = peer_keys[l].astype(BF16)
        u_tab = peer_u[l].astype(BF16)
        v_tab_t = peer_v[l].T.astype(BF16)

        xsm = jnp.concatenate([xs, xm], axis=0)
        x0_p, proj_p = _inproj(xp, ln_emb_g, ln_emb_b, w_in_b, b_in[l], apply_ln=(l == 0))
        x0_sm, proj_sm = _inproj(xsm, ln_emb_g, ln_emb_b, w_in_b, b_in[l], apply_ln=(l == 0))
        d_in = proj_p.shape[1]

        proj_m = proj_sm[n_s:].reshape(1, n_meta, d_in)
        proj_m8 = jnp.broadcast_to(proj_m, (SUBLANES, n_meta, d_in))
        zero_hg = jnp.zeros((SUBLANES, heads, LANES, LANES), F32)
        zero_ss = jnp.zeros((SUBLANES, groups, pstate), F32)
        o_m, y_m, hg_m, re_m, im_m = _sequence_mixers(proj_m8, lb, zero_hg, zero_ss, zero_ss, s5_prm, d_hg)

        o_p, y_p, shg, sre, sim = _sequence_mixers(
            proj_p.reshape(bp, seq, d_in), lb,
            jnp.broadcast_to(hg_m[:1], (bp,) + hg_m.shape[1:]),
            jnp.broadcast_to(re_m[:1], (bp,) + re_m.shape[1:]),
            jnp.broadcast_to(im_m[:1], (bp,) + im_m.shape[1:]), s5_prm, d_hg)
        hg_p.append(shg)
        re_p.append(sre)
        im_p.append(sim)

        o_s, y_s, shg, sre, sim = _sequence_mixers(
            proj_sm[:n_s].reshape(bs, dseq, d_in), lb, state_hgrn[l].astype(F32),
            state_ssm_re[l].astype(F32), state_ssm_im[l].astype(F32), s5_prm, d_hg)
        hg_s.append(shg)
        re_s.append(sre)
        im_s.append(sim)

        mix_args = (hg_norm_g[l], wglu, b_glu[l], wo_hg, wo_ssm, ln1_g[l], ln1_b[l])
        peer_args = (wq_t, keys, u_tab, v_tab_t, ln2_g[l], ln2_b[l])
        x1_p = _mix(x0_p, o_p.reshape(bp * seq, d_hg), proj_p, y_p.reshape(bp * seq, -1), *mix_args, alpha=alpha)
        xp = _peer(x1_p, *peer_args, alpha=alpha)
        x1_s = _mix(x0_sm[:n_s], o_s.reshape(n_s, d_hg), proj_sm[:n_s], y_s.reshape(n_s, -1), *mix_args, alpha=alpha)
        xs = _peer(x1_s, *peer_args, alpha=alpha)
        if not last:
            x1_m = _mix(jnp.broadcast_to(x0_sm[n_s:], (n_meta, d)), o_m[0], proj_sm[n_s:], y_m[0], *mix_args, alpha=alpha)
            xm = _peer(x1_m, *peer_args, alpha=alpha)

    y_prompt = xp.reshape(bp, seq, d).astype(x_prompt.dtype)
    y_sample = xs.reshape(bs, dseq, d).astype(x_sample.dtype)
    return (y_prompt, y_sample, jnp.stack(hg_p), jnp.stack(re_p), jnp.stack(im_p),
            jnp.stack(hg_s), jnp.stack(re_s), jnp.stack(im_s))
```

```python
import functools
import math

import jax
import jax.numpy as jnp
from jax import lax
from jax.experimental import pallas as pl
from jax.experimental.pallas import tpu as pltpu

F32 = jnp.float32
BF16 = jnp.bfloat16

LN_EPS = 1e-5
RMS_EPS = 1e-6
HG_CHUNK = 16
HG_CHUNKS_PER_ITER = 8
HG_SEQS_PER_STEP = 16
SSM_GROUP = 16
S5_CHUNK = 16
S5_GROUP_TILE = 8
S5_SEQS_PER_STEP = 4
PEER_TOPK = 16
LANES = 128
SUBLANES = 8
VMEM_LIMIT = 60 * 1024 * 1024
PEER_EXPERT_ROWS = 8


def _cparams(*sem):
    return pltpu.CompilerParams(dimension_semantics=sem, vmem_limit_bytes=VMEM_LIMIT)


def _layer_norm(x, g, b):
    mu = jnp.mean(x, axis=-1, keepdims=True)
    xc = x - mu
    var = jnp.mean(xc * xc, axis=-1, keepdims=True)
    return xc * lax.rsqrt(var + LN_EPS) * g + b


def _sigmoid(x):
    return 1.0 / (1.0 + jnp.exp(-x))


def _gelu(x):
    c1 = -2.0 * math.sqrt(2.0 / math.pi)
    return x / (1.0 + jnp.exp(x * (c1 + (c1 * 0.044715) * (x * x))))


def _row_tile(t, target):
    best = None
    for cand in range(SUBLANES, min(t, target) + 1, SUBLANES):
        if t % cand == 0:
            best = cand
    return best if best is not None else t


def _inproj_kernel(x_ref, g_ref, b_ref, w_ref, bi_ref, x0_ref, proj_ref, *, apply_ln):
    x = x_ref[...]
    if apply_ln:
        x = _layer_norm(x, g_ref[...], b_ref[...])
    x0_ref[...] = x
    proj_ref[...] = jnp.dot(x.astype(BF16), w_ref[...], preferred_element_type=F32) + bi_ref[...]


def _inproj(x, g, b, w_bf16, bias, *, apply_ln):
    t, d = x.shape
    d_in = w_bf16.shape[1]
    tm = _row_tile(t, 512)
    return pl.pallas_call(
        functools.partial(_inproj_kernel, apply_ln=apply_ln),
        grid=(t // tm,),
        in_specs=[
            pl.BlockSpec((tm, d), lambda i: (i, 0)),
            pl.BlockSpec((1, d), lambda i: (0, 0)),
            pl.BlockSpec((1, d), lambda i: (0, 0)),
            pl.BlockSpec((d, d_in), lambda i: (0, 0)),
            pl.BlockSpec((1, d_in), lambda i: (0, 0)),
        ],
        out_specs=[
            pl.BlockSpec((tm, d), lambda i: (i, 0)),
            pl.BlockSpec((tm, d_in), lambda i: (i, 0)),
        ],
        out_shape=[jax.ShapeDtypeStruct((t, d), F32), jax.ShapeDtypeStruct((t, d_in), F32)],
        compiler_params=_cparams("parallel"),
        name="inproj",
    )(x, g.reshape(1, d), b.reshape(1, d), w_bf16, bias.reshape(1, d_in))


def _hgrn_kernel(q_ref, f_ref, v_ref, lb_ref, s0_ref, o_ref, s_ref, st_ref,
                 *, chunk, n_iters, per_iter, seqs, valid):
    lb = lb_ref[...]
    ones = jnp.ones((LANES, LANES), BF16)
    groups = seqs * per_iter
    shape = (groups, chunk, LANES)
    row = lax.broadcasted_iota(jnp.int32, shape, 1)
    span = per_iter * chunk
    nt = (((1,), (1,)), ((), ()))
    tn = (((0,), (0,)), ((), ()))

    for b in range(seqs):
        st_ref[b] = s0_ref[b].T

    def take(ref, r0):
        return jnp.concatenate([ref[b, pl.ds(r0, span), :].reshape(per_iter, chunk, LANES)
                                for b in range(seqs)], axis=0)

    def step(i, carry):
        r0 = pl.multiple_of(i * span, span)
        q = take(q_ref, r0)
        fp = take(f_ref, r0)
        v = take(v_ref, r0)
        f = lb + (1.0 - lb) * _sigmoid(fp)
        logf = jnp.log(f)
        kk = (1.0 - lb) * _sigmoid(-fp)
        if valid < chunk:
            live = row < valid
            logf = jnp.where(live, logf, 0.0)
            kk = jnp.where(live, kk, 0.0)
            q = jnp.where(live, q, 0.0)
            v = jnp.where(live, v, 0.0)
        pick = lambda a, s: jnp.broadcast_to(a[:, s:s + 1, :], shape)
        bc = jnp.zeros(shape, F32)
        for s in range(chunk):
            bc = bc + jnp.where(row >= s, pick(logf, s), 0.0)
        split = chunk // 2 if chunk % (2 * SUBLANES) == 0 else 0
        lows = [split if s >= split else 0 for s in range(chunk)]
        prods = []
        for s, lo in zip(range(chunk), lows):
            part = (groups, chunk - lo, LANES)
            near = lambda a: jnp.broadcast_to(a[:, s:s + 1, :], part)
            e = jnp.where(row[:, lo:, :] >= s, jnp.exp(bc[:, lo:, :] - near(bc)), 0.0)
            prods.append((q[:, lo:, :] * e * near(kk)).reshape(groups * (chunk - lo), LANES))
        p_all = jnp.concatenate(prods, axis=0).astype(BF16)
        att = jnp.dot(p_all, ones, preferred_element_type=F32)
        o = jnp.zeros(shape, F32)
        o_low = jnp.zeros((groups, chunk - split, LANES), F32)
        off = 0
        for s, lo in zip(range(chunk), lows):
            part = (groups, chunk - lo, LANES)
            n = groups * (chunk - lo)
            term = att[off:off + n, :].reshape(part) * jnp.broadcast_to(v[:, s:s + 1, :], part)
            off += n
            if lo == 0:
                o = o + term
            else:
                o_low = o_low + term
        if split:
            o = o + jnp.concatenate([jnp.zeros((groups, split, LANES), F32), o_low], axis=1)
        qs = (q * jnp.exp(bc)).astype(BF16)
        kt = (kk * jnp.exp(pick(bc, chunk - 1) - bc)).astype(BF16)
        vb = v.astype(BF16)
        decay = jnp.exp(bc[:, chunk - 1:chunk, :])
        ds = [lax.dot_general(vb[g], kt[g], tn, preferred_element_type=F32) for g in range(groups)]
        outs = []
        for b in range(seqs):
            st = st_ref[b]
            before = []
            for n in range(per_iter):
                g = b * per_iter + n
                before.append(st.astype(BF16))
                st = st * decay[g] + ds[g]
            st_ref[b] = st
            for n in range(per_iter):
                g = b * per_iter + n
                outs.append(o[g] + lax.dot_general(qs[g], before[n], nt, preferred_element_type=F32))
        for b in range(seqs):
            ob = jnp.concatenate(outs[b * per_iter:(b + 1) * per_iter], axis=0)
            ob = ob * lax.rsqrt(jnp.mean(ob * ob, axis=-1, keepdims=True) + RMS_EPS)
            o_ref[b, pl.ds(r0, span), :] = ob
        return carry

    lax.fori_loop(0, n_iters, step, 0)
    for b in range(seqs):
        s_ref[b] = st_ref[b].T


def _hgrn(proj3, lb, s0, *, valid):
    bsz, length, _ = proj3.shape
    heads = s0.shape[1]
    chunk = HG_CHUNK if length % HG_CHUNK == 0 else length
    n_chunks = length // chunk
    divisor = lambda n, cap: max(c for c in range(1, cap + 1) if n % c == 0)
    if n_chunks > 1:
        seqs, per_iter = 1, divisor(n_chunks, HG_CHUNKS_PER_ITER)
    else:
        seqs, per_iter = divisor(bsz, HG_SEQS_PER_STEP), 1
    seq_spec = lambda off: pl.BlockSpec((seqs, length, LANES), lambda b, h: (b, 0, off + h))
    state_spec = pl.BlockSpec((seqs, None, LANES, LANES), lambda b, h: (b, h, 0, 0))
    return pl.pallas_call(
        functools.partial(_hgrn_kernel, chunk=chunk, n_iters=n_chunks // per_iter, per_iter=per_iter,
                          seqs=seqs, valid=min(valid, chunk)),
        grid=(bsz // seqs, heads),
        in_specs=[
            seq_spec(0), seq_spec(heads), seq_spec(2 * heads),
            pl.BlockSpec((1, LANES), lambda b, h: (0, h)),
            state_spec,
        ],
        out_specs=[
            pl.BlockSpec((seqs, length, LANES), lambda b, h: (b, 0, h)),
            state_spec,
        ],
        out_shape=[jax.ShapeDtypeStruct((bsz, length, heads * LANES), F32),
                   jax.ShapeDtypeStruct(s0.shape, F32)],
        scratch_shapes=[pltpu.VMEM((seqs, LANES, LANES), F32)],
        compiler_params=_cparams("parallel", "parallel"),
        name="hgrn2",
    )(proj3, proj3, proj3, lb, s0)


def _s5_kernel(u_ref, bb_ref, c_ref, are_ref, aim_ref, alre_ref, alim_ref, d_ref, h0re_ref, h0im_ref,
               y_ref, hre_ref, him_ref, hpre_ref, hpim_ref, *, lc, n_chunks, seqs):
    rows = seqs * n_chunks
    half = are_ref.shape[-1]
    a_re = are_ref[...]
    a_im = aim_ref[...]
    bb = bb_ref[...]

    def drive(t):
        u_t = u_ref[pl.ds(t, rows, stride=lc), :]
        return u_t, jnp.dot(u_t.astype(BF16), bb, preferred_element_type=F32)

    def advance(h_re, h_im, bu):
        return (a_re * h_re - a_im * h_im + bu[:, :half], a_re * h_im + a_im * h_re + bu[:, half:])

    h_re = jnp.zeros((rows, half), F32)
    h_im = jnp.zeros((rows, half), F32)
    for t in range(lc):
        h_re, h_im = advance(h_re, h_im, drive(t)[1])

    n_tiles = half // LANES

    def put(ref, idx, val):
        for k in range(n_tiles):
            ref[k, idx, :] = val[:, k * LANES:(k + 1) * LANES]

    def get(ref, idx):
        return jnp.concatenate([ref[k, idx, :] for k in range(n_tiles)], axis=-1)

    put(hpre_ref, slice(None), h_re)
    put(hpim_ref, slice(None), h_im)

    al_re = alre_ref[...]
    al_im = alim_ref[...]

    def carry_step(c, carry):
        c_re, c_im = carry
        idx = pl.ds(c, seqs, stride=n_chunks)
        l_re = get(hpre_ref, idx)
        l_im = get(hpim_ref, idx)
        put(hpre_ref, idx, c_re)
        put(hpim_ref, idx, c_im)
        return (al_re * c_re - al_im * c_im + l_re, al_re * c_im + al_im * c_re + l_im)

    c_re, c_im = lax.fori_loop(0, n_chunks, carry_step, (h0re_ref[...], h0im_ref[...]))
    hre_ref[...] = c_re
    him_ref[...] = c_im

    h_re = get(hpre_ref, slice(None))
    h_im = get(hpim_ref, slice(None))
    c_w = c_ref[...]
    d_vec = d_ref[...]
    for t in range(lc):
        u_t, bu = drive(t)
        h_re, h_im = advance(h_re, h_im, bu)
        y = (jnp.dot(h_re.astype(BF16), c_w[:half], preferred_element_type=F32)
             + jnp.dot(h_im.astype(BF16), c_w[half:], preferred_element_type=F32) + d_vec * u_t)
        y_ref[pl.ds(t, rows, stride=lc), :] = y


def _s5_params(a_re, a_im, log_dt, b_re, b_im, c_re, c_im, d, lc):
    groups, pstate = a_re.shape
    dt = jnp.exp(log_dt.astype(F32))[:, None]
    a_re = a_re.astype(F32)
    a_im = a_im.astype(F32)
    zr, zi = a_re * dt, a_im * dt
    mag = jnp.exp(zr)
    ab_re, ab_im = mag * jnp.cos(zi), mag * jnp.sin(zi)
    den = a_re * a_re + a_im * a_im
    nr = ab_re - 1.0
    coef_re = (nr * a_re + ab_im * a_im) / den
    coef_im = (ab_im * a_re - nr * a_im) / den
    b_re = b_re.astype(F32)
    b_im = b_im.astype(F32)
    bb_re = coef_re[..., None] * b_re - coef_im[..., None] * b_im
    bb_im = coef_re[..., None] * b_im + coef_im[..., None] * b_re
    pm = jnp.exp(zr * lc)
    al_re, al_im = pm * jnp.cos(zi * lc), pm * jnp.sin(zi * lc)
    gt = S5_GROUP_TILE
    tiles = groups // gt
    swap = lambda x: jnp.transpose(x, (0, 2, 1))
    bb = jnp.concatenate([_block_diag(swap(bb_re), gt), _block_diag(swap(bb_im), gt)], axis=-1)
    cw = jnp.concatenate([_block_diag(swap(c_re.astype(F32)), gt), -_block_diag(swap(c_im.astype(F32)), gt)], axis=1)
    lane = lambda x: x.reshape(tiles, 1, gt * pstate)
    return dict(bb=bb.astype(BF16), cw=cw.astype(BF16), are=lane(ab_re), aim=lane(ab_im),
                alre=lane(al_re), alim=lane(al_im), d=d.astype(F32).reshape(1, groups * SSM_GROUP))


def _block_diag(x, gm):
    if gm == 1:
        return x
    g, a, b = x.shape
    x = x.reshape(g // gm, gm, a, b)
    eye = jnp.eye(gm, dtype=x.dtype)
    return (x[:, :, :, None, :] * eye[None, :, None, :, None]).reshape(g // gm, gm * a, gm * b)


def _s5(proj, h0_re, h0_im, prm, *, length, lc, u_col):
    bsz, n_groups, pstate = h0_re.shape
    d_ssm = n_groups * SSM_GROUP
    tiles = n_groups // S5_GROUP_TILE
    half = S5_GROUP_TILE * pstate
    n_chunks = length // lc
    seqs = bsz if n_chunks == 1 else min(bsz, S5_SEQS_PER_STEP)
    steps = bsz // seqs
    rows = seqs * n_chunks
    h0r = h0_re.reshape(steps, seqs, n_groups * pstate)
    h0i = h0_im.reshape(steps, seqs, n_groups * pstate)
    tspec = lambda a, b: pl.BlockSpec((None, a, b), lambda i, g: (g, 0, 0))
    sspec = pl.BlockSpec((None, seqs, half), lambda i, g: (i, 0, g))
    y, hre, him = pl.pallas_call(
        functools.partial(_s5_kernel, lc=lc, n_chunks=n_chunks, seqs=seqs),
        grid=(steps, tiles),
        in_specs=[
            pl.BlockSpec((seqs * length, LANES), lambda i, g: (i, u_col // LANES + g)),
            tspec(LANES, 2 * half), tspec(2 * half, LANES),
            tspec(1, half), tspec(1, half), tspec(1, half), tspec(1, half),
            pl.BlockSpec((1, LANES), lambda i, g: (0, g)),
            sspec, sspec,
        ],
        out_specs=[
            pl.BlockSpec((seqs * length, LANES), lambda i, g: (i, g)),
            sspec, sspec,
        ],
        out_shape=[jax.ShapeDtypeStruct((bsz * length, d_ssm), F32),
                   jax.ShapeDtypeStruct(h0r.shape, F32),
                   jax.ShapeDtypeStruct(h0r.shape, F32)],
        scratch_shapes=[pltpu.VMEM((half // LANES, rows, LANES), F32),
                        pltpu.VMEM((half // LANES, rows, LANES), F32)],
        compiler_params=_cparams("parallel", "parallel"),
        name="s5",
    )(proj, prm['bb'], prm['cw'], prm['are'], prm['aim'], prm['alre'], prm['alim'], prm['d'], h0r, h0i)
    return y, hre.reshape(h0_re.shape), him.reshape(h0_re.shape)


def _mix_kernel(x_ref, o_ref, gate_ref, y_ref, hg_g_ref, wglu_ref, bglu_ref, wo_hg_ref, wo_ssm_ref,
                g_ref, b_ref, out_ref, *, alpha):
    gate = gate_ref[...]
    o_hg = o_ref[...] * hg_g_ref[...] * (gate * _sigmoid(gate))
    z = _gelu(y_ref[...])
    glu = jnp.dot(z.astype(BF16), wglu_ref[...], preferred_element_type=F32) + bglu_ref[...]
    o_ssm = z * _sigmoid(glu)
    mix = (jnp.dot(o_hg.astype(BF16), wo_hg_ref[...], preferred_element_type=F32)
           + jnp.dot(o_ssm.astype(BF16), wo_ssm_ref[...], preferred_element_type=F32))
    out_ref[...] = _layer_norm(alpha * x_ref[...] + mix, g_ref[...], b_ref[...])


def _mix(x0, o_hg, proj, y_ssm, hg_g, wglu, bglu, wo_hg, wo_ssm, g, b, *, alpha):
    t, d = x0.shape
    d_hg = o_hg.shape[1]
    d_ssm = y_ssm.shape[1]
    tm = _row_tile(t, 512)
    gate_block = (3 * d_hg) // d_hg
    row = lambda w: pl.BlockSpec((tm, w), lambda i: (i, 0))
    full = lambda a, bb: pl.BlockSpec((a, bb), lambda i: (0, 0))
    return pl.pallas_call(
        functools.partial(_mix_kernel, alpha=alpha),
        grid=(t // tm,),
        in_specs=[
            row(d), row(d_hg),
            pl.BlockSpec((tm, d_hg), lambda i: (i, gate_block)),
            row(d_ssm),
            full(1, d_hg), full(d_ssm, d_ssm), full(1, d_ssm), full(d_hg, d), full(d_ssm, d),
            full(1, d), full(1, d),
        ],
        out_specs=row(d),
        out_shape=jax.ShapeDtypeStruct((t, d), F32),
        compiler_params=_cparams("parallel"),
        name="mix",
    )(x0, o_hg, proj, y_ssm, hg_g.reshape(1, d_hg), wglu, bglu.reshape(1, d_ssm), wo_hg, wo_ssm,
      g.reshape(1, d), b.reshape(1, d))


def _top_rows(s, k):
    n, t = s.shape
    iota = lax.broadcasted_iota(jnp.int32, (n, t), 0).astype(F32)
    krow = lax.broadcasted_iota(jnp.int32, (k, t), 0)
    rank = jnp.full((n, t), float(k), F32)
    vals = jnp.zeros((k, t), F32)
    for a in range(k):
        m = jnp.max(s, axis=0, keepdims=True)
        idx = jnp.min(jnp.where(s == m, iota, float(n)), axis=0, keepdims=True)
        hit = iota == idx
        rank = jnp.where(hit, float(a), rank)
        s = jnp.where(hit, -jnp.inf, s)
        vals = jnp.where(krow == a, jnp.broadcast_to(m, (k, t)), vals)
    return vals, rank


def _candidate_pieces(t1, t2):
    k = PEER_TOPK
    t = t1.shape[1]
    bc = lambda r, n: jnp.broadcast_to(r, (n, t))
    pieces = [(bc(t1[0:1], k), t2, [b for b in range(k)], [True] * k)]
    half = k // 2
    for a in range(1, half):
        nb = k // (a + 1)
        pieces.append((bc(t1[a:a + 1], half), t2[0:half], [a * k + b for b in range(half)],
                       [b < nb for b in range(half)]))
    pieces.append((t1[half:k], bc(t2[0:1], half), [(half + r) * k for r in range(half)], [True] * half))
    return pieces


def _candidate_index_rows(t):
    k = PEER_TOPK
    dummy = jnp.zeros((k, 1), F32)
    vals = [i if ok else k * k for _, _, idx, val in _candidate_pieces(dummy, dummy) for i, ok in zip(idx, val)]
    return jnp.broadcast_to(jnp.asarray(vals, F32)[:, None], (len(vals), t))


def _peer_select(s1, s2, cidx):
    k = PEER_TOPK
    t = s1.shape[1]
    big = float(k * k)
    top1, rank1 = _top_rows(s1, k)
    top2, rank2 = _top_rows(s2, k)
    pieces = _candidate_pieces(top1, top2)
    cand = jnp.concatenate([x + y for x, y, _, _ in pieces], axis=0)
    cand = jnp.where(cidx < big, cand, -jnp.inf)
    chosen = jnp.zeros(cand.shape, F32)
    for _ in range(k):
        m = jnp.max(cand, axis=0, keepdims=True)
        idx = jnp.min(jnp.where(cand == m, cidx, big), axis=0, keepdims=True)
        hit = cidx == idx
        chosen = jnp.where(hit, 1.0, chosen)
        cand = jnp.where(hit, -jnp.inf, cand)
    e1 = jnp.exp(top1 - jnp.broadcast_to(top1[0:1], (k, t)))
    e2 = jnp.exp(top2 - jnp.broadcast_to(top2[0:1], (k, t)))
    ecand = jnp.concatenate([x * y for x, y, _, _ in _candidate_pieces(e1, e2)], axis=0)
    z = jnp.sum(chosen * ecand, axis=0, keepdims=True)
    counts = []
    off = 0
    half = k // 2
    for pi, (x, _, _, _) in enumerate(pieces):
        rows = x.shape[0]
        blk = chosen[off:off + rows]
        off += rows
        if pi < len(pieces) - 1:
            counts.append(jnp.sum(blk, axis=0, keepdims=True))
        else:
            counts.extend(blk[r:r + 1] for r in range(half))
    nk = s1.shape[0]
    c1 = jnp.zeros((nk, t), F32)
    for a in range(k):
        c1 = jnp.where(rank1 == float(a), jnp.broadcast_to(counts[a], (nk, t)), c1)
    phi = jnp.exp(s1 - jnp.broadcast_to(top1[0:1], (nk, t))) * jnp.broadcast_to(1.0 / z, (nk, t))
    psi = jnp.exp(s2 - jnp.broadcast_to(top2[0:1], (nk, t)))
    return c1, phi, rank2, psi


def _sort_network(n):
    out, p = [], 1
    while p < n:
        k = p
        while k >= 1:
            for j in range(k % p, n - k, 2 * k):
                for i in range(min(k, n - j - k)):
                    if (i + j) // (2 * p) == (i + j + k) // (2 * p):
                        out.append((i + j, i + j + k))
            k //= 2
        p *= 2
    return out


def _sorted_top(s):
    n = s.shape[0] // SUBLANES
    x = [s[i * SUBLANES:(i + 1) * SUBLANES, :] for i in range(n)]
    for i, j in _sort_network(n):
        x[i], x[j] = jnp.maximum(x[i], x[j]), jnp.minimum(x[i], x[j])
    shift = SUBLANES // 2
    while shift >= 1:
        r = [pltpu.roll(v, shift, axis=0) for v in x]
        x = [jnp.maximum(x[i], r[n - 1 - i]) for i in range(n)]
        d = n // 2
        while d >= 1:
            for i in range(n):
                if (i & d) == 0:
                    x[i], x[i + d] = jnp.maximum(x[i], x[i + d]), jnp.minimum(x[i], x[i + d])
            d //= 2
        shift //= 2
    return x


def _peer_select_fast(s1, s2, cidx):
    k = PEER_TOPK
    nk, t = s1.shape
    nblk = nk // SUBLANES
    big = float(k * k)
    l1 = _sorted_top(s1)
    l2 = _sorted_top(s2)
    sub = lax.broadcasted_iota(jnp.int32, (SUBLANES, t), 0)

    def stack(lst):
        halves = []
        for base in (0, SUBLANES):
            blk = lst[base]
            for a in range(1, SUBLANES):
                blk = jnp.where(sub == a, lst[base + a], blk)
            halves.append(blk)
        return jnp.concatenate(halves, axis=0)

    top1 = stack(l1)
    top2 = stack(l2)
    pieces = _candidate_pieces(top1, top2)
    cand = jnp.concatenate([x + y for x, y, _, _ in pieces], axis=0)
    cand = jnp.where(cidx < big, cand, -jnp.inf)
    chosen = jnp.zeros(cand.shape, F32)
    for _ in range(k):
        hit = cand == jnp.max(cand, axis=0, keepdims=True)
        chosen = jnp.where(hit, 1.0, chosen)
        cand = jnp.where(hit, -jnp.inf, cand)
    e1 = jnp.exp(top1 - jnp.broadcast_to(top1[0:1], (k, t)))
    e2 = jnp.exp(top2 - jnp.broadcast_to(top2[0:1], (k, t)))
    ecand = jnp.concatenate([x * y for x, y, _, _ in _candidate_pieces(e1, e2)], axis=0)
    z = jnp.sum(chosen * ecand, axis=0, keepdims=True)
    counts = []
    off = 0
    half = k // 2
    for pi, (x, _, _, _) in enumerate(pieces):
        rows = x.shape[0]
        blk = chosen[off:off + rows]
        off += rows
        if pi < len(pieces) - 1:
            counts.append(jnp.sum(blk, axis=0, keepdims=True))
        else:
            counts.extend(blk[r:r + 1] for r in range(half))
    c1_blocks, r2_blocks = [], []
    cnt1 = jnp.zeros((SUBLANES, t), F32)
    cnt2 = jnp.zeros((SUBLANES, t), F32)
    for i in range(nblk):
        x1 = s1[i * SUBLANES:(i + 1) * SUBLANES, :]
        x2 = s2[i * SUBLANES:(i + 1) * SUBLANES, :]
        c1b = jnp.zeros((SUBLANES, t), F32)
        r2b = jnp.zeros((SUBLANES, t), F32)
        for a in range(k):
            c1b = jnp.where(x1 == l1[a], jnp.broadcast_to(counts[a], (SUBLANES, t)), c1b)
            r2b = jnp.where(x2 < l2[a], float(a + 1), r2b)
        c1_blocks.append(c1b)
        r2_blocks.append(r2b)
        cnt1 = cnt1 + jnp.where(x1 >= l1[k - 1], 1.0, 0.0)
        cnt2 = cnt2 + jnp.where(x2 >= l2[k - 1], 1.0, 0.0)
    c1 = jnp.concatenate(c1_blocks, axis=0)
    r2 = jnp.concatenate(r2_blocks, axis=0)
    phi = jnp.exp(s1 - jnp.broadcast_to(top1[0:1], (nk, t))) * jnp.broadcast_to(1.0 / z, (nk, t))
    psi = jnp.exp(s2 - jnp.broadcast_to(top2[0:1], (nk, t)))
    bad = jnp.zeros((SUBLANES, t), F32)
    for a in range(k - 1):
        bad = jnp.where(l1[a] <= l1[a + 1], 1.0, bad)
        bad = jnp.where(l2[a] <= l2[a + 1], 1.0, bad)
    fk = float(k)
    bad = jnp.max(bad, axis=0, keepdims=True)
    bad = jnp.where(jnp.sum(cnt1, axis=0, keepdims=True) != fk, 1.0, bad)
    bad = jnp.where(jnp.sum(cnt2, axis=0, keepdims=True) != fk, 1.0, bad)
    bad = jnp.where(jnp.sum(chosen, axis=0, keepdims=True) != fk, 1.0, bad)
    return c1, phi, r2, psi, bad


def _rows_bf16(row, n):
    pack = 2 * SUBLANES
    tile = jnp.broadcast_to(row, (pack, row.shape[1])).astype(BF16)
    return jnp.concatenate([tile] * (n // pack), axis=0)


def _peer_kernel(x_ref, wq_ref, keys_ref, cidx_ref, u0_ref, u1_ref, u2_ref, vt_ref, g_ref, b_ref,
                 out_ref, xb_ref, qt_ref, c1_ref, phi_ref, r2_ref, psi_ref, uta_ref, utb_ref, wa_ref, wb_ref,
                 acc_ref,
                 *, alpha, heads, nk, rows_per_step):
    j = pl.program_id(1)
    nt = (((1,), (1,)), ((), ()))
    eb = rows_per_step * nk

    @pl.when(j == 0)
    def _():
        xb = x_ref[...].astype(BF16)
        xb_ref[...] = xb
        acc_ref[...] = jnp.zeros_like(acc_ref)
        qt_ref[...] = lax.dot_general(wq_ref[...], xb, nt, preferred_element_type=F32).astype(BF16)

        def select_head(h, carry):
            sc = []
            for c in range(2):
                hc = h * 2 + c
                qt = qt_ref[pl.ds(pl.multiple_of(hc * LANES, LANES), LANES), :]
                sc.append(jnp.dot(keys_ref[hc], qt, preferred_element_type=F32))
            def put(c1, phi, r2, psi):
                c1_ref[h] = c1
                phi_ref[h] = phi
                r2_ref[h] = r2.astype(BF16)
                psi_ref[h] = psi.astype(BF16)

            if nk // SUBLANES == PEER_TOPK:
                *sel, bad = _peer_select_fast(sc[0], sc[1], cidx_ref[...])
                put(*sel)

                @pl.when(jnp.max(bad) > 0.0)
                def _():
                    put(*_peer_select(sc[0], sc[1], cidx_ref[...]))
            else:
                put(*_peer_select(sc[0], sc[1], cidx_ref[...]))
            return carry

        lax.fori_loop(0, heads, select_head, 0)
        uta_ref[...] = lax.dot_general(u0_ref[...], xb, nt, preferred_element_type=F32)

    xb = xb_ref[...]

    def mix_block(block, ut_ref, w_ref):
        for r in range(rows_per_step):
            n1 = block * rows_per_step + r
            gsum = None
            for h in range(heads):
                c1row = _rows_bf16(c1_ref[h, pl.ds(n1, 1), :], nk)
                phirow = _rows_bf16(phi_ref[h, pl.ds(n1, 1), :], nk)
                term = jnp.where(r2_ref[h] < c1row, phirow * psi_ref[h], jnp.zeros((), BF16))
                gsum = term if gsum is None else gsum + term
            act = _gelu(ut_ref[r * nk:(r + 1) * nk, :]).astype(BF16)
            w_ref[r * nk:(r + 1) * nk, :] = gsum * act

    utb_ref[...] = lax.dot_general(u1_ref[...], xb, nt, preferred_element_type=F32)
    mix_block(2 * j, uta_ref, wa_ref)
    acc_ref[...] += jnp.dot(vt_ref[:, :eb], wa_ref[...], preferred_element_type=F32)
    uta_ref[...] = lax.dot_general(u2_ref[...], xb, nt, preferred_element_type=F32)
    mix_block(2 * j + 1, utb_ref, wb_ref)
    acc_ref[...] += jnp.dot(vt_ref[:, eb:], wb_ref[...], preferred_element_type=F32)

    @pl.when(j == pl.num_programs(1) - 1)
    def _():
        ffn = acc_ref[...].T
        out_ref[...] = _layer_norm(alpha * x_ref[...] + ffn, g_ref[...], b_ref[...])


def _peer(x, wq_t, keys, u_tab, v_tab_t, g, b, *, alpha):
    t, d = x.shape
    heads, _, nk, dh = keys.shape
    n_exp = u_tab.shape[0]
    tm = 512 if t % 512 == 0 else _row_tile(t, 512)
    rows = PEER_EXPERT_ROWS
    eb = rows * nk
    keys2 = keys.reshape(heads * 2, nk, dh)
    cidx = _candidate_index_rows(tm)
    n_blocks = n_exp // eb
    return pl.pallas_call(
        functools.partial(_peer_kernel, alpha=alpha, heads=heads, nk=nk, rows_per_step=rows),
        grid=(t // tm, n_blocks // 2),
        in_specs=[
            pl.BlockSpec((tm, d), lambda i, j: (i, 0)),
            pl.BlockSpec(wq_t.shape, lambda i, j: (0, 0)),
            pl.BlockSpec(keys2.shape, lambda i, j: (0, 0, 0)),
            pl.BlockSpec(cidx.shape, lambda i, j: (0, 0)),
            pl.BlockSpec((eb, d), lambda i, j: (0, 0)),
            pl.BlockSpec((eb, d), lambda i, j: (2 * j + 1, 0)),
            pl.BlockSpec((eb, d), lambda i, j: ((2 * j + 2) % n_blocks, 0)),
            pl.BlockSpec((d, 2 * eb), lambda i, j: (0, j)),
            pl.BlockSpec((1, d), lambda i, j: (0, 0)),
            pl.BlockSpec((1, d), lambda i, j: (0, 0)),
        ],
        out_specs=pl.BlockSpec((tm, d), lambda i, j: (i, 0)),
        out_shape=jax.ShapeDtypeStruct((t, d), F32),
        scratch_shapes=[
            pltpu.VMEM((tm, d), BF16),
            pltpu.VMEM((wq_t.shape[0], tm), BF16),
            pltpu.VMEM((heads, nk, tm), F32), pltpu.VMEM((heads, nk, tm), F32),
            pltpu.VMEM((heads, nk, tm), BF16), pltpu.VMEM((heads, nk, tm), BF16),
            pltpu.VMEM((eb, tm), F32), pltpu.VMEM((eb, tm), F32),
            pltpu.VMEM((eb, tm), BF16), pltpu.VMEM((eb, tm), BF16),
            pltpu.VMEM((d, tm), F32),
        ],
        compiler_params=_cparams("parallel", "arbitrary"),
        name="peer",
    )(x, wq_t, keys2, cidx, u_tab, u_tab, u_tab, v_tab_t, g.reshape(1, d), b.reshape(1, d))


def _pad_tokens(a, length):
    return jnp.pad(a, ((0, 0), (0, length - a.shape[1]), (0, 0)))


def _sequence_mixers(proj3, lb, s_hg, s_re, s_im, s5_prm, d_hg):
    bsz, length, _ = proj3.shape
    lpad = -(-length // SUBLANES) * SUBLANES
    o, s_hg_new = _hgrn(_pad_tokens(proj3, lpad) if lpad != length else proj3, lb, s_hg, valid=length)
    lc = S5_CHUNK if length % S5_CHUNK == 0 else length
    y, s_re_new, s_im_new = _s5(proj3.reshape(bsz * length, -1), s_re, s_im, s5_prm[lc],
                                length=length, lc=lc, u_col=4 * d_hg)
    return o[:, :length], y.reshape(bsz, length, -1), s_hg_new, s_re_new, s_im_new


def kernel(x_prompt, x_sample, state_hgrn, state_ssm_re, state_ssm_im, meta_tokens, ln_emb_g, ln_emb_b,
           lb_logits, w_in, b_in, hg_norm_g, ssm_a_re, ssm_a_im, ssm_log_dt, ssm_b_re, ssm_b_im,
           ssm_c_re, ssm_c_im, ssm_d, w_glu, b_glu, w_out, ln1_g, ln1_b, peer_w_q, peer_keys,
           peer_u, peer_v, ln2_g, ln2_b):
    depth = w_in.shape[0]
    alpha = (2.0 * depth) ** 0.25
    bp, seq, d = x_prompt.shape
    bs, dseq, _ = x_sample.shape
    n_meta = meta_tokens.shape[0]
    heads = state_hgrn.shape[2]
    d_hg = heads * LANES
    groups, pstate = state_ssm_re.shape[2], state_ssm_re.shape[3]
    lbs = jnp.cumsum(jax.nn.softmax(lb_logits.astype(F32), axis=0), axis=0)

    xp = x_prompt.astype(F32).reshape(bp * seq, d)
    xm = meta_tokens.astype(F32)
    xs = x_sample.astype(F32).reshape(bs * dseq, d)
    n_s = bs * dseq
    hg_p, re_p, im_p, hg_s, re_s, im_s = [], [], [], [], [], []
    for l in range(depth):
        last = l == depth - 1
        w_in_b = w_in[l].astype(BF16)
        lb = lbs[l].reshape(1, d_hg)
        s5_prm = {lc: _s5_params(ssm_a_re[l], ssm_a_im[l], ssm_log_dt[l], ssm_b_re[l], ssm_b_im[l],
                                 ssm_c_re[l], ssm_c_im[l], ssm_d[l], lc)
                  for lc in {S5_CHUNK if n % S5_CHUNK == 0 else n for n in (n_meta, seq, dseq)}}
        wglu = w_glu[l].astype(BF16)
        wo_hg = w_out[l, :d_hg].astype(BF16)
        wo_ssm = w_out[l, d_hg:].astype(BF16)
        wq_t = peer_w_q[l].T.astype(BF16)
        keys = peer_keys[l].astype(BF16)
        u_tab = peer_u[l].astype(BF16)
        v_tab_t = peer_v[l].T.astype(BF16)

        xsm = jnp.concatenate([xs, xm], axis=0)
        x0_p, proj_p = _inproj(xp, ln_emb_g, ln_emb_b, w_in_b, b_in[l], apply_ln=(l == 0))
        x0_sm, proj_sm = _inproj(xsm, ln_emb_g, ln_emb_b, w_in_b, b_in[l], apply_ln=(l == 0))
        d_in = proj_p.shape[1]

        proj_m = proj_sm[n_s:].reshape(1, n_meta, d_in)
        proj_m8 = jnp.broadcast_to(proj_m, (SUBLANES, n_meta, d_in))
        zero_hg = jnp.zeros((SUBLANES, heads, LANES, LANES), F32)
        zero_ss = jnp.zeros((SUBLANES, groups, pstate), F32)
        o_m, y_m, hg_m, re_m, im_m = _sequence_mixers(proj_m8, lb, zero_hg, zero_ss, zero_ss, s5_prm, d_hg)

        o_p, y_p, shg, sre, sim = _sequence_mixers(
            proj_p.reshape(bp, seq, d_in), lb,
            jnp.broadcast_to(hg_m[:1], (bp,) + hg_m.shape[1:]),
            jnp.broadcast_to(re_m[:1], (bp,) + re_m.shape[1:]),
            jnp.broadcast_to(im_m[:1], (bp,) + im_m.shape[1:]), s5_prm, d_hg)
        hg_p.append(shg)
        re_p.append(sre)
        im_p.append(sim)

        o_s, y_s, shg, sre, sim = _sequence_mixers(
            proj_sm[:n_s].reshape(bs, dseq, d_in), lb, state_hgrn[l].astype(F32),
            state_ssm_re[l].astype(F32), state_ssm_im[l].astype(F32), s5_prm, d_hg)
        hg_s.append(shg)
        re_s.append(sre)
        im_s.append(sim)

        mix_args = (hg_norm_g[l], wglu, b_glu[l], wo_hg, wo_ssm, ln1_g[l], ln1_b[l])
        peer_args = (wq_t, keys, u_tab, v_tab_t, ln2_g[l], ln2_b[l])
        x1_p = _mix(x0_p, o_p.reshape(bp * seq, d_hg), proj_p, y_p.reshape(bp * seq, -1), *mix_args, alpha=alpha)
        xp = _peer(x1_p, *peer_args, alpha=alpha)
        x1_s = _mix(x0_sm[:n_s], o_s.reshape(n_s, d_hg), proj_sm[:n_s], y_s.reshape(n_s, -1), *mix_args, alpha=alpha)
        xs = _peer(x1_s, *peer_args, alpha=alpha)
        if not last:
            x1_m = _mix(jnp.broadcast_to(x0_sm[n_s:], (n_meta, d)), o_m[0], proj_sm[n_s:], y_m[0], *mix_args, alpha=alpha)
            xm = _peer(x1_m, *peer_args, alpha=alpha)

    y_prompt = xp.reshape(bp, seq, d).astype(x_prompt.dtype)
    y_sample = xs.reshape(bs, dseq, d).astype(x_sample.dtype)
    return (y_prompt, y_sample, jnp.stack(hg_p), jnp.stack(re_p), jnp.stack(im_p),
            jnp.stack(hg_s), jnp.stack(re_s), jnp.stack(im_s))
```

```python
import functools
import math

import jax
import jax.numpy as jnp
from jax import lax
from jax.experimental import pallas as pl
from jax.experimental.pallas import tpu as pltpu

F32 = jnp.float32
BF16 = jnp.bfloat16

LN_EPS = 1e-5
RMS_EPS = 1e-6
HG_CHUNK = 16
HG_CHUNKS_PER_ITER = 8
HG_SEQS_PER_STEP = 16
SSM_GROUP = 16
S5_CHUNK = 16
S5_GROUP_TILE = 8
S5_SEQS_PER_STEP = 4
PEER_TOPK = 16
LANES = 128
SUBLANES = 8
VMEM_LIMIT = 62 * 1024 * 1024
PEER_EXPERT_ROWS = 8


def _cparams(*sem):
    return pltpu.CompilerParams(dimension_semantics=sem, vmem_limit_bytes=VMEM_LIMIT)


def _layer_norm(x, g, b):
    mu = jnp.mean(x, axis=-1, keepdims=True)
    xc = x - mu
    var = jnp.mean(xc * xc, axis=-1, keepdims=True)
    return xc * lax.rsqrt(var + LN_EPS) * g + b


def _sigmoid(x):
    return 1.0 / (1.0 + jnp.exp(-x))


def _gelu(x):
    c1 = -2.0 * math.sqrt(2.0 / math.pi)
    return x / (1.0 + jnp.exp(x * (c1 + (c1 * 0.044715) * (x * x))))


def _row_tile(t, target):
    best = None
    for cand in range(SUBLANES, min(t, target) + 1, SUBLANES):
        if t % cand == 0:
            best = cand
    return best if best is not None else t


def _inproj_kernel(x_ref, g_ref, b_ref, w_ref, bi_ref, x0_ref, proj_ref, *, apply_ln):
    x = x_ref[...]
    if apply_ln:
        x = _layer_norm(x, g_ref[...], b_ref[...])
    x0_ref[...] = x
    proj_ref[...] = jnp.dot(x.astype(BF16), w_ref[...], preferred_element_type=F32) + bi_ref[...]


def _inproj(x, g, b, w_bf16, bias, *, apply_ln):
    t, d = x.shape
    d_in = w_bf16.shape[1]
    tm = _row_tile(t, 512)
    return pl.pallas_call(
        functools.partial(_inproj_kernel, apply_ln=apply_ln),
        grid=(t // tm,),
        in_specs=[
            pl.BlockSpec((tm, d), lambda i: (i, 0)),
            pl.BlockSpec((1, d), lambda i: (0, 0)),
            pl.BlockSpec((1, d), lambda i: (0, 0)),
            pl.BlockSpec((d, d_in), lambda i: (0, 0)),
            pl.BlockSpec((1, d_in), lambda i: (0, 0)),
        ],
        out_specs=[
            pl.BlockSpec((tm, d), lambda i: (i, 0)),
            pl.BlockSpec((tm, d_in), lambda i: (i, 0)),
        ],
        out_shape=[jax.ShapeDtypeStruct((t, d), F32), jax.ShapeDtypeStruct((t, d_in), F32)],
        compiler_params=_cparams("parallel"),
        name="inproj",
    )(x, g.reshape(1, d), b.reshape(1, d), w_bf16, bias.reshape(1, d_in))


def _hgrn_kernel(q_ref, f_ref, v_ref, lb_ref, s0_ref, o_ref, s_ref, st_ref,
                 *, chunk, n_iters, per_iter, seqs, valid):
    lb = lb_ref[...]
    ones = jnp.ones((LANES, LANES), BF16)
    groups = seqs * per_iter
    shape = (groups, chunk, LANES)
    row = lax.broadcasted_iota(jnp.int32, shape, 1)
    span = per_iter * chunk
    nt = (((1,), (1,)), ((), ()))
    tn = (((0,), (0,)), ((), ()))

    for b in range(seqs):
        st_ref[b] = s0_ref[b].T

    def take(ref, r0):
        return jnp.concatenate([ref[b, pl.ds(r0, span), :].reshape(per_iter, chunk, LANES)
                                for b in range(seqs)], axis=0)

    def step(i, carry):
        r0 = pl.multiple_of(i * span, span)
        q = take(q_ref, r0)
        fp = take(f_ref, r0)
        v = take(v_ref, r0)
        f = lb + (1.0 - lb) * _sigmoid(fp)
        logf = jnp.log(f)
        kk = (1.0 - lb) * _sigmoid(-fp)
        if valid < chunk:
            live = row < valid
            logf = jnp.where(live, logf, 0.0)
            kk = jnp.where(live, kk, 0.0)
            q = jnp.where(live, q, 0.0)
            v = jnp.where(live, v, 0.0)
        pick = lambda a, s: jnp.broadcast_to(a[:, s:s + 1, :], shape)
        bc = jnp.zeros(shape, F32)
        for s in range(chunk):
            bc = bc + jnp.where(row >= s, pick(logf, s), 0.0)
        split = chunk // 2 if chunk % (2 * SUBLANES) == 0 else 0
        lows = [split if s >= split else 0 for s in range(chunk)]
        prods = []
        for s, lo in zip(range(chunk), lows):
            part = (groups, chunk - lo, LANES)
            near = lambda a: jnp.broadcast_to(a[:, s:s + 1, :], part)
            e = jnp.where(row[:, lo:, :] >= s, jnp.exp(bc[:, lo:, :] - near(bc)), 0.0)
            prods.append((q[:, lo:, :] * e * near(kk)).reshape(groups * (chunk - lo), LANES))
        p_all = jnp.concatenate(prods, axis=0).astype(BF16)
        att = jnp.dot(p_all, ones, preferred_element_type=F32)
        o = jnp.zeros(shape, F32)
        o_low = jnp.zeros((groups, chunk - split, LANES), F32)
        off = 0
        for s, lo in zip(range(chunk), lows):
            part = (groups, chunk - lo, LANES)
            n = groups * (chunk - lo)
            term = att[off:off + n, :].reshape(part) * jnp.broadcast_to(v[:, s:s + 1, :], part)
            off += n
            if lo == 0:
                o = o + term
            else:
                o_low = o_low + term
        if split:
            o = o + jnp.concatenate([jnp.zeros((groups, split, LANES), F32), o_low], axis=1)
        qs = (q * jnp.exp(bc)).astype(BF16)
        kt = (kk * jnp.exp(pick(bc, chunk - 1) - bc)).astype(BF16)
        vb = v.astype(BF16)
        decay = jnp.exp(bc[:, chunk - 1:chunk, :])
        ds = [lax.dot_general(vb[g], kt[g], tn, preferred_element_type=F32) for g in range(groups)]
        outs = []
        for b in range(seqs):
            st = st_ref[b]
            before = []
            for n in range(per_iter):
                g = b * per_iter + n
                before.append(st.astype(BF16))
                st = st * decay[g] + ds[g]
            st_ref[b] = st
            for n in range(per_iter):
                g = b * per_iter + n
                outs.append(o[g] + lax.dot_general(qs[g], before[n], nt, preferred_element_type=F32))
        for b in range(seqs):
            ob = jnp.concatenate(outs[b * per_iter:(b + 1) * per_iter], axis=0)
            ob = ob * lax.rsqrt(jnp.mean(ob * ob, axis=-1, keepdims=True) + RMS_EPS)
            o_ref[b, pl.ds(r0, span), :] = ob
        return carry

    lax.fori_loop(0, n_iters, step, 0)
    for b in range(seqs):
        s_ref[b] = st_ref[b].T


def _hgrn(proj3, lb, s0, *, valid):
    bsz, length, _ = proj3.shape
    heads = s0.shape[1]
    chunk = HG_CHUNK if length % HG_CHUNK == 0 else length
    n_chunks = length // chunk
    divisor = lambda n, cap: max(c for c in range(1, cap + 1) if n % c == 0)
    if n_chunks > 1:
        seqs, per_iter = 1, divisor(n_chunks, HG_CHUNKS_PER_ITER)
    else:
        seqs, per_iter = divisor(bsz, HG_SEQS_PER_STEP), 1
    seq_spec = lambda off: pl.BlockSpec((seqs, length, LANES), lambda b, h: (b, 0, off + h))
    state_spec = pl.BlockSpec((seqs, None, LANES, LANES), lambda b, h: (b, h, 0, 0))
    return pl.pallas_call(
        functools.partial(_hgrn_kernel, chunk=chunk, n_iters=n_chunks // per_iter, per_iter=per_iter,
                          seqs=seqs, valid=min(valid, chunk)),
        grid=(bsz // seqs, heads),
        in_specs=[
            seq_spec(0), seq_spec(heads), seq_spec(2 * heads),
            pl.BlockSpec((1, LANES), lambda b, h: (0, h)),
            state_spec,
        ],
        out_specs=[
            pl.BlockSpec((seqs, length, LANES), lambda b, h: (b, 0, h)),
            state_spec,
        ],
        out_shape=[jax.ShapeDtypeStruct((bsz, length, heads * LANES), F32),
                   jax.ShapeDtypeStruct(s0.shape, F32)],
        scratch_shapes=[pltpu.VMEM((seqs, LANES, LANES), F32)],
        compiler_params=_cparams("parallel", "parallel"),
        name="hgrn2",
    )(proj3, proj3, proj3, lb, s0)


def _s5_kernel(u_ref, bb_ref, c_ref, are_ref, aim_ref, alre_ref, alim_ref, d_ref, h0re_ref, h0im_ref,
               y_ref, hre_ref, him_ref, hpre_ref, hpim_ref, *, lc, n_chunks, seqs):
    rows = seqs * n_chunks
    half = are_ref.shape[-1]
    a_re = are_ref[...]
    a_im = aim_ref[...]
    bb = bb_ref[...]

    def drive(t):
        u_t = u_ref[pl.ds(t, rows, stride=lc), :]
        return u_t, jnp.dot(u_t.astype(BF16), bb, preferred_element_type=F32)

    def advance(h_re, h_im, bu):
        return (a_re * h_re - a_im * h_im + bu[:, :half], a_re * h_im + a_im * h_re + bu[:, half:])

    h_re = jnp.zeros((rows, half), F32)
    h_im = jnp.zeros((rows, half), F32)
    for t in range(lc):
        h_re, h_im = advance(h_re, h_im, drive(t)[1])

    n_tiles = half // LANES

    def put(ref, idx, val):
        for k in range(n_tiles):
            ref[k, idx, :] = val[:, k * LANES:(k + 1) * LANES]

    def get(ref, idx):
        return jnp.concatenate([ref[k, idx, :] for k in range(n_tiles)], axis=-1)

    put(hpre_ref, slice(None), h_re)
    put(hpim_ref, slice(None), h_im)

    al_re = alre_ref[...]
    al_im = alim_ref[...]

    def carry_step(c, carry):
        c_re, c_im = carry
        idx = pl.ds(c, seqs, stride=n_chunks)
        l_re = get(hpre_ref, idx)
        l_im = get(hpim_ref, idx)
        put(hpre_ref, idx, c_re)
        put(hpim_ref, idx, c_im)
        return (al_re * c_re - al_im * c_im + l_re, al_re * c_im + al_im * c_re + l_im)

    c_re, c_im = lax.fori_loop(0, n_chunks, carry_step, (h0re_ref[...], h0im_ref[...]))
    hre_ref[...] = c_re
    him_ref[...] = c_im

    h_re = get(hpre_ref, slice(None))
    h_im = get(hpim_ref, slice(None))
    c_w = c_ref[...]
    d_vec = d_ref[...]
    for t in range(lc):
        u_t, bu = drive(t)
        h_re, h_im = advance(h_re, h_im, bu)
        y = (jnp.dot(h_re.astype(BF16), c_w[:half], preferred_element_type=F32)
             + jnp.dot(h_im.astype(BF16), c_w[half:], preferred_element_type=F32) + d_vec * u_t)
        y_ref[pl.ds(t, rows, stride=lc), :] = y


def _s5_params(a_re, a_im, log_dt, b_re, b_im, c_re, c_im, d, lc):
    groups, pstate = a_re.shape
    dt = jnp.exp(log_dt.astype(F32))[:, None]
    a_re = a_re.astype(F32)
    a_im = a_im.astype(F32)
    zr, zi = a_re * dt, a_im * dt
    mag = jnp.exp(zr)
    ab_re, ab_im = mag * jnp.cos(zi), mag * jnp.sin(zi)
    den = a_re * a_re + a_im * a_im
    nr = ab_re - 1.0
    coef_re = (nr * a_re + ab_im * a_im) / den
    coef_im = (ab_im * a_re - nr * a_im) / den
    b_re = b_re.astype(F32)
    b_im = b_im.astype(F32)
    bb_re = coef_re[..., None] * b_re - coef_im[..., None] * b_im
    bb_im = coef_re[..., None] * b_im + coef_im[..., None] * b_re
    pm = jnp.exp(zr * lc)
    al_re, al_im = pm * jnp.cos(zi * lc), pm * jnp.sin(zi * lc)
    gt = S5_GROUP_TILE
    tiles = groups // gt
    swap = lambda x: jnp.transpose(x, (0, 2, 1))
    bb = jnp.concatenate([_block_diag(swap(bb_re), gt), _block_diag(swap(bb_im), gt)], axis=-1)
    cw = jnp.concatenate([_block_diag(swap(c_re.astype(F32)), gt), -_block_diag(swap(c_im.astype(F32)), gt)], axis=1)
    lane = lambda x: x.reshape(tiles, 1, gt * pstate)
    return dict(bb=bb.astype(BF16), cw=cw.astype(BF16), are=lane(ab_re), aim=lane(ab_im),
                alre=lane(al_re), alim=lane(al_im), d=d.astype(F32).reshape(1, groups * SSM_GROUP))


def _block_diag(x, gm):
    if gm == 1:
        return x
    g, a, b = x.shape
    x = x.reshape(g // gm, gm, a, b)
    eye = jnp.eye(gm, dtype=x.dtype)
    return (x[:, :, :, None, :] * eye[None, :, None, :, None]).reshape(g // gm, gm * a, gm * b)


def _s5(proj, h0_re, h0_im, prm, *, length, lc, u_col):
    bsz, n_groups, pstate = h0_re.shape
    d_ssm = n_groups * SSM_GROUP
    tiles = n_groups // S5_GROUP_TILE
    half = S5_GROUP_TILE * pstate
    n_chunks = length // lc
    seqs = bsz if n_chunks == 1 else min(bsz, S5_SEQS_PER_STEP)
    steps = bsz // seqs
    rows = seqs * n_chunks
    h0r = h0_re.reshape(steps, seqs, n_groups * pstate)
    h0i = h0_im.reshape(steps, seqs, n_groups * pstate)
    tspec = lambda a, b: pl.BlockSpec((None, a, b), lambda i, g: (g, 0, 0))
    sspec = pl.BlockSpec((None, seqs, half), lambda i, g: (i, 0, g))
    y, hre, him = pl.pallas_call(
        functools.partial(_s5_kernel, lc=lc, n_chunks=n_chunks, seqs=seqs),
        grid=(steps, tiles),
        in_specs=[
            pl.BlockSpec((seqs * length, LANES), lambda i, g: (i, u_col // LANES + g)),
            tspec(LANES, 2 * half), tspec(2 * half, LANES),
            tspec(1, half), tspec(1, half), tspec(1, half), tspec(1, half),
            pl.BlockSpec((1, LANES), lambda i, g: (0, g)),
            sspec, sspec,
        ],
        out_specs=[
            pl.BlockSpec((seqs * length, LANES), lambda i, g: (i, g)),
            sspec, sspec,
        ],
        out_shape=[jax.ShapeDtypeStruct((bsz * length, d_ssm), F32),
                   jax.ShapeDtypeStruct(h0r.shape, F32),
                   jax.ShapeDtypeStruct(h0r.shape, F32)],
        scratch_shapes=[pltpu.VMEM((half // LANES, rows, LANES), F32),
                        pltpu.VMEM((half // LANES, rows, LANES), F32)],
        compiler_params=_cparams("parallel", "parallel"),
        name="s5",
    )(proj, prm['bb'], prm['cw'], prm['are'], prm['aim'], prm['alre'], prm['alim'], prm['d'], h0r, h0i)
    return y, hre.reshape(h0_re.shape), him.reshape(h0_re.shape)


def _mix_kernel(x_ref, o_ref, gate_ref, y_ref, hg_g_ref, wglu_ref, bglu_ref, wo_hg_ref, wo_ssm_ref,
                g_ref, b_ref, out_ref, *, alpha):
    gate = gate_ref[...]
    o_hg = o_ref[...] * hg_g_ref[...] * (gate * _sigmoid(gate))
    z = _gelu(y_ref[...])
    glu = jnp.dot(z.astype(BF16), wglu_ref[...], preferred_element_type=F32) + bglu_ref[...]
    o_ssm = z * _sigmoid(glu)
    mix = (jnp.dot(o_hg.astype(BF16), wo_hg_ref[...], preferred_element_type=F32)
           + jnp.dot(o_ssm.astype(BF16), wo_ssm_ref[...], preferred_element_type=F32))
    out_ref[...] = _layer_norm(alpha * x_ref[...] + mix, g_ref[...], b_ref[...])


def _mix(x0, o_hg, proj, y_ssm, hg_g, wglu, bglu, wo_hg, wo_ssm, g, b, *, alpha):
    t, d = x0.shape
    d_hg = o_hg.shape[1]
    d_ssm = y_ssm.shape[1]
    tm = _row_tile(t, 512)
    gate_block = (3 * d_hg) // d_hg
    row = lambda w: pl.BlockSpec((tm, w), lambda i: (i, 0))
    full = lambda a, bb: pl.BlockSpec((a, bb), lambda i: (0, 0))
    return pl.pallas_call(
        functools.partial(_mix_kernel, alpha=alpha),
        grid=(t // tm,),
        in_specs=[
            row(d), row(d_hg),
            pl.BlockSpec((tm, d_hg), lambda i: (i, gate_block)),
            row(d_ssm),
            full(1, d_hg), full(d_ssm, d_ssm), full(1, d_ssm), full(d_hg, d), full(d_ssm, d),
            full(1, d), full(1, d),
        ],
        out_specs=row(d),
        out_shape=jax.ShapeDtypeStruct((t, d), F32),
        compiler_params=_cparams("parallel"),
        name="mix",
    )(x0, o_hg, proj, y_ssm, hg_g.reshape(1, d_hg), wglu, bglu.reshape(1, d_ssm), wo_hg, wo_ssm,
      g.reshape(1, d), b.reshape(1, d))


def _top_rows(s, k):
    n, t = s.shape
    iota = lax.broadcasted_iota(jnp.int32, (n, t), 0).astype(F32)
    krow = lax.broadcasted_iota(jnp.int32, (k, t), 0)
    rank = jnp.full((n, t), float(k), F32)
    vals = jnp.zeros((k, t), F32)
    for a in range(k):
        m = jnp.max(s, axis=0, keepdims=True)
        idx = jnp.min(jnp.where(s == m, iota, float(n)), axis=0, keepdims=True)
        hit = iota == idx
        rank = jnp.where(hit, float(a), rank)
        s = jnp.where(hit, -jnp.inf, s)
        vals = jnp.where(krow == a, jnp.broadcast_to(m, (k, t)), vals)
    return vals, rank


def _candidate_pieces(t1, t2):
    k = PEER_TOPK
    t = t1.shape[1]
    bc = lambda r, n: jnp.broadcast_to(r, (n, t))
    pieces = [(bc(t1[0:1], k), t2, [b for b in range(k)], [True] * k)]
    half = k // 2
    for a in range(1, half):
        nb = k // (a + 1)
        pieces.append((bc(t1[a:a + 1], half), t2[0:half], [a * k + b for b in range(half)],
                       [b < nb for b in range(half)]))
    pieces.append((t1[half:k], bc(t2[0:1], half), [(half + r) * k for r in range(half)], [True] * half))
    return pieces


def _candidate_index_rows(t):
    k = PEER_TOPK
    dummy = jnp.zeros((k, 1), F32)
    vals = [i if ok else k * k for _, _, idx, val in _candidate_pieces(dummy, dummy) for i, ok in zip(idx, val)]
    return jnp.broadcast_to(jnp.asarray(vals, F32)[:, None], (len(vals), t))


def _peer_select(s1, s2, cidx):
    k = PEER_TOPK
    t = s1.shape[1]
    big = float(k * k)
    top1, rank1 = _top_rows(s1, k)
    top2, rank2 = _top_rows(s2, k)
    pieces = _candidate_pieces(top1, top2)
    cand = jnp.concatenate([x + y for x, y, _, _ in pieces], axis=0)
    cand = jnp.where(cidx < big, cand, -jnp.inf)
    chosen = jnp.zeros(cand.shape, F32)
    for _ in range(k):
        m = jnp.max(cand, axis=0, keepdims=True)
        idx = jnp.min(jnp.where(cand == m, cidx, big), axis=0, keepdims=True)
        hit = cidx == idx
        chosen = jnp.where(hit, 1.0, chosen)
        cand = jnp.where(hit, -jnp.inf, cand)
    e1 = jnp.exp(top1 - jnp.broadcast_to(top1[0:1], (k, t)))
    e2 = jnp.exp(top2 - jnp.broadcast_to(top2[0:1], (k, t)))
    ecand = jnp.concatenate([x * y for x, y, _, _ in _candidate_pieces(e1, e2)], axis=0)
    z = jnp.sum(chosen * ecand, axis=0, keepdims=True)
    counts = []
    off = 0
    half = k // 2
    for pi, (x, _, _, _) in enumerate(pieces):
        rows = x.shape[0]
        blk = chosen[off:off + rows]
        off += rows
        if pi < len(pieces) - 1:
            counts.append(jnp.sum(blk, axis=0, keepdims=True))
        else:
            counts.extend(blk[r:r + 1] for r in range(half))
    nk = s1.shape[0]
    c1 = jnp.zeros((nk, t), F32)
    for a in range(k):
        c1 = jnp.where(rank1 == float(a), jnp.broadcast_to(counts[a], (nk, t)), c1)
    phi = jnp.exp(s1 - jnp.broadcast_to(top1[0:1], (nk, t))) * jnp.broadcast_to(1.0 / z, (nk, t))
    psi = jnp.exp(s2 - jnp.broadcast_to(top2[0:1], (nk, t)))
    return c1, phi, rank2, psi


def _sort_network(n):
    out, p = [], 1
    while p < n:
        k = p
        while k >= 1:
            for j in range(k % p, n - k, 2 * k):
                for i in range(min(k, n - j - k)):
                    if (i + j) // (2 * p) == (i + j + k) // (2 * p):
                        out.append((i + j, i + j + k))
            k //= 2
        p *= 2
    return out


def _sorted_top(s):
    n = s.shape[0] // SUBLANES
    x = [s[i * SUBLANES:(i + 1) * SUBLANES, :] for i in range(n)]
    for i, j in _sort_network(n):
        x[i], x[j] = jnp.maximum(x[i], x[j]), jnp.minimum(x[i], x[j])
    shift = SUBLANES // 2
    while shift >= 1:
        r = [pltpu.roll(v, shift, axis=0) for v in x]
        x = [jnp.maximum(x[i], r[n - 1 - i]) for i in range(n)]
        d = n // 2
        while d >= 1:
            for i in range(n):
                if (i & d) == 0:
                    x[i], x[i + d] = jnp.maximum(x[i], x[i + d]), jnp.minimum(x[i], x[i + d])
            d //= 2
        shift //= 2
    return x


def _peer_select_fast(s1, s2, cidx):
    k = PEER_TOPK
    nk, t = s1.shape
    nblk = nk // SUBLANES
    big = float(k * k)
    l1 = _sorted_top(s1)
    l2 = _sorted_top(s2)
    sub = lax.broadcasted_iota(jnp.int32, (SUBLANES, t), 0)

    def stack(lst):
        halves = []
        for base in (0, SUBLANES):
            blk = lst[base]
            for a in range(1, SUBLANES):
                blk = jnp.where(sub == a, lst[base + a], blk)
            halves.append(blk)
        return jnp.concatenate(halves, axis=0)

    top1 = stack(l1)
    top2 = stack(l2)
    pieces = _candidate_pieces(top1, top2)
    cand = jnp.concatenate([x + y for x, y, _, _ in pieces], axis=0)
    cand = jnp.where(cidx < big, cand, -jnp.inf)
    chosen = jnp.zeros(cand.shape, F32)
    for _ in range(k):
        hit = cand == jnp.max(cand, axis=0, keepdims=True)
        chosen = jnp.where(hit, 1.0, chosen)
        cand = jnp.where(hit, -jnp.inf, cand)
    e1 = jnp.exp(top1 - jnp.broadcast_to(top1[0:1], (k, t)))
    e2 = jnp.exp(top2 - jnp.broadcast_to(top2[0:1], (k, t)))
    ecand = jnp.concatenate([x * y for x, y, _, _ in _candidate_pieces(e1, e2)], axis=0)
    z = jnp.sum(chosen * ecand, axis=0, keepdims=True)
    counts = []
    off = 0
    half = k // 2
    for pi, (x, _, _, _) in enumerate(pieces):
        rows = x.shape[0]
        blk = chosen[off:off + rows]
        off += rows
        if pi < len(pieces) - 1:
            counts.append(jnp.sum(blk, axis=0, keepdims=True))
        else:
            counts.extend(blk[r:r + 1] for r in range(half))
    c1_blocks, r2_blocks = [], []
    cnt1 = jnp.zeros((SUBLANES, t), F32)
    cnt2 = jnp.zeros((SUBLANES, t), F32)
    for i in range(nblk):
        x1 = s1[i * SUBLANES:(i + 1) * SUBLANES, :]
        x2 = s2[i * SUBLANES:(i + 1) * SUBLANES, :]
        c1b = jnp.zeros((SUBLANES, t), F32)
        r2b = jnp.zeros((SUBLANES, t), F32)
        for a in range(k):
            c1b = jnp.where(x1 == l1[a], jnp.broadcast_to(counts[a], (SUBLANES, t)), c1b)
            r2b = jnp.where(x2 < l2[a], float(a + 1), r2b)
        c1_blocks.append(c1b)
        r2_blocks.append(r2b)
        cnt1 = cnt1 + jnp.where(x1 >= l1[k - 1], 1.0, 0.0)
        cnt2 = cnt2 + jnp.where(x2 >= l2[k - 1], 1.0, 0.0)
    c1 = jnp.concatenate(c1_blocks, axis=0)
    r2 = jnp.concatenate(r2_blocks, axis=0)
    phi = jnp.exp(s1 - jnp.broadcast_to(top1[0:1], (nk, t))) * jnp.broadcast_to(1.0 / z, (nk, t))
    psi = jnp.exp(s2 - jnp.broadcast_to(top2[0:1], (nk, t)))
    bad = jnp.zeros((SUBLANES, t), F32)
    for a in range(k - 1):
        bad = jnp.where(l1[a] <= l1[a + 1], 1.0, bad)
        bad = jnp.where(l2[a] <= l2[a + 1], 1.0, bad)
    fk = float(k)
    bad = jnp.max(bad, axis=0, keepdims=True)
    bad = jnp.where(jnp.sum(cnt1, axis=0, keepdims=True) != fk, 1.0, bad)
    bad = jnp.where(jnp.sum(cnt2, axis=0, keepdims=True) != fk, 1.0, bad)
    bad = jnp.where(jnp.sum(chosen, axis=0, keepdims=True) != fk, 1.0, bad)
    return c1, phi, r2, psi, bad


def _rows_bf16(row, n):
    pack = 2 * SUBLANES
    tile = jnp.broadcast_to(row, (pack, row.shape[1])).astype(BF16)
    return jnp.concatenate([tile] * (n // pack), axis=0)


def _peer_kernel(x_ref, wq_ref, keys_ref, cidx_ref, u_ref, vt_ref, g_ref, b_ref,
                 out_ref, xb_ref, qt_ref, ut_ref, c1_ref, phi_ref, r2_ref, psi_ref, wa_ref, wb_ref, acc_ref,
                 *, alpha, heads, nk, rows_per_block):
    j = pl.program_id(1)
    nt = (((1,), (1,)), ((), ()))
    eb = rows_per_block * nk

    @pl.when(j == 0)
    def _():
        xb = x_ref[...].astype(BF16)
        xb_ref[...] = xb
        acc_ref[...] = jnp.zeros_like(acc_ref)
        qt_ref[...] = lax.dot_general(wq_ref[...], xb, nt, preferred_element_type=F32).astype(BF16)

    @pl.when(j < heads)
    def _():
        h = j
        start = pl.multiple_of(h * (2 * eb), 2 * eb)
        ut_ref[pl.ds(start, 2 * eb), :] = lax.dot_general(
            u_ref[...], xb_ref[...], nt, preferred_element_type=F32).astype(BF16)
        sc = []
        for c in range(2):
            hc = h * 2 + c
            qt = qt_ref[pl.ds(pl.multiple_of(hc * LANES, LANES), LANES), :]
            sc.append(jnp.dot(keys_ref[hc], qt, preferred_element_type=F32))

        def put(c1, phi, r2, psi):
            c1_ref[h] = c1
            phi_ref[h] = phi
            r2_ref[h] = r2.astype(BF16)
            psi_ref[h] = psi.astype(BF16)

        if nk // SUBLANES == PEER_TOPK:
            *sel, bad = _peer_select_fast(sc[0], sc[1], cidx_ref[...])
            put(*sel)

            @pl.when(jnp.max(bad) > 0.0)
            def _():
                put(*_peer_select(sc[0], sc[1], cidx_ref[...]))
        else:
            put(*_peer_select(sc[0], sc[1], cidx_ref[...]))

    def mix_block(block, w_ref):
        for r in range(rows_per_block):
            n1 = block * rows_per_block + r
            gsum = None
            for h in range(heads):
                c1row = _rows_bf16(c1_ref[h, pl.ds(n1, 1), :], nk)
                phirow = _rows_bf16(phi_ref[h, pl.ds(n1, 1), :], nk)
                term = jnp.where(r2_ref[h] < c1row, phirow * psi_ref[h], jnp.zeros((), BF16))
                gsum = term if gsum is None else gsum + term
            pre = ut_ref[pl.ds(pl.multiple_of(n1 * nk, nk), nk), :].astype(F32)
            w_ref[r * nk:(r + 1) * nk, :] = gsum * _gelu(pre).astype(BF16)

    @pl.when(j >= heads)
    def _():
        m = j - heads
        mix_block(2 * m, wa_ref)
        acc_ref[...] += jnp.dot(vt_ref[:, :eb], wa_ref[...], preferred_element_type=F32)
        mix_block(2 * m + 1, wb_ref)
        acc_ref[...] += jnp.dot(vt_ref[:, eb:], wb_ref[...], preferred_element_type=F32)

    @pl.when(j == pl.num_programs(1) - 1)
    def _():
        ffn = acc_ref[...].T
        out_ref[...] = _layer_norm(alpha * x_ref[...] + ffn, g_ref[...], b_ref[...])


def _peer(x, wq_t, keys, u_tab, v_tab_t, g, b, *, alpha):
    t, d = x.shape
    heads, _, nk, dh = keys.shape
    n_exp = u_tab.shape[0]
    tm = 512 if t % 512 == 0 else _row_tile(t, 512)
    rows = PEER_EXPERT_ROWS
    eb = rows * nk
    keys2 = keys.reshape(heads * 2, nk, dh)
    cidx = _candidate_index_rows(tm)
    n_pairs = n_exp // (2 * eb)
    assert n_pairs == heads, "one pair of expert blocks per selection step"
    return pl.pallas_call(
        functools.partial(_peer_kernel, alpha=alpha, heads=heads, nk=nk, rows_per_block=rows),
        grid=(t // tm, heads + n_pairs),
        in_specs=[
            pl.BlockSpec((tm, d), lambda i, j: (i, 0)),
            pl.BlockSpec(wq_t.shape, lambda i, j: (0, 0), pipeline_mode=pl.Buffered(1)),
            pl.BlockSpec(keys2.shape, lambda i, j: (0, 0, 0)),
            pl.BlockSpec(cidx.shape, lambda i, j: (0, 0)),
            pl.BlockSpec((2 * eb, d), lambda i, j: (jnp.minimum(j, heads - 1), 0)),
            pl.BlockSpec((d, 2 * eb), lambda i, j: (0, jnp.maximum(j - heads, 0))),
            pl.BlockSpec((1, d), lambda i, j: (0, 0)),
            pl.BlockSpec((1, d), lambda i, j: (0, 0)),
        ],
        out_specs=pl.BlockSpec((tm, d), lambda i, j: (i, 0), pipeline_mode=pl.Buffered(1)),
        out_shape=jax.ShapeDtypeStruct((t, d), F32),
        scratch_shapes=[
            pltpu.VMEM((tm, d), BF16),
            pltpu.VMEM((wq_t.shape[0], tm), BF16),
            pltpu.VMEM((n_exp, tm), BF16),
            pltpu.VMEM((heads, nk, tm), F32), pltpu.VMEM((heads, nk, tm), F32),
            pltpu.VMEM((heads, nk, tm), BF16), pltpu.VMEM((heads, nk, tm), BF16),
            pltpu.VMEM((eb, tm), BF16), pltpu.VMEM((eb, tm), BF16),
            pltpu.VMEM((d, tm), F32),
        ],
        compiler_params=_cparams("parallel", "arbitrary"),
        name="peer",
    )(x, wq_t, keys2, cidx, u_tab, v_tab_t, g.reshape(1, d), b.reshape(1, d))


def _pad_tokens(a, length):
    return jnp.pad(a, ((0, 0), (0, length - a.shape[1]), (0, 0)))


def _sequence_mixers(proj3, lb, s_hg, s_re, s_im, s5_prm, d_hg):
    bsz, length, _ = proj3.shape
    lpad = -(-length // SUBLANES) * SUBLANES
    o, s_hg_new = _hgrn(_pad_tokens(proj3, lpad) if lpad != length else proj3, lb, s_hg, valid=length)
    lc = S5_CHUNK if length % S5_CHUNK == 0 else length
    y, s_re_new, s_im_new = _s5(proj3.reshape(bsz * length, -1), s_re, s_im, s5_prm[lc],
                                length=length, lc=lc, u_col=4 * d_hg)
    return o[:, :length], y.reshape(bsz, length, -1), s_hg_new, s_re_new, s_im_new


def kernel(x_prompt, x_sample, state_hgrn, state_ssm_re, state_ssm_im, meta_tokens, ln_emb_g, ln_emb_b,
           lb_logits, w_in, b_in, hg_norm_g, ssm_a_re, ssm_a_im, ssm_log_dt, ssm_b_re, ssm_b_im,
           ssm_c_re, ssm_c_im, ssm_d, w_glu, b_glu, w_out, ln1_g, ln1_b, peer_w_q, peer_keys,
           peer_u, peer_v, ln2_g, ln2_b):
    depth = w_in.shape[0]
    alpha = (2.0 * depth) ** 0.25
    bp, seq, d = x_prompt.shape
    bs, dseq, _ = x_sample.shape
    n_meta = meta_tokens.shape[0]
    heads = state_hgrn.shape[2]
    d_hg = heads * LANES
    groups, pstate = state_ssm_re.shape[2], state_ssm_re.shape[3]
    lbs = jnp.cumsum(jax.nn.softmax(lb_logits.astype(F32), axis=0), axis=0)

    xp = x_prompt.astype(F32).reshape(bp * seq, d)
    xm = meta_tokens.astype(F32)
    xs = x_sample.astype(F32).reshape(bs * dseq, d)
    n_s = bs * dseq
    hg_p, re_p, im_p, hg_s, re_s, im_s = [], [], [], [], [], []
    for l in range(depth):
        last = l == depth - 1
        w_in_b = w_in[l].astype(BF16)
        lb = lbs[l].reshape(1, d_hg)
        s5_prm = {lc: _s5_params(ssm_a_re[l], ssm_a_im[l], ssm_log_dt[l], ssm_b_re[l], ssm_b_im[l],
                                 ssm_c_re[l], ssm_c_im[l], ssm_d[l], lc)
                  for lc in {S5_CHUNK if n % S5_CHUNK == 0 else n for n in (n_meta, seq, dseq)}}
        wglu = w_glu[l].astype(BF16)
        wo_hg = w_out[l, :d_hg].astype(BF16)
        wo_ssm = w_out[l, d_hg:].astype(BF16)
        wq_t = peer_w_q[l].T.astype(BF16)
        keys = peer_keys[l].astype(BF16)
        u_tab = peer_u[l].astype(BF16)
        v_tab_t = peer_v[l].T.astype(BF16)

        xsm = jnp.concatenate([xs, xm], axis=0)
        x0_p, proj_p = _inproj(xp, ln_emb_g, ln_emb_b, w_in_b, b_in[l], apply_ln=(l == 0))
        x0_sm, proj_sm = _inproj(xsm, ln_emb_g, ln_emb_b, w_in_b, b_in[l], apply_ln=(l == 0))
        d_in = proj_p.shape[1]

        proj_m = proj_sm[n_s:].reshape(1, n_meta, d_in)
        proj_m8 = jnp.broadcast_to(proj_m, (SUBLANES, n_meta, d_in))
        zero_hg = jnp.zeros((SUBLANES, heads, LANES, LANES), F32)
        zero_ss = jnp.zeros((SUBLANES, groups, pstate), F32)
        o_m, y_m, hg_m, re_m, im_m = _sequence_mixers(proj_m8, lb, zero_hg, zero_ss, zero_ss, s5_prm, d_hg)

        o_p, y_p, shg, sre, sim = _sequence_mixers(
            proj_p.reshape(bp, seq, d_in), lb,
            jnp.broadcast_to(hg_m[:1], (bp,) + hg_m.shape[1:]),
            jnp.broadcast_to(re_m[:1], (bp,) + re_m.shape[1:]),
            jnp.broadcast_to(im_m[:1], (bp,) + im_m.shape[1:]), s5_prm, d_hg)
        hg_p.append(shg)
        re_p.append(sre)
        im_p.append(sim)

        o_s, y_s, shg, sre, sim = _sequence_mixers(
            proj_sm[:n_s].reshape(bs, dseq, d_in), lb, state_hgrn[l].astype(F32),
            state_ssm_re[l].astype(F32), state_ssm_im[l].astype(F32), s5_prm, d_hg)
        hg_s.append(shg)
        re_s.append(sre)
        im_s.append(sim)

        mix_args = (hg_norm_g[l], wglu, b_glu[l], wo_hg, wo_ssm, ln1_g[l], ln1_b[l])
        peer_args = (wq_t, keys, u_tab, v_tab_t, ln2_g[l], ln2_b[l])
        x1_p = _mix(x0_p, o_p.reshape(bp * seq, d_hg), proj_p, y_p.reshape(bp * seq, -1), *mix_args, alpha=alpha)
        xp = _peer(x1_p, *peer_args, alpha=alpha)
        x1_s = _mix(x0_sm[:n_s], o_s.reshape(n_s, d_hg), proj_sm[:n_s], y_s.reshape(n_s, -1), *mix_args, alpha=alpha)
        xs = _peer(x1_s, *peer_args, alpha=alpha)
        if not last:
            x1_m = _mix(jnp.broadcast_to(x0_sm[n_s:], (n_meta, d)), o_m[0], proj_sm[n_s:], y_m[0], *mix_args, alpha=alpha)
            xm = _peer(x1_m, *peer_args, alpha=alpha)

    y_prompt = xp.reshape(bp, seq, d).astype(x_prompt.dtype)
    y_sample = xs.reshape(bs, dseq, d).astype(x_sample.dtype)
    return (y_prompt, y_sample, jnp.stack(hg_p), jnp.stack(re_p), jnp.stack(im_p),
            jnp.stack(hg_s), jnp.stack(re_s), jnp.stack(im_s))
```

```python
import functools
import math

import jax
import jax.numpy as jnp
from jax import lax
from jax.experimental import pallas as pl
from jax.experimental.pallas import tpu as pltpu

F32 = jnp.float32
BF16 = jnp.bfloat16

LN_EPS = 1e-5
RMS_EPS = 1e-6
HG_CHUNK = 16
HG_CHUNKS_PER_ITER = 16
HG_SEQS_PER_STEP = 16
SSM_GROUP = 16
S5_CHUNK = 16
S5_GROUP_TILE = 8
S5_SEQS_PER_STEP = 4
PEER_TOPK = 16
LANES = 128
SUBLANES = 8
VMEM_LIMIT = 62 * 1024 * 1024
PEER_EXPERT_ROWS = 8


def _cparams(*sem):
    return pltpu.CompilerParams(dimension_semantics=sem, vmem_limit_bytes=VMEM_LIMIT)


def _layer_norm(x, g, b):
    mu = jnp.mean(x, axis=-1, keepdims=True)
    xc = x - mu
    var = jnp.mean(xc * xc, axis=-1, keepdims=True)
    return xc * lax.rsqrt(var + LN_EPS) * g + b


def _sigmoid(x):
    return 1.0 / (1.0 + jnp.exp(-x))


def _gelu(x):
    c1 = -2.0 * math.sqrt(2.0 / math.pi)
    return x / (1.0 + jnp.exp(x * (c1 + (c1 * 0.044715) * (x * x))))


def _row_tile(t, target):
    best = None
    for cand in range(SUBLANES, min(t, target) + 1, SUBLANES):
        if t % cand == 0:
            best = cand
    return best if best is not None else t


def _inproj_kernel(x_ref, g_ref, b_ref, w_ref, bi_ref, x0_ref, proj_ref, *, apply_ln):
    x = x_ref[...]
    if apply_ln:
        x = _layer_norm(x, g_ref[...], b_ref[...])
    x0_ref[...] = x
    proj_ref[...] = jnp.dot(x.astype(BF16), w_ref[...], preferred_element_type=F32) + bi_ref[...]


def _inproj(x, g, b, w_bf16, bias, *, apply_ln):
    t, d = x.shape
    d_in = w_bf16.shape[1]
    tm = _row_tile(t, 512)
    return pl.pallas_call(
        functools.partial(_inproj_kernel, apply_ln=apply_ln),
        grid=(t // tm,),
        in_specs=[
            pl.BlockSpec((tm, d), lambda i: (i, 0)),
            pl.BlockSpec((1, d), lambda i: (0, 0)),
            pl.BlockSpec((1, d), lambda i: (0, 0)),
            pl.BlockSpec((d, d_in), lambda i: (0, 0)),
            pl.BlockSpec((1, d_in), lambda i: (0, 0)),
        ],
        out_specs=[
            pl.BlockSpec((tm, d), lambda i: (i, 0)),
            pl.BlockSpec((tm, d_in), lambda i: (i, 0)),
        ],
        out_shape=[jax.ShapeDtypeStruct((t, d), F32), jax.ShapeDtypeStruct((t, d_in), F32)],
        compiler_params=_cparams("parallel"),
        name="inproj",
    )(x, g.reshape(1, d), b.reshape(1, d), w_bf16, bias.reshape(1, d_in))


def _hgrn_kernel(q_ref, f_ref, v_ref, lb_ref, s0_ref, o_ref, s_ref, st_ref,
                 *, chunk, n_iters, per_iter, seqs, valid):
    lb = lb_ref[...]
    ones = jnp.ones((LANES, LANES), BF16)
    groups = seqs * per_iter
    shape = (groups, chunk, LANES)
    row = lax.broadcasted_iota(jnp.int32, shape, 1)
    span = per_iter * chunk
    nt = (((1,), (1,)), ((), ()))
    tn = (((0,), (0,)), ((), ()))

    for b in range(seqs):
        st_ref[b] = s0_ref[b].T

    def take(ref, r0):
        return jnp.concatenate([ref[b, pl.ds(r0, span), :].reshape(per_iter, chunk, LANES)
                                for b in range(seqs)], axis=0)

    def step(i, carry):
        r0 = pl.multiple_of(i * span, span)
        q = take(q_ref, r0)
        fp = take(f_ref, r0)
        v = take(v_ref, r0)
        f = lb + (1.0 - lb) * _sigmoid(fp)
        logf = jnp.log(f)
        kk = (1.0 - lb) * _sigmoid(-fp)
        if valid < chunk:
            live = row < valid
            logf = jnp.where(live, logf, 0.0)
            kk = jnp.where(live, kk, 0.0)
            q = jnp.where(live, q, 0.0)
            v = jnp.where(live, v, 0.0)
        pick = lambda a, s: jnp.broadcast_to(a[:, s:s + 1, :], shape)
        bc = jnp.zeros(shape, F32)
        for s in range(chunk):
            bc = bc + jnp.where(row >= s, pick(logf, s), 0.0)
        split = chunk // 2 if chunk % (2 * SUBLANES) == 0 else 0
        lows = [split if s >= split else 0 for s in range(chunk)]
        prods = []
        for s, lo in zip(range(chunk), lows):
            part = (groups, chunk - lo, LANES)
            near = lambda a: jnp.broadcast_to(a[:, s:s + 1, :], part)
            e = jnp.where(row[:, lo:, :] >= s, jnp.exp(bc[:, lo:, :] - near(bc)), 0.0)
            prods.append((q[:, lo:, :] * e * near(kk)).reshape(groups * (chunk - lo), LANES))
        p_all = jnp.concatenate(prods, axis=0).astype(BF16)
        att = jnp.dot(p_all, ones, preferred_element_type=F32)
        o = jnp.zeros(shape, F32)
        o_low = jnp.zeros((groups, chunk - split, LANES), F32)
        off = 0
        for s, lo in zip(range(chunk), lows):
            part = (groups, chunk - lo, LANES)
            n = groups * (chunk - lo)
            term = att[off:off + n, :].reshape(part) * jnp.broadcast_to(v[:, s:s + 1, :], part)
            off += n
            if lo == 0:
                o = o + term
            else:
                o_low = o_low + term
        if split:
            o = o + jnp.concatenate([jnp.zeros((groups, split, LANES), F32), o_low], axis=1)
        qs = (q * jnp.exp(bc)).astype(BF16)
        kt = (kk * jnp.exp(pick(bc, chunk - 1) - bc)).astype(BF16)
        vb = v.astype(BF16)
        decay = jnp.exp(bc[:, chunk - 1:chunk, :])
        ds = [lax.dot_general(vb[g], kt[g], tn, preferred_element_type=F32) for g in range(groups)]
        outs = []
        for b in range(seqs):
            st = st_ref[b]
            before = []
            for n in range(per_iter):
                g = b * per_iter + n
                before.append(st.astype(BF16))
                st = st * decay[g] + ds[g]
            st_ref[b] = st
            for n in range(per_iter):
                g = b * per_iter + n
                outs.append(o[g] + lax.dot_general(qs[g], before[n], nt, preferred_element_type=F32))
        for b in range(seqs):
            ob = jnp.concatenate(outs[b * per_iter:(b + 1) * per_iter], axis=0)
            ob = ob * lax.rsqrt(jnp.mean(ob * ob, axis=-1, keepdims=True) + RMS_EPS)
            o_ref[b, pl.ds(r0, span), :] = ob
        return carry

    lax.fori_loop(0, n_iters, step, 0)
    for b in range(seqs):
        s_ref[b] = st_ref[b].T


def _hgrn(proj3, lb, s0, *, valid):
    bsz, length, _ = proj3.shape
    heads = s0.shape[1]
    chunk = HG_CHUNK if length % HG_CHUNK == 0 else length
    n_chunks = length // chunk
    divisor = lambda n, cap: max(c for c in range(1, cap + 1) if n % c == 0)
    if n_chunks > 1:
        seqs, per_iter = 1, divisor(n_chunks, HG_CHUNKS_PER_ITER)
    else:
        seqs, per_iter = divisor(bsz, HG_SEQS_PER_STEP), 1
    seq_spec = lambda off: pl.BlockSpec((seqs, length, LANES), lambda b, h: (b, 0, off + h))
    state_spec = pl.BlockSpec((seqs, None, LANES, LANES), lambda b, h: (b, h, 0, 0))
    return pl.pallas_call(
        functools.partial(_hgrn_kernel, chunk=chunk, n_iters=n_chunks // per_iter, per_iter=per_iter,
                          seqs=seqs, valid=min(valid, chunk)),
        grid=(bsz // seqs, heads),
        in_specs=[
            seq_spec(0), seq_spec(heads), seq_spec(2 * heads),
            pl.BlockSpec((1, LANES), lambda b, h: (0, h)),
            state_spec,
        ],
        out_specs=[
            pl.BlockSpec((seqs, length, LANES), lambda b, h: (b, 0, h)),
            state_spec,
        ],
        out_shape=[jax.ShapeDtypeStruct((bsz, length, heads * LANES), F32),
                   jax.ShapeDtypeStruct(s0.shape, F32)],
        scratch_shapes=[pltpu.VMEM((seqs, LANES, LANES), F32)],
        compiler_params=_cparams("parallel", "parallel"),
        name="hgrn2",
    )(proj3, proj3, proj3, lb, s0)


def _s5_kernel(u_ref, bb_ref, c_ref, are_ref, aim_ref, alre_ref, alim_ref, d_ref, h0re_ref, h0im_ref,
               y_ref, hre_ref, him_ref, hpre_ref, hpim_ref, *, lc, n_chunks, seqs):
    rows = seqs * n_chunks
    half = are_ref.shape[-1]
    a_re = are_ref[...]
    a_im = aim_ref[...]
    bb = bb_ref[...]

    def drive(t):
        u_t = u_ref[pl.ds(t, rows, stride=lc), :]
        return u_t, jnp.dot(u_t.astype(BF16), bb, preferred_element_type=F32)

    def advance(h_re, h_im, bu):
        return (a_re * h_re - a_im * h_im + bu[:, :half], a_re * h_im + a_im * h_re + bu[:, half:])

    h_re = jnp.zeros((rows, half), F32)
    h_im = jnp.zeros((rows, half), F32)
    for t in range(lc):
        h_re, h_im = advance(h_re, h_im, drive(t)[1])

    n_tiles = half // LANES

    def put(ref, idx, val):
        for k in range(n_tiles):
            ref[k, idx, :] = val[:, k * LANES:(k + 1) * LANES]

    def get(ref, idx):
        return jnp.concatenate([ref[k, idx, :] for k in range(n_tiles)], axis=-1)

    put(hpre_ref, slice(None), h_re)
    put(hpim_ref, slice(None), h_im)

    al_re = alre_ref[...]
    al_im = alim_ref[...]

    def carry_step(c, carry):
        c_re, c_im = carry
        idx = pl.ds(c, seqs, stride=n_chunks)
        l_re = get(hpre_ref, idx)
        l_im = get(hpim_ref, idx)
        put(hpre_ref, idx, c_re)
        put(hpim_ref, idx, c_im)
        return (al_re * c_re - al_im * c_im + l_re, al_re * c_im + al_im * c_re + l_im)

    c_re, c_im = lax.fori_loop(0, n_chunks, carry_step, (h0re_ref[...], h0im_ref[...]))
    hre_ref[...] = c_re
    him_ref[...] = c_im

    h_re = get(hpre_ref, slice(None))
    h_im = get(hpim_ref, slice(None))
    c_w = c_ref[...]
    d_vec = d_ref[...]
    for t in range(lc):
        u_t, bu = drive(t)
        h_re, h_im = advance(h_re, h_im, bu)
        y = (jnp.dot(h_re.astype(BF16), c_w[:half], preferred_element_type=F32)
             + jnp.dot(h_im.astype(BF16), c_w[half:], preferred_element_type=F32) + d_vec * u_t)
        y_ref[pl.ds(t, rows, stride=lc), :] = y


def _s5_params(a_re, a_im, log_dt, b_re, b_im, c_re, c_im, d, lc):
    groups, pstate = a_re.shape
    dt = jnp.exp(log_dt.astype(F32))[:, None]
    a_re = a_re.astype(F32)
    a_im = a_im.astype(F32)
    zr, zi = a_re * dt, a_im * dt
    mag = jnp.exp(zr)
    ab_re, ab_im = mag * jnp.cos(zi), mag * jnp.sin(zi)
    den = a_re * a_re + a_im * a_im
    nr = ab_re - 1.0
    coef_re = (nr * a_re + ab_im * a_im) / den
    coef_im = (ab_im * a_re - nr * a_im) / den
    b_re = b_re.astype(F32)
    b_im = b_im.astype(F32)
    bb_re = coef_re[..., None] * b_re - coef_im[..., None] * b_im
    bb_im = coef_re[..., None] * b_im + coef_im[..., None] * b_re
    pm = jnp.exp(zr * lc)
    al_re, al_im = pm * jnp.cos(zi * lc), pm * jnp.sin(zi * lc)
    gt = S5_GROUP_TILE
    tiles = groups // gt
    swap = lambda x: jnp.transpose(x, (0, 2, 1))
    bb = jnp.concatenate([_block_diag(swap(bb_re), gt), _block_diag(swap(bb_im), gt)], axis=-1)
    cw = jnp.concatenate([_block_diag(swap(c_re.astype(F32)), gt), -_block_diag(swap(c_im.astype(F32)), gt)], axis=1)
    lane = lambda x: x.reshape(tiles, 1, gt * pstate)
    return dict(bb=bb.astype(BF16), cw=cw.astype(BF16), are=lane(ab_re), aim=lane(ab_im),
                alre=lane(al_re), alim=lane(al_im), d=d.astype(F32).reshape(1, groups * SSM_GROUP))


def _block_diag(x, gm):
    if gm == 1:
        return x
    g, a, b = x.shape
    x = x.reshape(g // gm, gm, a, b)
    eye = jnp.eye(gm, dtype=x.dtype)
    return (x[:, :, :, None, :] * eye[None, :, None, :, None]).reshape(g // gm, gm * a, gm * b)


def _s5(proj, h0_re, h0_im, prm, *, length, lc, u_col):
    bsz, n_groups, pstate = h0_re.shape
    d_ssm = n_groups * SSM_GROUP
    tiles = n_groups // S5_GROUP_TILE
    half = S5_GROUP_TILE * pstate
    n_chunks = length // lc
    seqs = bsz if n_chunks == 1 else min(bsz, S5_SEQS_PER_STEP)
    steps = bsz // seqs
    rows = seqs * n_chunks
    h0r = h0_re.reshape(steps, seqs, n_groups * pstate)
    h0i = h0_im.reshape(steps, seqs, n_groups * pstate)
    tspec = lambda a, b: pl.BlockSpec((None, a, b), lambda i, g: (g, 0, 0))
    sspec = pl.BlockSpec((None, seqs, half), lambda i, g: (i, 0, g))
    y, hre, him = pl.pallas_call(
        functools.partial(_s5_kernel, lc=lc, n_chunks=n_chunks, seqs=seqs),
        grid=(steps, tiles),
        in_specs=[
            pl.BlockSpec((seqs * length, LANES), lambda i, g: (i, u_col // LANES + g)),
            tspec(LANES, 2 * half), tspec(2 * half, LANES),
            tspec(1, half), tspec(1, half), tspec(1, half), tspec(1, half),
            pl.BlockSpec((1, LANES), lambda i, g: (0, g)),
            sspec, sspec,
        ],
        out_specs=[
            pl.BlockSpec((seqs * length, LANES), lambda i, g: (i, g)),
            sspec, sspec,
        ],
        out_shape=[jax.ShapeDtypeStruct((bsz * length, d_ssm), F32),
                   jax.ShapeDtypeStruct(h0r.shape, F32),
                   jax.ShapeDtypeStruct(h0r.shape, F32)],
        scratch_shapes=[pltpu.VMEM((half // LANES, rows, LANES), F32),
                        pltpu.VMEM((half // LANES, rows, LANES), F32)],
        compiler_params=_cparams("parallel", "parallel"),
        name="s5",
    )(proj, prm['bb'], prm['cw'], prm['are'], prm['aim'], prm['alre'], prm['alim'], prm['d'], h0r, h0i)
    return y, hre.reshape(h0_re.shape), him.reshape(h0_re.shape)


def _mix_kernel(x_ref, o_ref, gate_ref, y_ref, hg_g_ref, wglu_ref, bglu_ref, wo_hg_ref, wo_ssm_ref,
                g_ref, b_ref, out_ref, *, alpha):
    gate = gate_ref[...]
    o_hg = o_ref[...] * hg_g_ref[...] * (gate * _sigmoid(gate))
    z = _gelu(y_ref[...])
    glu = jnp.dot(z.astype(BF16), wglu_ref[...], preferred_element_type=F32) + bglu_ref[...]
    o_ssm = z * _sigmoid(glu)
    mix = (jnp.dot(o_hg.astype(BF16), wo_hg_ref[...], preferred_element_type=F32)
           + jnp.dot(o_ssm.astype(BF16), wo_ssm_ref[...], preferred_element_type=F32))
    out_ref[...] = _layer_norm(alpha * x_ref[...] + mix, g_ref[...], b_ref[...])


def _mix(x0, o_hg, proj, y_ssm, hg_g, wglu, bglu, wo_hg, wo_ssm, g, b, *, alpha):
    t, d = x0.shape
    d_hg = o_hg.shape[1]
    d_ssm = y_ssm.shape[1]
    tm = _row_tile(t, 512)
    gate_block = (3 * d_hg) // d_hg
    row = lambda w: pl.BlockSpec((tm, w), lambda i: (i, 0))
    full = lambda a, bb: pl.BlockSpec((a, bb), lambda i: (0, 0))
    return pl.pallas_call(
        functools.partial(_mix_kernel, alpha=alpha),
        grid=(t // tm,),
        in_specs=[
            row(d), row(d_hg),
            pl.BlockSpec((tm, d_hg), lambda i: (i, gate_block)),
            row(d_ssm),
            full(1, d_hg), full(d_ssm, d_ssm), full(1, d_ssm), full(d_hg, d), full(d_ssm, d),
            full(1, d), full(1, d),
        ],
        out_specs=row(d),
        out_shape=jax.ShapeDtypeStruct((t, d), F32),
        compiler_params=_cparams("parallel"),
        name="mix",
    )(x0, o_hg, proj, y_ssm, hg_g.reshape(1, d_hg), wglu, bglu.reshape(1, d_ssm), wo_hg, wo_ssm,
      g.reshape(1, d), b.reshape(1, d))


def _top_rows(s, k):
    n, t = s.shape
    iota = lax.broadcasted_iota(jnp.int32, (n, t), 0).astype(F32)
    krow = lax.broadcasted_iota(jnp.int32, (k, t), 0)
    rank = jnp.full((n, t), float(k), F32)
    vals = jnp.zeros((k, t), F32)
    for a in range(k):
        m = jnp.max(s, axis=0, keepdims=True)
        idx = jnp.min(jnp.where(s == m, iota, float(n)), axis=0, keepdims=True)
        hit = iota == idx
        rank = jnp.where(hit, float(a), rank)
        s = jnp.where(hit, -jnp.inf, s)
        vals = jnp.where(krow == a, jnp.broadcast_to(m, (k, t)), vals)
    return vals, rank


def _candidate_pieces(t1, t2):
    k = PEER_TOPK
    t = t1.shape[1]
    bc = lambda r, n: jnp.broadcast_to(r, (n, t))
    pieces = [(bc(t1[0:1], k), t2, [b for b in range(k)], [True] * k)]
    half = k // 2
    for a in range(1, half):
        nb = k // (a + 1)
        pieces.append((bc(t1[a:a + 1], half), t2[0:half], [a * k + b for b in range(half)],
                       [b < nb for b in range(half)]))
    pieces.append((t1[half:k], bc(t2[0:1], half), [(half + r) * k for r in range(half)], [True] * half))
    return pieces


def _candidate_index_rows(t):
    k = PEER_TOPK
    dummy = jnp.zeros((k, 1), F32)
    vals = [i if ok else k * k for _, _, idx, val in _candidate_pieces(dummy, dummy) for i, ok in zip(idx, val)]
    return jnp.broadcast_to(jnp.asarray(vals, F32)[:, None], (len(vals), t))


def _peer_select(s1, s2, cidx):
    k = PEER_TOPK
    t = s1.shape[1]
    big = float(k * k)
    top1, rank1 = _top_rows(s1, k)
    top2, rank2 = _top_rows(s2, k)
    pieces = _candidate_pieces(top1, top2)
    cand = jnp.concatenate([x + y for x, y, _, _ in pieces], axis=0)
    cand = jnp.where(cidx < big, cand, -jnp.inf)
    chosen = jnp.zeros(cand.shape, F32)
    for _ in range(k):
        m = jnp.max(cand, axis=0, keepdims=True)
        idx = jnp.min(jnp.where(cand == m, cidx, big), axis=0, keepdims=True)
        hit = cidx == idx
        chosen = jnp.where(hit, 1.0, chosen)
        cand = jnp.where(hit, -jnp.inf, cand)
    e1 = jnp.exp(top1 - jnp.broadcast_to(top1[0:1], (k, t)))
    e2 = jnp.exp(top2 - jnp.broadcast_to(top2[0:1], (k, t)))
    ecand = jnp.concatenate([x * y for x, y, _, _ in _candidate_pieces(e1, e2)], axis=0)
    z = jnp.sum(chosen * ecand, axis=0, keepdims=True)
    counts = []
    off = 0
    half = k // 2
    for pi, (x, _, _, _) in enumerate(pieces):
        rows = x.shape[0]
        blk = chosen[off:off + rows]
        off += rows
        if pi < len(pieces) - 1:
            counts.append(jnp.sum(blk, axis=0, keepdims=True))
        else:
            counts.extend(blk[r:r + 1] for r in range(half))
    nk = s1.shape[0]
    c1 = jnp.zeros((nk, t), F32)
    for a in range(k):
        c1 = jnp.where(rank1 == float(a), jnp.broadcast_to(counts[a], (nk, t)), c1)
    phi = jnp.exp(s1 - jnp.broadcast_to(top1[0:1], (nk, t))) * jnp.broadcast_to(1.0 / z, (nk, t))
    psi = jnp.exp(s2 - jnp.broadcast_to(top2[0:1], (nk, t)))
    return c1, phi, rank2, psi


def _sort_network(n):
    out, p = [], 1
    while p < n:
        k = p
        while k >= 1:
            for j in range(k % p, n - k, 2 * k):
                for i in range(min(k, n - j - k)):
                    if (i + j) // (2 * p) == (i + j + k) // (2 * p):
                        out.append((i + j, i + j + k))
            k //= 2
        p *= 2
    return out


def _sorted_top(s):
    n = s.shape[0] // SUBLANES
    x = [s[i * SUBLANES:(i + 1) * SUBLANES, :] for i in range(n)]
    for i, j in _sort_network(n):
        x[i], x[j] = jnp.maximum(x[i], x[j]), jnp.minimum(x[i], x[j])
    shift = SUBLANES // 2
    while shift >= 1:
        r = [pltpu.roll(v, shift, axis=0) for v in x]
        x = [jnp.maximum(x[i], r[n - 1 - i]) for i in range(n)]
        d = n // 2
        while d >= 1:
            for i in range(n):
                if (i & d) == 0:
                    x[i], x[i + d] = jnp.maximum(x[i], x[i + d]), jnp.minimum(x[i], x[i + d])
            d //= 2
        shift //= 2
    return x


def _peer_select_fast(s1, s2, cidx):
    k = PEER_TOPK
    nk, t = s1.shape
    nblk = nk // SUBLANES
    big = float(k * k)
    l1 = _sorted_top(s1)
    l2 = _sorted_top(s2)
    sub = lax.broadcasted_iota(jnp.int32, (SUBLANES, t), 0)

    def stack(lst):
        halves = []
        for base in (0, SUBLANES):
            blk = lst[base]
            for a in range(1, SUBLANES):
                blk = jnp.where(sub == a, lst[base + a], blk)
            halves.append(blk)
        return jnp.concatenate(halves, axis=0)

    top1 = stack(l1)
    top2 = stack(l2)
    pieces = _candidate_pieces(top1, top2)
    cand = jnp.concatenate([x + y for x, y, _, _ in pieces], axis=0)
    cand = jnp.where(cidx < big, cand, -jnp.inf)
    chosen = jnp.zeros(cand.shape, F32)
    for _ in range(k):
        hit = cand == jnp.max(cand, axis=0, keepdims=True)
        chosen = jnp.where(hit, 1.0, chosen)
        cand = jnp.where(hit, -jnp.inf, cand)
    e1 = jnp.exp(top1 - jnp.broadcast_to(top1[0:1], (k, t)))
    e2 = jnp.exp(top2 - jnp.broadcast_to(top2[0:1], (k, t)))
    ecand = jnp.concatenate([x * y for x, y, _, _ in _candidate_pieces(e1, e2)], axis=0)
    z = jnp.sum(chosen * ecand, axis=0, keepdims=True)
    counts = []
    off = 0
    half = k // 2
    for pi, (x, _, _, _) in enumerate(pieces):
        rows = x.shape[0]
        blk = chosen[off:off + rows]
        off += rows
        if pi < len(pieces) - 1:
            counts.append(jnp.sum(blk, axis=0, keepdims=True))
        else:
            counts.extend(blk[r:r + 1] for r in range(half))
    c1_blocks, r2_blocks = [], []
    cnt1 = jnp.zeros((SUBLANES, t), F32)
    cnt2 = jnp.zeros((SUBLANES, t), F32)
    for i in range(nblk):
        x1 = s1[i * SUBLANES:(i + 1) * SUBLANES, :]
        x2 = s2[i * SUBLANES:(i + 1) * SUBLANES, :]
        c1b = jnp.zeros((SUBLANES, t), F32)
        r2b = jnp.zeros((SUBLANES, t), F32)
        for a in range(k):
            c1b = jnp.where(x1 == l1[a], jnp.broadcast_to(counts[a], (SUBLANES, t)), c1b)
            r2b = jnp.where(x2 < l2[a], float(a + 1), r2b)
        c1_blocks.append(c1b)
        r2_blocks.append(r2b)
        cnt1 = cnt1 + jnp.where(x1 >= l1[k - 1], 1.0, 0.0)
        cnt2 = cnt2 + jnp.where(x2 >= l2[k - 1], 1.0, 0.0)
    c1 = jnp.concatenate(c1_blocks, axis=0)
    r2 = jnp.concatenate(r2_blocks, axis=0)
    phi = jnp.exp(s1 - jnp.broadcast_to(top1[0:1], (nk, t))) * jnp.broadcast_to(1.0 / z, (nk, t))
    psi = jnp.exp(s2 - jnp.broadcast_to(top2[0:1], (nk, t)))
    bad = jnp.zeros((SUBLANES, t), F32)
    for a in range(k - 1):
        bad = jnp.where(l1[a] <= l1[a + 1], 1.0, bad)
        bad = jnp.where(l2[a] <= l2[a + 1], 1.0, bad)
    fk = float(k)
    bad = jnp.max(bad, axis=0, keepdims=True)
    bad = jnp.where(jnp.sum(cnt1, axis=0, keepdims=True) != fk, 1.0, bad)
    bad = jnp.where(jnp.sum(cnt2, axis=0, keepdims=True) != fk, 1.0, bad)
    bad = jnp.where(jnp.sum(chosen, axis=0, keepdims=True) != fk, 1.0, bad)
    return c1, phi, r2, psi, bad


def _rows_bf16(row, n):
    pack = 2 * SUBLANES
    tile = jnp.broadcast_to(row, (pack, row.shape[1])).astype(BF16)
    return jnp.concatenate([tile] * (n // pack), axis=0)


def _peer_kernel(x_ref, wq_ref, keys_ref, cidx_ref, u_ref, vt_ref, g_ref, b_ref,
                 out_ref, xb_ref, qt_ref, act_ref, c1_ref, phi_ref, r2_ref, psi_ref, wa_ref, wb_ref, acc_ref,
                 *, alpha, heads, nk, rows_per_block):
    j = pl.program_id(1)
    nt = (((1,), (1,)), ((), ()))
    eb = rows_per_block * nk

    @pl.when(j == 0)
    def _():
        xb = x_ref[...].astype(BF16)
        xb_ref[...] = xb
        acc_ref[...] = jnp.zeros_like(acc_ref)
        qt_ref[...] = lax.dot_general(wq_ref[...], xb, nt, preferred_element_type=F32).astype(BF16)

    @pl.when(j < heads)
    def _():
        h = j
        sc = []
        for c in range(2):
            hc = h * 2 + c
            qt = qt_ref[pl.ds(pl.multiple_of(hc * LANES, LANES), LANES), :]
            sc.append(jnp.dot(keys_ref[hc], qt, preferred_element_type=F32))
        start = pl.multiple_of(h * (2 * eb), 2 * eb)
        pre = lax.dot_general(u_ref[...], xb_ref[...], nt, preferred_element_type=F32)
        act_ref[pl.ds(start, 2 * eb), :] = _gelu(pre).astype(BF16)

        def put(c1, phi, r2, psi):
            c1_ref[h] = c1
            phi_ref[h] = phi
            r2_ref[h] = r2.astype(BF16)
            psi_ref[h] = psi.astype(BF16)

        if nk // SUBLANES == PEER_TOPK:
            *sel, bad = _peer_select_fast(sc[0], sc[1], cidx_ref[...])
            put(*sel)

            @pl.when(jnp.max(bad) > 0.0)
            def _():
                put(*_peer_select(sc[0], sc[1], cidx_ref[...]))
        else:
            put(*_peer_select(sc[0], sc[1], cidx_ref[...]))

    def mix_block(block, w_ref):
        for r in range(rows_per_block):
            n1 = block * rows_per_block + r
            gsum = None
            for h in range(heads):
                c1row = _rows_bf16(c1_ref[h, pl.ds(n1, 1), :], nk)
                phirow = _rows_bf16(phi_ref[h, pl.ds(n1, 1), :], nk)
                term = jnp.where(r2_ref[h] < c1row, phirow * psi_ref[h], jnp.zeros((), BF16))
                gsum = term if gsum is None else gsum + term
            w_ref[r * nk:(r + 1) * nk, :] = gsum * act_ref[pl.ds(pl.multiple_of(n1 * nk, nk), nk), :]

    @pl.when(j >= heads)
    def _():
        m = j - heads
        mix_block(2 * m, wa_ref)
        acc_ref[...] += jnp.dot(vt_ref[:, :eb], wa_ref[...], preferred_element_type=F32)
        mix_block(2 * m + 1, wb_ref)
        acc_ref[...] += jnp.dot(vt_ref[:, eb:], wb_ref[...], preferred_element_type=F32)

    @pl.when(j == pl.num_programs(1) - 1)
    def _():
        ffn = acc_ref[...].T
        out_ref[...] = _layer_norm(alpha * x_ref[...] + ffn, g_ref[...], b_ref[...])


def _peer(x, wq_t, keys, u_tab, v_tab_t, g, b, *, alpha):
    t, d = x.shape
    heads, _, nk, dh = keys.shape
    n_exp = u_tab.shape[0]
    tm = 512 if t % 512 == 0 else _row_tile(t, 512)
    rows = PEER_EXPERT_ROWS
    eb = rows * nk
    keys2 = keys.reshape(heads * 2, nk, dh)
    cidx = _candidate_index_rows(tm)
    n_pairs = n_exp // (2 * eb)
    assert n_pairs == heads, "one pair of expert blocks per selection step"
    return pl.pallas_call(
        functools.partial(_peer_kernel, alpha=alpha, heads=heads, nk=nk, rows_per_block=rows),
        grid=(t // tm, heads + n_pairs),
        in_specs=[
            pl.BlockSpec((tm, d), lambda i, j: (i, 0)),
            pl.BlockSpec(wq_t.shape, lambda i, j: (0, 0), pipeline_mode=pl.Buffered(1)),
            pl.BlockSpec(keys2.shape, lambda i, j: (0, 0, 0)),
            pl.BlockSpec(cidx.shape, lambda i, j: (0, 0)),
            pl.BlockSpec((2 * eb, d), lambda i, j: (jnp.minimum(j, heads - 1), 0)),
            pl.BlockSpec((d, 2 * eb), lambda i, j: (0, jnp.maximum(j - heads, 0))),
            pl.BlockSpec((1, d), lambda i, j: (0, 0)),
            pl.BlockSpec((1, d), lambda i, j: (0, 0)),
        ],
        out_specs=pl.BlockSpec((tm, d), lambda i, j: (i, 0), pipeline_mode=pl.Buffered(1)),
        out_shape=jax.ShapeDtypeStruct((t, d), F32),
        scratch_shapes=[
            pltpu.VMEM((tm, d), BF16),
            pltpu.VMEM((wq_t.shape[0], tm), BF16),
            pltpu.VMEM((n_exp, tm), BF16),
            pltpu.VMEM((heads, nk, tm), F32), pltpu.VMEM((heads, nk, tm), F32),
            pltpu.VMEM((heads, nk, tm), BF16), pltpu.VMEM((heads, nk, tm), BF16),
            pltpu.VMEM((eb, tm), BF16), pltpu.VMEM((eb, tm), BF16),
            pltpu.VMEM((d, tm), F32),
        ],
        compiler_params=_cparams("parallel", "arbitrary"),
        name="peer",
    )(x, wq_t, keys2, cidx, u_tab, v_tab_t, g.reshape(1, d), b.reshape(1, d))


def _pad_tokens(a, length):
    return jnp.pad(a, ((0, 0), (0, length - a.shape[1]), (0, 0)))


def _sequence_mixers(proj3, lb, s_hg, s_re, s_im, s5_prm, d_hg):
    bsz, length, _ = proj3.shape
    lpad = -(-length // SUBLANES) * SUBLANES
    o, s_hg_new = _hgrn(_pad_tokens(proj3, lpad) if lpad != length else proj3, lb, s_hg, valid=length)
    lc = S5_CHUNK if length % S5_CHUNK == 0 else length
    y, s_re_new, s_im_new = _s5(proj3.reshape(bsz * length, -1), s_re, s_im, s5_prm[lc],
                                length=length, lc=lc, u_col=4 * d_hg)
    return o[:, :length], y.reshape(bsz, length, -1), s_hg_new, s_re_new, s_im_new


def kernel(x_prompt, x_sample, state_hgrn, state_ssm_re, state_ssm_im, meta_tokens, ln_emb_g, ln_emb_b,
           lb_logits, w_in, b_in, hg_norm_g, ssm_a_re, ssm_a_im, ssm_log_dt, ssm_b_re, ssm_b_im,
           ssm_c_re, ssm_c_im, ssm_d, w_glu, b_glu, w_out, ln1_g, ln1_b, peer_w_q, peer_keys,
           peer_u, peer_v, ln2_g, ln2_b):
    depth = w_in.shape[0]
    alpha = (2.0 * depth) ** 0.25
    bp, seq, d = x_prompt.shape
    bs, dseq, _ = x_sample.shape
    n_meta = meta_tokens.shape[0]
    heads = state_hgrn.shape[2]
    d_hg = heads * LANES
    groups, pstate = state_ssm_re.shape[2], state_ssm_re.shape[3]
    lbs = jnp.cumsum(jax.nn.softmax(lb_logits.astype(F32), axis=0), axis=0)

    xp = x_prompt.astype(F32).reshape(bp * seq, d)
    xm = meta_tokens.astype(F32)
    xs = x_sample.astype(F32).reshape(bs * dseq, d)
    n_s = bs * dseq
    hg_p, re_p, im_p, hg_s, re_s, im_s = [], [], [], [], [], []
    for l in range(depth):
        last = l == depth - 1
        w_in_b = w_in[l].astype(BF16)
        lb = lbs[l].reshape(1, d_hg)
        s5_prm = {lc: _s5_params(ssm_a_re[l], ssm_a_im[l], ssm_log_dt[l], ssm_b_re[l], ssm_b_im[l],
                                 ssm_c_re[l], ssm_c_im[l], ssm_d[l], lc)
                  for lc in {S5_CHUNK if n % S5_CHUNK == 0 else n for n in (n_meta, seq, dseq)}}
        wglu = w_glu[l].astype(BF16)
        wo_hg = w_out[l, :d_hg].astype(BF16)
        wo_ssm = w_out[l, d_hg:].astype(BF16)
        wq_t = peer_w_q[l].T.astype(BF16)
        keys = peer_keys[l].astype(BF16)
        u_tab = peer_u[l].astype(BF16)
        v_tab_t = peer_v[l].T.astype(BF16)

        xsm = jnp.concatenate([xs, xm], axis=0)
        x0_p, proj_p = _inproj(xp, ln_emb_g, ln_emb_b, w_in_b, b_in[l], apply_ln=(l == 0))
        x0_sm, proj_sm = _inproj(xsm, ln_emb_g, ln_emb_b, w_in_b, b_in[l], apply_ln=(l == 0))
        d_in = proj_p.shape[1]

        proj_m = proj_sm[n_s:].reshape(1, n_meta, d_in)
        proj_m8 = jnp.broadcast_to(proj_m, (SUBLANES, n_meta, d_in))
        zero_hg = jnp.zeros((SUBLANES, heads, LANES, LANES), F32)
        zero_ss = jnp.zeros((SUBLANES, groups, pstate), F32)
        o_m, y_m, hg_m, re_m, im_m = _sequence_mixers(proj_m8, lb, zero_hg, zero_ss, zero_ss, s5_prm, d_hg)

        o_p, y_p, shg, sre, sim = _sequence_mixers(
            proj_p.reshape(bp, seq, d_in), lb,
            jnp.broadcast_to(hg_m[:1], (bp,) + hg_m.shape[1:]),
            jnp.broadcast_to(re_m[:1], (bp,) + re_m.shape[1:]),
            jnp.broadcast_to(im_m[:1], (bp,) + im_m.shape[1:]), s5_prm, d_hg)
        hg_p.append(shg)
        re_p.append(sre)
        im_p.append(sim)

        o_s, y_s, shg, sre, sim = _sequence_mixers(
            proj_sm[:n_s].reshape(bs, dseq, d_in), lb, state_hgrn[l].astype(F32),
            state_ssm_re[l].astype(F32), state_ssm_im[l].astype(F32), s5_prm, d_hg)
        hg_s.append(shg)
        re_s.append(sre)
        im_s.append(sim)

        mix_args = (hg_norm_g[l], wglu, b_glu[l], wo_hg, wo_ssm, ln1_g[l], ln1_b[l])
        peer_args = (wq_t, keys, u_tab, v_tab_t, ln2_g[l], ln2_b[l])
        x1_p = _mix(x0_p, o_p.reshape(bp * seq, d_hg), proj_p, y_p.reshape(bp * seq, -1), *mix_args, alpha=alpha)
        xp = _peer(x1_p, *peer_args, alpha=alpha)
        x1_s = _mix(x0_sm[:n_s], o_s.reshape(n_s, d_hg), proj_sm[:n_s], y_s.reshape(n_s, -1), *mix_args, alpha=alpha)
        xs = _peer(x1_s, *peer_args, alpha=alpha)
        if not last:
            x1_m = _mix(jnp.broadcast_to(x0_sm[n_s:], (n_meta, d)), o_m[0], proj_sm[n_s:], y_m[0], *mix_args, alpha=alpha)
            xm = _peer(x1_m, *peer_args, alpha=alpha)

    y_prompt = xp.reshape(bp, seq, d).astype(x_prompt.dtype)
    y_sample = xs.reshape(bs, dseq, d).astype(x_sample.dtype)
    return (y_prompt, y_sample, jnp.stack(hg_p), jnp.stack(re_p), jnp.stack(im_p),
            jnp.stack(hg_s), jnp.stack(re_s), jnp.stack(im_s))
```

```python
import functools
import math

import jax
import jax.numpy as jnp
from jax import lax
from jax.experimental import pallas as pl
from jax.experimental.pallas import tpu as pltpu

F32 = jnp.float32
BF16 = jnp.bfloat16

LN_EPS = 1e-5
RMS_EPS = 1e-6
HG_CHUNK = 16
HG_CHUNKS_PER_ITER = 16
HG_SEQS_PER_STEP = 16
SSM_GROUP = 16
S5_CHUNK = 16
S5_GROUP_TILE = 8
S5_SEQS_PER_STEP = 4
PEER_TOPK = 16
LANES = 128
SUBLANES = 8
VMEM_LIMIT = 60 * 1024 * 1024
PEER_EXPERT_ROWS = 8


def _cparams(*sem):
    return pltpu.CompilerParams(dimension_semantics=sem, vmem_limit_bytes=VMEM_LIMIT)


def _layer_norm(x, g, b):
    mu = jnp.mean(x, axis=-1, keepdims=True)
    xc = x - mu
    var = jnp.mean(xc * xc, axis=-1, keepdims=True)
    return xc * lax.rsqrt(var + LN_EPS) * g + b


def _sigmoid(x):
    return 1.0 / (1.0 + jnp.exp(-x))


def _gelu(x):
    c1 = -2.0 * math.sqrt(2.0 / math.pi)
    return x / (1.0 + jnp.exp(x * (c1 + (c1 * 0.044715) * (x * x))))


def _row_tile(t, target):
    best = None
    for cand in range(SUBLANES, min(t, target) + 1, SUBLANES):
        if t % cand == 0:
            best = cand
    return best if best is not None else t


def _inproj_kernel(x_ref, g_ref, b_ref, w_ref, bi_ref, x0_ref, proj_ref, *, apply_ln):
    x = x_ref[...]
    if apply_ln:
        x = _layer_norm(x, g_ref[...], b_ref[...])
    x0_ref[...] = x
    proj_ref[...] = jnp.dot(x.astype(BF16), w_ref[...], preferred_element_type=F32) + bi_ref[...]


def _inproj(x, g, b, w_bf16, bias, *, apply_ln):
    t, d = x.shape
    d_in = w_bf16.shape[1]
    tm = _row_tile(t, 512)
    return pl.pallas_call(
        functools.partial(_inproj_kernel, apply_ln=apply_ln),
        grid=(t // tm,),
        in_specs=[
            pl.BlockSpec((tm, d), lambda i: (i, 0)),
            pl.BlockSpec((1, d), lambda i: (0, 0)),
            pl.BlockSpec((1, d), lambda i: (0, 0)),
            pl.BlockSpec((d, d_in), lambda i: (0, 0)),
            pl.BlockSpec((1, d_in), lambda i: (0, 0)),
        ],
        out_specs=[
            pl.BlockSpec((tm, d), lambda i: (i, 0)),
            pl.BlockSpec((tm, d_in), lambda i: (i, 0)),
        ],
        out_shape=[jax.ShapeDtypeStruct((t, d), F32), jax.ShapeDtypeStruct((t, d_in), F32)],
        compiler_params=_cparams("parallel"),
        name="inproj",
    )(x, g.reshape(1, d), b.reshape(1, d), w_bf16, bias.reshape(1, d_in))


def _hgrn_kernel(q_ref, f_ref, v_ref, lb_ref, s0_ref, o_ref, s_ref, st_ref,
                 *, chunk, n_iters, per_iter, seqs, valid):
    lb = lb_ref[...]
    ones = jnp.ones((LANES, LANES), BF16)
    groups = seqs * per_iter
    shape = (groups, chunk, LANES)
    row = lax.broadcasted_iota(jnp.int32, shape, 1)
    span = per_iter * chunk
    nt = (((1,), (1,)), ((), ()))
    tn = (((0,), (0,)), ((), ()))

    for b in range(seqs):
        st_ref[b] = s0_ref[b].T

    def take(ref, r0):
        return jnp.concatenate([ref[b, pl.ds(r0, span), :].reshape(per_iter, chunk, LANES)
                                for b in range(seqs)], axis=0)

    def step(i, carry):
        r0 = pl.multiple_of(i * span, span)
        q = take(q_ref, r0)
        fp = take(f_ref, r0)
        v = take(v_ref, r0)
        f = lb + (1.0 - lb) * _sigmoid(fp)
        logf = jnp.log(f)
        kk = (1.0 - lb) * _sigmoid(-fp)
        if valid < chunk:
            live = row < valid
            logf = jnp.where(live, logf, 0.0)
            kk = jnp.where(live, kk, 0.0)
            q = jnp.where(live, q, 0.0)
            v = jnp.where(live, v, 0.0)
        pick = lambda a, s: jnp.broadcast_to(a[:, s:s + 1, :], shape)
        bc = jnp.zeros(shape, F32)
        for s in range(chunk):
            bc = bc + jnp.where(row >= s, pick(logf, s), 0.0)
        split = chunk // 2 if chunk % (2 * SUBLANES) == 0 else 0
        lows = [split if s >= split else 0 for s in range(chunk)]
        prods = []
        for s, lo in zip(range(chunk), lows):
            part = (groups, chunk - lo, LANES)
            near = lambda a: jnp.broadcast_to(a[:, s:s + 1, :], part)
            e = jnp.where(row[:, lo:, :] >= s, jnp.exp(bc[:, lo:, :] - near(bc)), 0.0)
            prods.append((q[:, lo:, :] * e * near(kk)).reshape(groups * (chunk - lo), LANES))
        p_all = jnp.concatenate(prods, axis=0).astype(BF16)
        att = jnp.dot(p_all, ones, preferred_element_type=F32)
        o = jnp.zeros(shape, F32)
        o_low = jnp.zeros((groups, chunk - split, LANES), F32)
        off = 0
        for s, lo in zip(range(chunk), lows):
            part = (groups, chunk - lo, LANES)
            n = groups * (chunk - lo)
            term = att[off:off + n, :].reshape(part) * jnp.broadcast_to(v[:, s:s + 1, :], part)
            off += n
            if lo == 0:
                o = o + term
            else:
                o_low = o_low + term
        if split:
            o = o + jnp.concatenate([jnp.zeros((groups, split, LANES), F32), o_low], axis=1)
        qs = (q * jnp.exp(bc)).astype(BF16)
        kt = (kk * jnp.exp(pick(bc, chunk - 1) - bc)).astype(BF16)
        vb = v.astype(BF16)
        decay = jnp.exp(bc[:, chunk - 1:chunk, :])
        ds = [lax.dot_general(vb[g], kt[g], tn, preferred_element_type=F32) for g in range(groups)]
        outs = []
        for b in range(seqs):
            st = st_ref[b]
            before = []
            for n in range(per_iter):
                g = b * per_iter + n
                before.append(st.astype(BF16))
                st = st * decay[g] + ds[g]
            st_ref[b] = st
            for n in range(per_iter):
                g = b * per_iter + n
                outs.append(o[g] + lax.dot_general(qs[g], before[n], nt, preferred_element_type=F32))
        for b in range(seqs):
            ob = jnp.concatenate(outs[b * per_iter:(b + 1) * per_iter], axis=0)
            ob = ob * lax.rsqrt(jnp.mean(ob * ob, axis=-1, keepdims=True) + RMS_EPS)
            o_ref[b, pl.ds(r0, span), :] = ob
        return carry

    lax.fori_loop(0, n_iters, step, 0)
    for b in range(seqs):
        s_ref[b] = st_ref[b].T


def _hgrn(proj3, lb, s0, *, valid):
    bsz, length, _ = proj3.shape
    heads = s0.shape[1]
    chunk = HG_CHUNK if length % HG_CHUNK == 0 else length
    n_chunks = length // chunk
    divisor = lambda n, cap: max(c for c in range(1, cap + 1) if n % c == 0)
    if n_chunks > 1:
        seqs, per_iter = 1, divisor(n_chunks, HG_CHUNKS_PER_ITER)
    else:
        seqs, per_iter = divisor(bsz, HG_SEQS_PER_STEP), 1
    seq_spec = lambda off: pl.BlockSpec((seqs, length, LANES), lambda b, h: (b, 0, off + h))
    state_spec = pl.BlockSpec((seqs, None, LANES, LANES), lambda b, h: (b, h, 0, 0))
    return pl.pallas_call(
        functools.partial(_hgrn_kernel, chunk=chunk, n_iters=n_chunks // per_iter, per_iter=per_iter,
                          seqs=seqs, valid=min(valid, chunk)),
        grid=(bsz // seqs, heads),
        in_specs=[
            seq_spec(0), seq_spec(heads), seq_spec(2 * heads),
            pl.BlockSpec((1, LANES), lambda b, h: (0, h)),
            state_spec,
        ],
        out_specs=[
            pl.BlockSpec((seqs, length, LANES), lambda b, h: (b, 0, h)),
            state_spec,
        ],
        out_shape=[jax.ShapeDtypeStruct((bsz, length, heads * LANES), F32),
                   jax.ShapeDtypeStruct(s0.shape, F32)],
        scratch_shapes=[pltpu.VMEM((seqs, LANES, LANES), F32)],
        compiler_params=_cparams("parallel", "parallel"),
        name="hgrn2",
    )(proj3, proj3, proj3, lb, s0)


def _s5_kernel(u_ref, bb_ref, c_ref, are_ref, aim_ref, alre_ref, alim_ref, d_ref, h0re_ref, h0im_ref,
               y_ref, hre_ref, him_ref, hpre_ref, hpim_ref, *, lc, n_chunks, seqs):
    rows = seqs * n_chunks
    half = are_ref.shape[-1]
    a_re = are_ref[...]
    a_im = aim_ref[...]
    bb = bb_ref[...]

    def drive(t):
        u_t = u_ref[pl.ds(t, rows, stride=lc), :]
        return u_t, jnp.dot(u_t.astype(BF16), bb, preferred_element_type=F32)

    def advance(h_re, h_im, bu):
        return (a_re * h_re - a_im * h_im + bu[:, :half], a_re * h_im + a_im * h_re + bu[:, half:])

    h_re = jnp.zeros((rows, half), F32)
    h_im = jnp.zeros((rows, half), F32)
    for t in range(lc):
        h_re, h_im = advance(h_re, h_im, drive(t)[1])

    n_tiles = half // LANES

    def put(ref, idx, val):
        for k in range(n_tiles):
            ref[k, idx, :] = val[:, k * LANES:(k + 1) * LANES]

    def get(ref, idx):
        return jnp.concatenate([ref[k, idx, :] for k in range(n_tiles)], axis=-1)

    put(hpre_ref, slice(None), h_re)
    put(hpim_ref, slice(None), h_im)

    al_re = alre_ref[...]
    al_im = alim_ref[...]

    def carry_step(c, carry):
        c_re, c_im = carry
        idx = pl.ds(c, seqs, stride=n_chunks)
        l_re = get(hpre_ref, idx)
        l_im = get(hpim_ref, idx)
        put(hpre_ref, idx, c_re)
        put(hpim_ref, idx, c_im)
        return (al_re * c_re - al_im * c_im + l_re, al_re * c_im + al_im * c_re + l_im)

    c_re, c_im = lax.fori_loop(0, n_chunks, carry_step, (h0re_ref[...], h0im_ref[...]))
    hre_ref[...] = c_re
    him_ref[...] = c_im

    h_re = get(hpre_ref, slice(None))
    h_im = get(hpim_ref, slice(None))
    c_w = c_ref[...]
    d_vec = d_ref[...]
    for t in range(lc):
        u_t, bu = drive(t)
        h_re, h_im = advance(h_re, h_im, bu)
        y = (jnp.dot(h_re.astype(BF16), c_w[:half], preferred_element_type=F32)
             + jnp.dot(h_im.astype(BF16), c_w[half:], preferred_element_type=F32) + d_vec * u_t)
        y_ref[pl.ds(t, rows, stride=lc), :] = y


def _s5_params(a_re, a_im, log_dt, b_re, b_im, c_re, c_im, d, lc):
    groups, pstate = a_re.shape
    dt = jnp.exp(log_dt.astype(F32))[:, None]
    a_re = a_re.astype(F32)
    a_im = a_im.astype(F32)
    zr, zi = a_re * dt, a_im * dt
    mag = jnp.exp(zr)
    ab_re, ab_im = mag * jnp.cos(zi), mag * jnp.sin(zi)
    den = a_re * a_re + a_im * a_im
    nr = ab_re - 1.0
    coef_re = (nr * a_re + ab_im * a_im) / den
    coef_im = (ab_im * a_re - nr * a_im) / den
    b_re = b_re.astype(F32)
    b_im = b_im.astype(F32)
    bb_re = coef_re[..., None] * b_re - coef_im[..., None] * b_im
    bb_im = coef_re[..., None] * b_im + coef_im[..., None] * b_re
    pm = jnp.exp(zr * lc)
    al_re, al_im = pm * jnp.cos(zi * lc), pm * jnp.sin(zi * lc)
    gt = S5_GROUP_TILE
    tiles = groups // gt
    swap = lambda x: jnp.transpose(x, (0, 2, 1))
    bb = jnp.concatenate([_block_diag(swap(bb_re), gt), _block_diag(swap(bb_im), gt)], axis=-1)
    cw = jnp.concatenate([_block_diag(swap(c_re.astype(F32)), gt), -_block_diag(swap(c_im.astype(F32)), gt)], axis=1)
    lane = lambda x: x.reshape(tiles, 1, gt * pstate)
    return dict(bb=bb.astype(BF16), cw=cw.astype(BF16), are=lane(ab_re), aim=lane(ab_im),
                alre=lane(al_re), alim=lane(al_im), d=d.astype(F32).reshape(1, groups * SSM_GROUP))


def _block_diag(x, gm):
    if gm == 1:
        return x
    g, a, b = x.shape
    x = x.reshape(g // gm, gm, a, b)
    eye = jnp.eye(gm, dtype=x.dtype)
    return (x[:, :, :, None, :] * eye[None, :, None, :, None]).reshape(g // gm, gm * a, gm * b)


def _s5(proj, h0_re, h0_im, prm, *, length, lc, u_col):
    bsz, n_groups, pstate = h0_re.shape
    d_ssm = n_groups * SSM_GROUP
    tiles = n_groups // S5_GROUP_TILE
    half = S5_GROUP_TILE * pstate
    n_chunks = length // lc
    seqs = bsz if n_chunks == 1 else min(bsz, S5_SEQS_PER_STEP)
    steps = bsz // seqs
    rows = seqs * n_chunks
    h0r = h0_re.reshape(steps, seqs, n_groups * pstate)
    h0i = h0_im.reshape(steps, seqs, n_groups * pstate)
    tspec = lambda a, b: pl.BlockSpec((None, a, b), lambda i, g: (g, 0, 0))
    sspec = pl.BlockSpec((None, seqs, half), lambda i, g: (i, 0, g))
    y, hre, him = pl.pallas_call(
        functools.partial(_s5_kernel, lc=lc, n_chunks=n_chunks, seqs=seqs),
        grid=(steps, tiles),
        in_specs=[
            pl.BlockSpec((seqs * length, LANES), lambda i, g: (i, u_col // LANES + g)),
            tspec(LANES, 2 * half), tspec(2 * half, LANES),
            tspec(1, half), tspec(1, half), tspec(1, half), tspec(1, half),
            pl.BlockSpec((1, LANES), lambda i, g: (0, g)),
            sspec, sspec,
        ],
        out_specs=[
            pl.BlockSpec((seqs * length, LANES), lambda i, g: (i, g)),
            sspec, sspec,
        ],
        out_shape=[jax.ShapeDtypeStruct((bsz * length, d_ssm), F32),
                   jax.ShapeDtypeStruct(h0r.shape, F32),
                   jax.ShapeDtypeStruct(h0r.shape, F32)],
        scratch_shapes=[pltpu.VMEM((half // LANES, rows, LANES), F32),
                        pltpu.VMEM((half // LANES, rows, LANES), F32)],
        compiler_params=_cparams("parallel", "parallel"),
        name="s5",
    )(proj, prm['bb'], prm['cw'], prm['are'], prm['aim'], prm['alre'], prm['alim'], prm['d'], h0r, h0i)
    return y, hre.reshape(h0_re.shape), him.reshape(h0_re.shape)


def _mix_kernel(x_ref, o_ref, gate_ref, y_ref, hg_g_ref, wglu_ref, bglu_ref, wo_hg_ref, wo_ssm_ref,
                g_ref, b_ref, out_ref, *, alpha):
    gate = gate_ref[...]
    o_hg = o_ref[...] * hg_g_ref[...] * (gate * _sigmoid(gate))
    z = _gelu(y_ref[...])
    glu = jnp.dot(z.astype(BF16), wglu_ref[...], preferred_element_type=F32) + bglu_ref[...]
    o_ssm = z * _sigmoid(glu)
    mix = (jnp.dot(o_hg.astype(BF16), wo_hg_ref[...], preferred_element_type=F32)
           + jnp.dot(o_ssm.astype(BF16), wo_ssm_ref[...], preferred_element_type=F32))
    out_ref[...] = _layer_norm(alpha * x_ref[...] + mix, g_ref[...], b_ref[...])


def _mix(x0, o_hg, proj, y_ssm, hg_g, wglu, bglu, wo_hg, wo_ssm, g, b, *, alpha):
    t, d = x0.shape
    d_hg = o_hg.shape[1]
    d_ssm = y_ssm.shape[1]
    tm = _row_tile(t, 512)
    gate_block = (3 * d_hg) // d_hg
    row = lambda w: pl.BlockSpec((tm, w), lambda i: (i, 0))
    full = lambda a, bb: pl.BlockSpec((a, bb), lambda i: (0, 0))
    return pl.pallas_call(
        functools.partial(_mix_kernel, alpha=alpha),
        grid=(t // tm,),
        in_specs=[
            row(d), row(d_hg),
            pl.BlockSpec((tm, d_hg), lambda i: (i, gate_block)),
            row(d_ssm),
            full(1, d_hg), full(d_ssm, d_ssm), full(1, d_ssm), full(d_hg, d), full(d_ssm, d),
            full(1, d), full(1, d),
        ],
        out_specs=row(d),
        out_shape=jax.ShapeDtypeStruct((t, d), F32),
        compiler_params=_cparams("parallel"),
        name="mix",
    )(x0, o_hg, proj, y_ssm, hg_g.reshape(1, d_hg), wglu, bglu.reshape(1, d_ssm), wo_hg, wo_ssm,
      g.reshape(1, d), b.reshape(1, d))


def _top_rows(s, k):
    n, t = s.shape
    iota = lax.broadcasted_iota(jnp.int32, (n, t), 0).astype(F32)
    krow = lax.broadcasted_iota(jnp.int32, (k, t), 0)
    rank = jnp.full((n, t), float(k), F32)
    vals = jnp.zeros((k, t), F32)
    for a in range(k):
        m = jnp.max(s, axis=0, keepdims=True)
        idx = jnp.min(jnp.where(s == m, iota, float(n)), axis=0, keepdims=True)
        hit = iota == idx
        rank = jnp.where(hit, float(a), rank)
        s = jnp.where(hit, -jnp.inf, s)
        vals = jnp.where(krow == a, jnp.broadcast_to(m, (k, t)), vals)
    return vals, rank


def _candidate_pieces(t1, t2):
    k = PEER_TOPK
    t = t1.shape[1]
    bc = lambda r, n: jnp.broadcast_to(r, (n, t))
    pieces = [(bc(t1[0:1], k), t2, [b for b in range(k)], [True] * k)]
    half = k // 2
    for a in range(1, half):
        nb = k // (a + 1)
        pieces.append((bc(t1[a:a + 1], half), t2[0:half], [a * k + b for b in range(half)],
                       [b < nb for b in range(half)]))
    pieces.append((t1[half:k], bc(t2[0:1], half), [(half + r) * k for r in range(half)], [True] * half))
    return pieces


def _candidate_index_rows(t):
    k = PEER_TOPK
    dummy = jnp.zeros((k, 1), F32)
    vals = [i if ok else k * k for _, _, idx, val in _candidate_pieces(dummy, dummy) for i, ok in zip(idx, val)]
    return jnp.broadcast_to(jnp.asarray(vals, F32)[:, None], (len(vals), t))


def _peer_select(s1, s2, cidx):
    k = PEER_TOPK
    t = s1.shape[1]
    big = float(k * k)
    top1, rank1 = _top_rows(s1, k)
    top2, rank2 = _top_rows(s2, k)
    pieces = _candidate_pieces(top1, top2)
    cand = jnp.concatenate([x + y for x, y, _, _ in pieces], axis=0)
    cand = jnp.where(cidx < big, cand, -jnp.inf)
    chosen = jnp.zeros(cand.shape, F32)
    for _ in range(k):
        m = jnp.max(cand, axis=0, keepdims=True)
        idx = jnp.min(jnp.where(cand == m, cidx, big), axis=0, keepdims=True)
        hit = cidx == idx
        chosen = jnp.where(hit, 1.0, chosen)
        cand = jnp.where(hit, -jnp.inf, cand)
    e1 = jnp.exp(top1 - jnp.broadcast_to(top1[0:1], (k, t)))
    e2 = jnp.exp(top2 - jnp.broadcast_to(top2[0:1], (k, t)))
    ecand = jnp.concatenate([x * y for x, y, _, _ in _candidate_pieces(e1, e2)], axis=0)
    z = jnp.sum(chosen * ecand, axis=0, keepdims=True)
    counts = []
    off = 0
    half = k // 2
    for pi, (x, _, _, _) in enumerate(pieces):
        rows = x.shape[0]
        blk = chosen[off:off + rows]
        off += rows
        if pi < len(pieces) - 1:
            counts.append(jnp.sum(blk, axis=0, keepdims=True))
        else:
            counts.extend(blk[r:r + 1] for r in range(half))
    nk = s1.shape[0]
    c1 = jnp.zeros((nk, t), F32)
    for a in range(k):
        c1 = jnp.where(rank1 == float(a), jnp.broadcast_to(counts[a], (nk, t)), c1)
    phi = jnp.exp(s1 - jnp.broadcast_to(top1[0:1], (nk, t))) * jnp.broadcast_to(1.0 / z, (nk, t))
    psi = jnp.exp(s2 - jnp.broadcast_to(top2[0:1], (nk, t)))
    return c1, phi, rank2, psi


def _sort_network(n):
    out, p = [], 1
    while p < n:
        k = p
        while k >= 1:
            for j in range(k % p, n - k, 2 * k):
                for i in range(min(k, n - j - k)):
                    if (i + j) // (2 * p) == (i + j + k) // (2 * p):
                        out.append((i + j, i + j + k))
            k //= 2
        p *= 2
    return out


def _sorted_top(s):
    n = s.shape[0] // SUBLANES
    x = [s[i * SUBLANES:(i + 1) * SUBLANES, :] for i in range(n)]
    for i, j in _sort_network(n):
        x[i], x[j] = jnp.maximum(x[i], x[j]), jnp.minimum(x[i], x[j])
    shift = SUBLANES // 2
    while shift >= 1:
        r = [pltpu.roll(v, shift, axis=0) for v in x]
        x = [jnp.maximum(x[i], r[n - 1 - i]) for i in range(n)]
        d = n // 2
        while d >= 1:
            for i in range(n):
                if (i & d) == 0:
                    x[i], x[i + d] = jnp.maximum(x[i], x[i + d]), jnp.minimum(x[i], x[i + d])
            d //= 2
        shift //= 2
    return x


def _peer_select_fast(s1, s2, cidx):
    k = PEER_TOPK
    nk, t = s1.shape
    nblk = nk // SUBLANES
    big = float(k * k)
    l1 = _sorted_top(s1)
    l2 = _sorted_top(s2)
    sub = lax.broadcasted_iota(jnp.int32, (SUBLANES, t), 0)

    def stack(lst):
        halves = []
        for base in (0, SUBLANES):
            blk = lst[base]
            for a in range(1, SUBLANES):
                blk = jnp.where(sub == a, lst[base + a], blk)
            halves.append(blk)
        return jnp.concatenate(halves, axis=0)

    top1 = stack(l1)
    top2 = stack(l2)
    pieces = _candidate_pieces(top1, top2)
    cand = jnp.concatenate([x + y for x, y, _, _ in pieces], axis=0)
    cand = jnp.where(cidx < big, cand, -jnp.inf)
    chosen = jnp.zeros(cand.shape, F32)
    for _ in range(k):
        hit = cand == jnp.max(cand, axis=0, keepdims=True)
        chosen = jnp.where(hit, 1.0, chosen)
        cand = jnp.where(hit, -jnp.inf, cand)
    e1 = jnp.exp(top1 - jnp.broadcast_to(top1[0:1], (k, t)))
    e2 = jnp.exp(top2 - jnp.broadcast_to(top2[0:1], (k, t)))
    ecand = jnp.concatenate([x * y for x, y, _, _ in _candidate_pieces(e1, e2)], axis=0)
    z = jnp.sum(chosen * ecand, axis=0, keepdims=True)
    counts = []
    off = 0
    half = k // 2
    for pi, (x, _, _, _) in enumerate(pieces):
        rows = x.shape[0]
        blk = chosen[off:off + rows]
        off += rows
        if pi < len(pieces) - 1:
            counts.append(jnp.sum(blk, axis=0, keepdims=True))
        else:
            counts.extend(blk[r:r + 1] for r in range(half))
    c1_blocks, r2_blocks = [], []
    cnt1 = jnp.zeros((SUBLANES, t), F32)
    cnt2 = jnp.zeros((SUBLANES, t), F32)
    for i in range(nblk):
        x1 = s1[i * SUBLANES:(i + 1) * SUBLANES, :]
        x2 = s2[i * SUBLANES:(i + 1) * SUBLANES, :]
        c1b = jnp.zeros((SUBLANES, t), F32)
        r2b = jnp.zeros((SUBLANES, t), F32)
        for a in range(k):
            c1b = jnp.where(x1 == l1[a], jnp.broadcast_to(counts[a], (SUBLANES, t)), c1b)
            r2b = jnp.where(x2 < l2[a], float(a + 1), r2b)
        c1_blocks.append(c1b)
        r2_blocks.append(r2b)
        cnt1 = cnt1 + jnp.where(x1 >= l1[k - 1], 1.0, 0.0)
        cnt2 = cnt2 + jnp.where(x2 >= l2[k - 1], 1.0, 0.0)
    c1 = jnp.concatenate(c1_blocks, axis=0)
    r2 = jnp.concatenate(r2_blocks, axis=0)
    phi = jnp.exp(s1 - jnp.broadcast_to(top1[0:1], (nk, t))) * jnp.broadcast_to(1.0 / z, (nk, t))
    psi = jnp.exp(s2 - jnp.broadcast_to(top2[0:1], (nk, t)))
    bad = jnp.zeros((SUBLANES, t), F32)
    for a in range(k - 1):
        bad = jnp.where(l1[a] <= l1[a + 1], 1.0, bad)
        bad = jnp.where(l2[a] <= l2[a + 1], 1.0, bad)
    fk = float(k)
    bad = jnp.max(bad, axis=0, keepdims=True)
    bad = jnp.where(jnp.sum(cnt1, axis=0, keepdims=True) != fk, 1.0, bad)
    bad = jnp.where(jnp.sum(cnt2, axis=0, keepdims=True) != fk, 1.0, bad)
    bad = jnp.where(jnp.sum(chosen, axis=0, keepdims=True) != fk, 1.0, bad)
    return c1, phi, r2, psi, bad


def _rows_bf16(row, n):
    pack = 2 * SUBLANES
    tile = jnp.broadcast_to(row, (pack, row.shape[1])).astype(BF16)
    return jnp.concatenate([tile] * (n // pack), axis=0)


def _peer_kernel(x_ref, wq_ref, keys_ref, cidx_ref, u0_ref, u1_ref, u2_ref, vt_ref, g_ref, b_ref,
                 out_ref, xb_ref, qt_ref, c1_ref, phi_ref, r2_ref, psi_ref, uta_ref, utb_ref, wa_ref, wb_ref,
                 acc_ref,
                 *, alpha, heads, nk, rows_per_step):
    j = pl.program_id(1)
    nt = (((1,), (1,)), ((), ()))
    eb = rows_per_step * nk

    @pl.when(j == 0)
    def _():
        xb = x_ref[...].astype(BF16)
        xb_ref[...] = xb
        acc_ref[...] = jnp.zeros_like(acc_ref)
        qt_ref[...] = lax.dot_general(wq_ref[...], xb, nt, preferred_element_type=F32).astype(BF16)

        def select_head(h, carry):
            sc = []
            for c in range(2):
                hc = h * 2 + c
                qt = qt_ref[pl.ds(pl.multiple_of(hc * LANES, LANES), LANES), :]
                sc.append(jnp.dot(keys_ref[hc], qt, preferred_element_type=F32))

            def put(c1, phi, r2, psi):
                c1_ref[h] = c1
                phi_ref[h] = phi
                r2_ref[h] = r2.astype(BF16)
                psi_ref[h] = psi.astype(BF16)

            if nk // SUBLANES == PEER_TOPK:
                *sel, bad = _peer_select_fast(sc[0], sc[1], cidx_ref[...])
                put(*sel)

                @pl.when(jnp.max(bad) > 0.0)
                def _():
                    put(*_peer_select(sc[0], sc[1], cidx_ref[...]))
            else:
                put(*_peer_select(sc[0], sc[1], cidx_ref[...]))
            return carry

        lax.fori_loop(0, heads, select_head, 0)
        uta_ref[...] = lax.dot_general(u0_ref[...], xb, nt, preferred_element_type=F32)

    xb = xb_ref[...]

    def mix_block(block, ut_ref, w_ref):
        for r in range(rows_per_step):
            n1 = block * rows_per_step + r
            gsum = None
            for h in range(heads):
                c1row = _rows_bf16(c1_ref[h, pl.ds(n1, 1), :], nk)
                phirow = _rows_bf16(phi_ref[h, pl.ds(n1, 1), :], nk)
                term = jnp.where(r2_ref[h] < c1row, phirow * psi_ref[h], jnp.zeros((), BF16))
                gsum = term if gsum is None else gsum + term
            act = _gelu(ut_ref[r * nk:(r + 1) * nk, :]).astype(BF16)
            w_ref[r * nk:(r + 1) * nk, :] = gsum * act

    utb_ref[...] = lax.dot_general(u1_ref[...], xb, nt, preferred_element_type=F32)
    mix_block(2 * j, uta_ref, wa_ref)
    acc_ref[...] += jnp.dot(vt_ref[:, :eb], wa_ref[...], preferred_element_type=F32)
    uta_ref[...] = lax.dot_general(u2_ref[...], xb, nt, preferred_element_type=F32)
    mix_block(2 * j + 1, utb_ref, wb_ref)
    acc_ref[...] += jnp.dot(vt_ref[:, eb:], wb_ref[...], preferred_element_type=F32)

    @pl.when(j == pl.num_programs(1) - 1)
    def _():
        ffn = acc_ref[...].T
        out_ref[...] = _layer_norm(alpha * x_ref[...] + ffn, g_ref[...], b_ref[...])


def _peer(x, wq_t, keys, u_tab, v_tab_t, g, b, *, alpha):
    t, d = x.shape
    heads, _, nk, dh = keys.shape
    n_exp = u_tab.shape[0]
    tm = 512 if t % 512 == 0 else _row_tile(t, 512)
    rows = PEER_EXPERT_ROWS
    eb = rows * nk
    keys2 = keys.reshape(heads * 2, nk, dh)
    cidx = _candidate_index_rows(tm)
    n_blocks = n_exp // eb
    return pl.pallas_call(
        functools.partial(_peer_kernel, alpha=alpha, heads=heads, nk=nk, rows_per_step=rows),
        grid=(t // tm, n_blocks // 2),
        in_specs=[
            pl.BlockSpec((tm, d), lambda i, j: (i, 0)),
            pl.BlockSpec(wq_t.shape, lambda i, j: (0, 0)),
            pl.BlockSpec(keys2.shape, lambda i, j: (0, 0, 0)),
            pl.BlockSpec(cidx.shape, lambda i, j: (0, 0)),
            pl.BlockSpec((eb, d), lambda i, j: (0, 0)),
            pl.BlockSpec((eb, d), lambda i, j: (2 * j + 1, 0)),
            pl.BlockSpec((eb, d), lambda i, j: ((2 * j + 2) % n_blocks, 0)),
            pl.BlockSpec((d, 2 * eb), lambda i, j: (0, j)),
            pl.BlockSpec((1, d), lambda i, j: (0, 0)),
            pl.BlockSpec((1, d), lambda i, j: (0, 0)),
        ],
        out_specs=pl.BlockSpec((tm, d), lambda i, j: (i, 0)),
        out_shape=jax.ShapeDtypeStruct((t, d), F32),
        scratch_shapes=[
            pltpu.VMEM((tm, d), BF16),
            pltpu.VMEM((wq_t.shape[0], tm), BF16),
            pltpu.VMEM((heads, nk, tm), F32), pltpu.VMEM((heads, nk, tm), F32),
            pltpu.VMEM((heads, nk, tm), BF16), pltpu.VMEM((heads, nk, tm), BF16),
            pltpu.VMEM((eb, tm), F32), pltpu.VMEM((eb, tm), F32),
            pltpu.VMEM((eb, tm), BF16), pltpu.VMEM((eb, tm), BF16),
            pltpu.VMEM((d, tm), F32),
        ],
        compiler_params=_cparams("parallel", "arbitrary"),
        name="peer",
    )(x, wq_t, keys2, cidx, u_tab, u_tab, u_tab, v_tab_t, g.reshape(1, d), b.reshape(1, d))


def _pad_tokens(a, length):
    return jnp.pad(a, ((0, 0), (0, length - a.shape[1]), (0, 0)))


def _sequence_mixers(proj3, lb, s_hg, s_re, s_im, s5_prm, d_hg):
    bsz, length, _ = proj3.shape
    lpad = -(-length // SUBLANES) * SUBLANES
    o, s_hg_new = _hgrn(_pad_tokens(proj3, lpad) if lpad != length else proj3, lb, s_hg, valid=length)
    lc = S5_CHUNK if length % S5_CHUNK == 0 else length
    y, s_re_new, s_im_new = _s5(proj3.reshape(bsz * length, -1), s_re, s_im, s5_prm[lc],
                                length=length, lc=lc, u_col=4 * d_hg)
    return o[:, :length], y.reshape(bsz, length, -1), s_hg_new, s_re_new, s_im_new


def kernel(x_prompt, x_sample, state_hgrn, state_ssm_re, state_ssm_im, meta_tokens, ln_emb_g, ln_emb_b,
           lb_logits, w_in, b_in, hg_norm_g, ssm_a_re, ssm_a_im, ssm_log_dt, ssm_b_re, ssm_b_im,
           ssm_c_re, ssm_c_im, ssm_d, w_glu, b_glu, w_out, ln1_g, ln1_b, peer_w_q, peer_keys,
           peer_u, peer_v, ln2_g, ln2_b):
    depth = w_in.shape[0]
    alpha = (2.0 * depth) ** 0.25
    bp, seq, d = x_prompt.shape
    bs, dseq, _ = x_sample.shape
    n_meta = meta_tokens.shape[0]
    heads = state_hgrn.shape[2]
    d_hg = heads * LANES
    groups, pstate = state_ssm_re.shape[2], state_ssm_re.shape[3]
    lbs = jnp.cumsum(jax.nn.softmax(lb_logits.astype(F32), axis=0), axis=0)

    xp = x_prompt.astype(F32).reshape(bp * seq, d)
    xm = meta_tokens.astype(F32)
    xs = x_sample.astype(F32).reshape(bs * dseq, d)
    n_s = bs * dseq
    hg_p, re_p, im_p, hg_s, re_s, im_s = [], [], [], [], [], []
    for l in range(depth):
        last = l == depth - 1
        w_in_b = w_in[l].astype(BF16)
        lb = lbs[l].reshape(1, d_hg)
        s5_prm = {lc: _s5_params(ssm_a_re[l], ssm_a_im[l], ssm_log_dt[l], ssm_b_re[l], ssm_b_im[l],
                                 ssm_c_re[l], ssm_c_im[l], ssm_d[l], lc)
                  for lc in {S5_CHUNK if n % S5_CHUNK == 0 else n for n in (n_meta, seq, dseq)}}
        wglu = w_glu[l].astype(BF16)
        wo_hg = w_out[l, :d_hg].astype(BF16)
        wo_ssm = w_out[l, d_hg:].astype(BF16)
        wq_t = peer_w_q[l].T.astype(BF16)
        keys = peer_keys[l].astype(BF16)
        u_tab = peer_u[l].astype(BF16)
        v_tab_t = peer_v[l].T.astype(BF16)

        xsm = jnp.concatenate([xs, xm], axis=0)
        x0_p, proj_p = _inproj(xp, ln_emb_g, ln_emb_b, w_in_b, b_in[l], apply_ln=(l == 0))
        x0_sm, proj_sm = _inproj(xsm, ln_emb_g, ln_emb_b, w_in_b, b_in[l], apply_ln=(l == 0))
        d_in = proj_p.shape[1]

        proj_m = proj_sm[n_s:].reshape(1, n_meta, d_in)
        proj_m8 = jnp.broadcast_to(proj_m, (SUBLANES, n_meta, d_in))
        zero_hg = jnp.zeros((SUBLANES, heads, LANES, LANES), F32)
        zero_ss = jnp.zeros((SUBLANES, groups, pstate), F32)
        o_m, y_m, hg_m, re_m, im_m = _sequence_mixers(proj_m8, lb, zero_hg, zero_ss, zero_ss, s5_prm, d_hg)

        o_p, y_p, shg, sre, sim = _sequence_mixers(
            proj_p.reshape(bp, seq, d_in), lb,
            jnp.broadcast_to(hg_m[:1], (bp,) + hg_m.shape[1:]),
            jnp.broadcast_to(re_m[:1], (bp,) + re_m.shape[1:]),
            jnp.broadcast_to(im_m[:1], (bp,) + im_m.shape[1:]), s5_prm, d_hg)
        hg_p.append(shg)
        re_p.append(sre)
        im_p.append(sim)

        o_s, y_s, shg, sre, sim = _sequence_mixers(
            proj_sm[:n_s].reshape(bs, dseq, d_in), lb, state_hgrn[l].astype(F32),
            state_ssm_re[l].astype(F32), state_ssm_im[l].astype(F32), s5_prm, d_hg)
        hg_s.append(shg)
        re_s.append(sre)
        im_s.append(sim)

        mix_args = (hg_norm_g[l], wglu, b_glu[l], wo_hg, wo_ssm, ln1_g[l], ln1_b[l])
        peer_args = (wq_t, keys, u_tab, v_tab_t, ln2_g[l], ln2_b[l])
        x1_p = _mix(x0_p, o_p.reshape(bp * seq, d_hg), proj_p, y_p.reshape(bp * seq, -1), *mix_args, alpha=alpha)
        xp = _peer(x1_p, *peer_args, alpha=alpha)
        x1_s = _mix(x0_sm[:n_s], o_s.reshape(n_s, d_hg), proj_sm[:n_s], y_s.reshape(n_s, -1), *mix_args, alpha=alpha)
        xs = _peer(x1_s, *peer_args, alpha=alpha)
        if not last:
            x1_m = _mix(jnp.broadcast_to(x0_sm[n_s:], (n_meta, d)), o_m[0], proj_sm[n_s:], y_m[0], *mix_args, alpha=alpha)
            xm = _peer(x1_m, *peer_args, alpha=alpha)

    y_prompt = xp.reshape(bp, seq, d).astype(x_prompt.dtype)
    y_sample = xs.reshape(bs, dseq, d).astype(x_sample.dtype)
    return (y_prompt, y_sample, jnp.stack(hg_p), jnp.stack(re_p), jnp.stack(im_p),
            jnp.stack(hg_s), jnp.stack(re_s), jnp.stack(im_s))
```

```python
import functools
import math

import jax
import jax.numpy as jnp
from jax import lax
from jax.experimental import pallas as pl
from jax.experimental.pallas import tpu as pltpu

F32 = jnp.float32
BF16 = jnp.bfloat16

LN_EPS = 1e-5
RMS_EPS = 1e-6
HG_CHUNK = 16
HG_CHUNKS_PER_ITER = 16
HG_SEQS_PER_STEP = 16
SSM_GROUP = 16
S5_CHUNK = 16
S5_GROUP_TILE = 8
S5_SEQS_PER_STEP = 4
PEER_TOPK = 16
LANES = 128
SUBLANES = 8
VMEM_LIMIT = 60 * 1024 * 1024
PEER_EXPERT_ROWS = 8


def _cparams(*sem):
    return pltpu.CompilerParams(dimension_semantics=sem, vmem_limit_bytes=VMEM_LIMIT)


def _layer_norm(x, g, b):
    mu = jnp.mean(x, axis=-1, keepdims=True)
    xc = x - mu
    var = jnp.mean(xc * xc, axis=-1, keepdims=True)
    return xc * lax.rsqrt(var + LN_EPS) * g + b


def _sigmoid(x):
    return 1.0 / (1.0 + jnp.exp(-x))


def _gelu(x):
    c1 = -2.0 * math.sqrt(2.0 / math.pi)
    return x / (1.0 + jnp.exp(x * (c1 + (c1 * 0.044715) * (x * x))))


def _row_tile(t, target):
    best = None
    for cand in range(SUBLANES, min(t, target) + 1, SUBLANES):
        if t % cand == 0:
            best = cand
    return best if best is not None else t


def _inproj_kernel(x_ref, g_ref, b_ref, w_ref, bi_ref, x0_ref, proj_ref, *, apply_ln):
    x = x_ref[...]
    if apply_ln:
        x = _layer_norm(x, g_ref[...], b_ref[...])
    x0_ref[...] = x
    proj_ref[...] = jnp.dot(x.astype(BF16), w_ref[...], preferred_element_type=F32) + bi_ref[...]


def _inproj(x, g, b, w_bf16, bias, *, apply_ln):
    t, d = x.shape
    d_in = w_bf16.shape[1]
    tm = _row_tile(t, 512)
    return pl.pallas_call(
        functools.partial(_inproj_kernel, apply_ln=apply_ln),
        grid=(t // tm,),
        in_specs=[
            pl.BlockSpec((tm, d), lambda i: (i, 0)),
            pl.BlockSpec((1, d), lambda i: (0, 0)),
            pl.BlockSpec((1, d), lambda i: (0, 0)),
            pl.BlockSpec((d, d_in), lambda i: (0, 0)),
            pl.BlockSpec((1, d_in), lambda i: (0, 0)),
        ],
        out_specs=[
            pl.BlockSpec((tm, d), lambda i: (i, 0)),
            pl.BlockSpec((tm, d_in), lambda i: (i, 0)),
        ],
        out_shape=[jax.ShapeDtypeStruct((t, d), F32), jax.ShapeDtypeStruct((t, d_in), F32)],
        compiler_params=_cparams("parallel"),
        name="inproj",
    )(x, g.reshape(1, d), b.reshape(1, d), w_bf16, bias.reshape(1, d_in))


def _hgrn_kernel(q_ref, f_ref, v_ref, lb_ref, s0_ref, o_ref, s_ref, st_ref,
                 *, chunk, n_iters, per_iter, seqs, valid):
    lb = lb_ref[...]
    ones = jnp.ones((LANES, LANES), BF16)
    groups = seqs * per_iter
    shape = (groups, chunk, LANES)
    row = lax.broadcasted_iota(jnp.int32, shape, 1)
    span = per_iter * chunk
    nt = (((1,), (1,)), ((), ()))
    tn = (((0,), (0,)), ((), ()))

    for b in range(seqs):
        st_ref[b] = s0_ref[b].T

    def take(ref, r0):
        return jnp.concatenate([ref[b, pl.ds(r0, span), :].reshape(per_iter, chunk, LANES)
                                for b in range(seqs)], axis=0)

    def step(i, carry):
        r0 = pl.multiple_of(i * span, span)
        q = take(q_ref, r0)
        fp = take(f_ref, r0)
        v = take(v_ref, r0)
        f = lb + (1.0 - lb) * _sigmoid(fp)
        logf = jnp.log(f)
        kk = (1.0 - lb) * _sigmoid(-fp)
        if valid < chunk:
            live = row < valid
            logf = jnp.where(live, logf, 0.0)
            kk = jnp.where(live, kk, 0.0)
            q = jnp.where(live, q, 0.0)
            v = jnp.where(live, v, 0.0)
        pick = lambda a, s: jnp.broadcast_to(a[:, s:s + 1, :], shape)
        bc = jnp.zeros(shape, F32)
        for s in range(chunk):
            bc = bc + jnp.where(row >= s, pick(logf, s), 0.0)
        split = chunk // 2 if chunk % (2 * SUBLANES) == 0 else 0
        lows = [split if s >= split else 0 for s in range(chunk)]
        prods = []
        for s, lo in zip(range(chunk), lows):
            part = (groups, chunk - lo, LANES)
            near = lambda a: jnp.broadcast_to(a[:, s:s + 1, :], part)
            e = jnp.where(row[:, lo:, :] >= s, jnp.exp(bc[:, lo:, :] - near(bc)), 0.0)
            prods.append((q[:, lo:, :] * e * near(kk)).reshape(groups * (chunk - lo), LANES))
        p_all = jnp.concatenate(prods, axis=0).astype(BF16)
        att = jnp.dot(p_all, ones, preferred_element_type=F32)
        o = jnp.zeros(shape, F32)
        o_low = jnp.zeros((groups, chunk - split, LANES), F32)
        off = 0
        for s, lo in zip(range(chunk), lows):
            part = (groups, chunk - lo, LANES)
            n = groups * (chunk - lo)
            term = att[off:off + n, :].reshape(part) * jnp.broadcast_to(v[:, s:s + 1, :], part)
            off += n
            if lo == 0:
                o = o + term
            else:
                o_low = o_low + term
        if split:
            o = o + jnp.concatenate([jnp.zeros((groups, split, LANES), F32), o_low], axis=1)
        qs = (q * jnp.exp(bc)).astype(BF16)
        kt = (kk * jnp.exp(pick(bc, chunk - 1) - bc)).astype(BF16)
        vb = v.astype(BF16)
        decay = jnp.exp(bc[:, chunk - 1:chunk, :])
        ds = [lax.dot_general(vb[g], kt[g], tn, preferred_element_type=F32) for g in range(groups)]
        outs = []
        for b in range(seqs):
            st = st_ref[b]
            before = []
            for n in range(per_iter):
                g = b * per_iter + n
                before.append(st.astype(BF16))
                st = st * decay[g] + ds[g]
            st_ref[b] = st
            for n in range(per_iter):
                g = b * per_iter + n
                outs.append(o[g] + lax.dot_general(qs[g], before[n], nt, preferred_element_type=F32))
        for b in range(seqs):
            ob = jnp.concatenate(outs[b * per_iter:(b + 1) * per_iter], axis=0)
            ob = ob * lax.rsqrt(jnp.mean(ob * ob, axis=-1, keepdims=True) + RMS_EPS)
            o_ref[b, pl.ds(r0, span), :] = ob
        return carry

    lax.fori_loop(0, n_iters, step, 0)
    for b in range(seqs):
        s_ref[b] = st_ref[b].T


def _hgrn(proj3, lb, s0, *, valid):
    bsz, length, _ = proj3.shape
    heads = s0.shape[1]
    chunk = HG_CHUNK if length % HG_CHUNK == 0 else length
    n_chunks = length // chunk
    divisor = lambda n, cap: max(c for c in range(1, cap + 1) if n % c == 0)
    if n_chunks > 1:
        seqs, per_iter = 1, divisor(n_chunks, HG_CHUNKS_PER_ITER)
    else:
        seqs, per_iter = divisor(bsz, HG_SEQS_PER_STEP), 1
    seq_spec = lambda off: pl.BlockSpec((seqs, length, LANES), lambda b, h: (b, 0, off + h))
    state_spec = pl.BlockSpec((seqs, None, LANES, LANES), lambda b, h: (b, h, 0, 0))
    return pl.pallas_call(
        functools.partial(_hgrn_kernel, chunk=chunk, n_iters=n_chunks // per_iter, per_iter=per_iter,
                          seqs=seqs, valid=min(valid, chunk)),
        grid=(bsz // seqs, heads),
        in_specs=[
            seq_spec(0), seq_spec(heads), seq_spec(2 * heads),
            pl.BlockSpec((1, LANES), lambda b, h: (0, h)),
            state_spec,
        ],
        out_specs=[
            pl.BlockSpec((seqs, length, LANES), lambda b, h: (b, 0, h)),
            state_spec,
        ],
        out_shape=[jax.ShapeDtypeStruct((bsz, length, heads * LANES), F32),
                   jax.ShapeDtypeStruct(s0.shape, F32)],
        scratch_shapes=[pltpu.VMEM((seqs, LANES, LANES), F32)],
        compiler_params=_cparams("parallel", "parallel"),
        name="hgrn2",
    )(proj3, proj3, proj3, lb, s0)


def _s5_kernel(u_ref, bb_ref, c_ref, are_ref, aim_ref, alre_ref, alim_ref, d_ref, h0re_ref, h0im_ref,
               y_ref, hre_ref, him_ref, hpre_ref, hpim_ref, *, lc, n_chunks, seqs):
    rows = seqs * n_chunks
    half = are_ref.shape[-1]
    a_re = are_ref[...]
    a_im = aim_ref[...]
    bb = bb_ref[...]

    def drive(t):
        u_t = u_ref[pl.ds(t, rows, stride=lc), :]
        return u_t, jnp.dot(u_t.astype(BF16), bb, preferred_element_type=F32)

    def advance(h_re, h_im, bu):
        return (a_re * h_re - a_im * h_im + bu[:, :half], a_re * h_im + a_im * h_re + bu[:, half:])

    h_re = jnp.zeros((rows, half), F32)
    h_im = jnp.zeros((rows, half), F32)
    for t in range(lc):
        h_re, h_im = advance(h_re, h_im, drive(t)[1])

    n_tiles = half // LANES

    def put(ref, idx, val):
        for k in range(n_tiles):
            ref[k, idx, :] = val[:, k * LANES:(k + 1) * LANES]

    def get(ref, idx):
        return jnp.concatenate([ref[k, idx, :] for k in range(n_tiles)], axis=-1)

    put(hpre_ref, slice(None), h_re)
    put(hpim_ref, slice(None), h_im)

    al_re = alre_ref[...]
    al_im = alim_ref[...]

    def carry_step(c, carry):
        c_re, c_im = carry
        idx = pl.ds(c, seqs, stride=n_chunks)
        l_re = get(hpre_ref, idx)
        l_im = get(hpim_ref, idx)
        put(hpre_ref, idx, c_re)
        put(hpim_ref, idx, c_im)
        return (al_re * c_re - al_im * c_im + l_re, al_re * c_im + al_im * c_re + l_im)

    c_re, c_im = lax.fori_loop(0, n_chunks, carry_step, (h0re_ref[...], h0im_ref[...]))
    hre_ref[...] = c_re
    him_ref[...] = c_im

    h_re = get(hpre_ref, slice(None))
    h_im = get(hpim_ref, slice(None))
    c_w = c_ref[...]
    d_vec = d_ref[...]
    for t in range(lc):
        u_t, bu = drive(t)
        h_re, h_im = advance(h_re, h_im, bu)
        y = (jnp.dot(h_re.astype(BF16), c_w[:half], preferred_element_type=F32)
             + jnp.dot(h_im.astype(BF16), c_w[half:], preferred_element_type=F32) + d_vec * u_t)
        y_ref[pl.ds(t, rows, stride=lc), :] = y


def _s5_params(a_re, a_im, log_dt, b_re, b_im, c_re, c_im, d, lc):
    groups, pstate = a_re.shape
    dt = jnp.exp(log_dt.astype(F32))[:, None]
    a_re = a_re.astype(F32)
    a_im = a_im.astype(F32)
    zr, zi = a_re * dt, a_im * dt
    mag = jnp.exp(zr)
    ab_re, ab_im = mag * jnp.cos(zi), mag * jnp.sin(zi)
    den = a_re * a_re + a_im * a_im
    nr = ab_re - 1.0
    coef_re = (nr * a_re + ab_im * a_im) / den
    coef_im = (ab_im * a_re - nr * a_im) / den
    b_re = b_re.astype(F32)
    b_im = b_im.astype(F32)
    bb_re = coef_re[..., None] * b_re - coef_im[..., None] * b_im
    bb_im = coef_re[..., None] * b_im + coef_im[..., None] * b_re
    pm = jnp.exp(zr * lc)
    al_re, al_im = pm * jnp.cos(zi * lc), pm * jnp.sin(zi * lc)
    gt = S5_GROUP_TILE
    tiles = groups // gt
    swap = lambda x: jnp.transpose(x, (0, 2, 1))
    bb = jnp.concatenate([_block_diag(swap(bb_re), gt), _block_diag(swap(bb_im), gt)], axis=-1)
    cw = jnp.concatenate([_block_diag(swap(c_re.astype(F32)), gt), -_block_diag(swap(c_im.astype(F32)), gt)], axis=1)
    lane = lambda x: x.reshape(tiles, 1, gt * pstate)
    return dict(bb=bb.astype(BF16), cw=cw.astype(BF16), are=lane(ab_re), aim=lane(ab_im),
                alre=lane(al_re), alim=lane(al_im), d=d.astype(F32).reshape(1, groups * SSM_GROUP))


def _block_diag(x, gm):
    if gm == 1:
        return x
    g, a, b = x.shape
    x = x.reshape(g // gm, gm, a, b)
    eye = jnp.eye(gm, dtype=x.dtype)
    return (x[:, :, :, None, :] * eye[None, :, None, :, None]).reshape(g // gm, gm * a, gm * b)


def _s5(proj, h0_re, h0_im, prm, *, length, lc, u_col):
    bsz, n_groups, pstate = h0_re.shape
    d_ssm = n_groups * SSM_GROUP
    tiles = n_groups // S5_GROUP_TILE
    half = S5_GROUP_TILE * pstate
    n_chunks = length // lc
    seqs = bsz if n_chunks == 1 else min(bsz, S5_SEQS_PER_STEP)
    steps = bsz // seqs
    rows = seqs * n_chunks
    h0r = h0_re.reshape(steps, seqs, n_groups * pstate)
    h0i = h0_im.reshape(steps, seqs, n_groups * pstate)
    tspec = lambda a, b: pl.BlockSpec((None, a, b), lambda i, g: (g, 0, 0))
    sspec = pl.BlockSpec((None, seqs, half), lambda i, g: (i, 0, g))
    y, hre, him = pl.pallas_call(
        functools.partial(_s5_kernel, lc=lc, n_chunks=n_chunks, seqs=seqs),
        grid=(steps, tiles),
        in_specs=[
            pl.BlockSpec((seqs * length, LANES), lambda i, g: (i, u_col // LANES + g)),
            tspec(LANES, 2 * half), tspec(2 * half, LANES),
            tspec(1, half), tspec(1, half), tspec(1, half), tspec(1, half),
            pl.BlockSpec((1, LANES), lambda i, g: (0, g)),
            sspec, sspec,
        ],
        out_specs=[
            pl.BlockSpec((seqs * length, LANES), lambda i, g: (i, g)),
            sspec, sspec,
        ],
        out_shape=[jax.ShapeDtypeStruct((bsz * length, d_ssm), F32),
                   jax.ShapeDtypeStruct(h0r.shape, F32),
                   jax.ShapeDtypeStruct(h0r.shape, F32)],
        scratch_shapes=[pltpu.VMEM((half // LANES, rows, LANES), F32),
                        pltpu.VMEM((half // LANES, rows, LANES), F32)],
        compiler_params=_cparams("parallel", "parallel"),
        name="s5",
    )(proj, prm['bb'], prm['cw'], prm['are'], prm['aim'], prm['alre'], prm['alim'], prm['d'], h0r, h0i)
    return y, hre.reshape(h0_re.shape), him.reshape(h0_re.shape)


def _mix_kernel(x_ref, o_ref, gate_ref, y_ref, hg_g_ref, wglu_ref, bglu_ref, wo_hg_ref, wo_ssm_ref,
                g_ref, b_ref, out_ref, *, alpha):
    gate = gate_ref[...]
    o_hg = o_ref[...] * hg_g_ref[...] * (gate * _sigmoid(gate))
    z = _gelu(y_ref[...])
    glu = jnp.dot(z.astype(BF16), wglu_ref[...], preferred_element_type=F32) + bglu_ref[...]
    o_ssm = z * _sigmoid(glu)
    mix = (jnp.dot(o_hg.astype(BF16), wo_hg_ref[...], preferred_element_type=F32)
           + jnp.dot(o_ssm.astype(BF16), wo_ssm_ref[...], preferred_element_type=F32))
    out_ref[...] = _layer_norm(alpha * x_ref[...] + mix, g_ref[...], b_ref[...])


def _mix(x0, o_hg, proj, y_ssm, hg_g, wglu, bglu, wo_hg, wo_ssm, g, b, *, alpha):
    t, d = x0.shape
    d_hg = o_hg.shape[1]
    d_ssm = y_ssm.shape[1]
    tm = _row_tile(t, 512)
    gate_block = (3 * d_hg) // d_hg
    row = lambda w: pl.BlockSpec((tm, w), lambda i: (i, 0))
    full = lambda a, bb: pl.BlockSpec((a, bb), lambda i: (0, 0))
    return pl.pallas_call(
        functools.partial(_mix_kernel, alpha=alpha),
        grid=(t // tm,),
        in_specs=[
            row(d), row(d_hg),
            pl.BlockSpec((tm, d_hg), lambda i: (i, gate_block)),
            row(d_ssm),
            full(1, d_hg), full(d_ssm, d_ssm), full(1, d_ssm), full(d_hg, d), full(d_ssm, d),
            full(1, d), full(1, d),
        ],
        out_specs=row(d),
        out_shape=jax.ShapeDtypeStruct((t, d), F32),
        compiler_params=_cparams("parallel"),
        name="mix",
    )(x0, o_hg, proj, y_ssm, hg_g.reshape(1, d_hg), wglu, bglu.reshape(1, d_ssm), wo_hg, wo_ssm,
      g.reshape(1, d), b.reshape(1, d))


def _top_rows(s, k):
    n, t = s.shape
    iota = lax.broadcasted_iota(jnp.int32, (n, t), 0).astype(F32)
    krow = lax.broadcasted_iota(jnp.int32, (k, t), 0)
    rank = jnp.full((n, t), float(k), F32)
    vals = jnp.zeros((k, t), F32)
    for a in range(k):
        m = jnp.max(s, axis=0, keepdims=True)
        idx = jnp.min(jnp.where(s == m, iota, float(n)), axis=0, keepdims=True)
        hit = iota == idx
        rank = jnp.where(hit, float(a), rank)
        s = jnp.where(hit, -jnp.inf, s)
        vals = jnp.where(krow == a, jnp.broadcast_to(m, (k, t)), vals)
    return vals, rank


def _candidate_pieces(t1, t2):
    k = PEER_TOPK
    t = t1.shape[1]
    bc = lambda r, n: jnp.broadcast_to(r, (n, t))
    pieces = [(bc(t1[0:1], k), t2, [b for b in range(k)], [True] * k)]
    half = k // 2
    for a in range(1, half):
        nb = k // (a + 1)
        pieces.append((bc(t1[a:a + 1], half), t2[0:half], [a * k + b for b in range(half)],
                       [b < nb for b in range(half)]))
    pieces.append((t1[half:k], bc(t2[0:1], half), [(half + r) * k for r in range(half)], [True] * half))
    return pieces


def _candidate_index_rows(t):
    k = PEER_TOPK
    dummy = jnp.zeros((k, 1), F32)
    vals = [i if ok else k * k for _, _, idx, val in _candidate_pieces(dummy, dummy) for i, ok in zip(idx, val)]
    return jnp.broadcast_to(jnp.asarray(vals, F32)[:, None], (len(vals), t))


def _peer_select(s1, s2, cidx):
    k = PEER_TOPK
    t = s1.shape[1]
    big = float(k * k)
    top1, rank1 = _top_rows(s1, k)
    top2, rank2 = _top_rows(s2, k)
    pieces = _candidate_pieces(top1, top2)
    cand = jnp.concatenate([x + y for x, y, _, _ in pieces], axis=0)
    cand = jnp.where(cidx < big, cand, -jnp.inf)
    chosen = jnp.zeros(cand.shape, F32)
    for _ in range(k):
        m = jnp.max(cand, axis=0, keepdims=True)
        idx = jnp.min(jnp.where(cand == m, cidx, big), axis=0, keepdims=True)
        hit = cidx == idx
        chosen = jnp.where(hit, 1.0, chosen)
        cand = jnp.where(hit, -jnp.inf, cand)
    e1 = jnp.exp(top1 - jnp.broadcast_to(top1[0:1], (k, t)))
    e2 = jnp.exp(top2 - jnp.broadcast_to(top2[0:1], (k, t)))
    ecand = jnp.concatenate([x * y for x, y, _, _ in _candidate_pieces(e1, e2)], axis=0)
    z = jnp.sum(chosen * ecand, axis=0, keepdims=True)
    counts = []
    off = 0
    half = k // 2
    for pi, (x, _, _, _) in enumerate(pieces):
        rows = x.shape[0]
        blk = chosen[off:off + rows]
        off += rows
        if pi < len(pieces) - 1:
            counts.append(jnp.sum(blk, axis=0, keepdims=True))
        else:
            counts.extend(blk[r:r + 1] for r in range(half))
    nk = s1.shape[0]
    c1 = jnp.zeros((nk, t), F32)
    for a in range(k):
        c1 = jnp.where(rank1 == float(a), jnp.broadcast_to(counts[a], (nk, t)), c1)
    phi = jnp.exp(s1 - jnp.broadcast_to(top1[0:1], (nk, t))) * jnp.broadcast_to(1.0 / z, (nk, t))
    psi = jnp.exp(s2 - jnp.broadcast_to(top2[0:1], (nk, t)))
    return c1, phi, rank2, psi


def _sort_network(n):
    out, p = [], 1
    while p < n:
        k = p
        while k >= 1:
            for j in range(k % p, n - k, 2 * k):
                for i in range(min(k, n - j - k)):
                    if (i + j) // (2 * p) == (i + j + k) // (2 * p):
                        out.append((i + j, i + j + k))
            k //= 2
        p *= 2
    return out


def _sorted_top(s):
    n = s.shape[0] // SUBLANES
    x = [s[i * SUBLANES:(i + 1) * SUBLANES, :] for i in range(n)]
    for i, j in _sort_network(n):
        x[i], x[j] = jnp.maximum(x[i], x[j]), jnp.minimum(x[i], x[j])
    shift = SUBLANES // 2
    while shift >= 1:
        r = [pltpu.roll(v, shift, axis=0) for v in x]
        x = [jnp.maximum(x[i], r[n - 1 - i]) for i in range(n)]
        d = n // 2
        while d >= 1:
            for i in range(n):
                if (i & d) == 0:
                    x[i], x[i + d] = jnp.maximum(x[i], x[i + d]), jnp.minimum(x[i], x[i + d])
            d //= 2
        shift //= 2
    return x


def _peer_select_fast(s1, s2, cidx):
    k = PEER_TOPK
    nk, t = s1.shape
    nblk = nk // SUBLANES
    big = float(k * k)
    l1 = _sorted_top(s1)
    l2 = _sorted_top(s2)
    sub = lax.broadcasted_iota(jnp.int32, (SUBLANES, t), 0)

    def stack(lst):
        halves = []
        for base in (0, SUBLANES):
            blk = lst[base]
            for a in range(1, SUBLANES):
                blk = jnp.where(sub == a, lst[base + a], blk)
            halves.append(blk)
        return jnp.concatenate(halves, axis=0)

    top1 = stack(l1)
    top2 = stack(l2)
    pieces = _candidate_pieces(top1, top2)
    cand = jnp.concatenate([x + y for x, y, _, _ in pieces], axis=0)
    cand = jnp.where(cidx < big, cand, -jnp.inf)
    chosen = jnp.zeros(cand.shape, F32)
    for _ in range(k):
        hit = cand == jnp.max(cand, axis=0, keepdims=True)
        chosen = jnp.where(hit, 1.0, chosen)
        cand = jnp.where(hit, -jnp.inf, cand)
    e1 = jnp.exp(top1 - jnp.broadcast_to(top1[0:1], (k, t)))
    e2 = jnp.exp(top2 - jnp.broadcast_to(top2[0:1], (k, t)))
    ecand = jnp.concatenate([x * y for x, y, _, _ in _candidate_pieces(e1, e2)], axis=0)
    z = jnp.sum(chosen * ecand, axis=0, keepdims=True)
    counts = []
    off = 0
    half = k // 2
    for pi, (x, _, _, _) in enumerate(pieces):
        rows = x.shape[0]
        blk = chosen[off:off + rows]
        off += rows
        if pi < len(pieces) - 1:
            counts.append(jnp.sum(blk, axis=0, keepdims=True))
        else:
            counts.extend(blk[r:r + 1] for r in range(half))
    c1_blocks, r2_blocks = [], []
    cnt1 = jnp.zeros((SUBLANES, t), F32)
    cnt2 = jnp.zeros((SUBLANES, t), F32)
    for i in range(nblk):
        x1 = s1[i * SUBLANES:(i + 1) * SUBLANES, :]
        x2 = s2[i * SUBLANES:(i + 1) * SUBLANES, :]
        c1b = jnp.zeros((SUBLANES, t), F32)
        r2b = jnp.zeros((SUBLANES, t), F32)
        for a in range(k):
            c1b = jnp.where(x1 == l1[a], jnp.broadcast_to(counts[a], (SUBLANES, t)), c1b)
            r2b = jnp.where(x2 < l2[a], float(a + 1), r2b)
        c1_blocks.append(c1b)
        r2_blocks.append(r2b)
        cnt1 = cnt1 + jnp.where(x1 >= l1[k - 1], 1.0, 0.0)
        cnt2 = cnt2 + jnp.where(x2 >= l2[k - 1], 1.0, 0.0)
    c1 = jnp.concatenate(c1_blocks, axis=0)
    r2 = jnp.concatenate(r2_blocks, axis=0)
    phi = jnp.exp(s1 - jnp.broadcast_to(top1[0:1], (nk, t))) * jnp.broadcast_to(1.0 / z, (nk, t))
    psi = jnp.exp(s2 - jnp.broadcast_to(top2[0:1], (nk, t)))
    bad = jnp.zeros((SUBLANES, t), F32)
    for a in range(k - 1):
        bad = jnp.where(l1[a] <= l1[a + 1], 1.0, bad)
        bad = jnp.where(l2[a] <= l2[a + 1], 1.0, bad)
    fk = float(k)
    bad = jnp.max(bad, axis=0, keepdims=True)
    bad = jnp.where(jnp.sum(cnt1, axis=0, keepdims=True) != fk, 1.0, bad)
    bad = jnp.where(jnp.sum(cnt2, axis=0, keepdims=True) != fk, 1.0, bad)
    bad = jnp.where(jnp.sum(chosen, axis=0, keepdims=True) != fk, 1.0, bad)
    return c1, phi, r2, psi, bad


def _rows_bf16(row, n):
    pack = 2 * SUBLANES
    tile = jnp.broadcast_to(row, (pack, row.shape[1])).astype(BF16)
    return jnp.concatenate([tile] * (n // pack), axis=0)


def _peer_kernel(x_ref, wq_ref, keys_ref, cidx_ref, ua_ref, ub_ref, vtp_ref, vtc_ref, vtl_ref, g_ref, b_ref,
                 out_ref, xb_ref, qt_ref, c1_ref, phi_ref, r2_ref, psi_ref, uta_ref, utb_ref, wa_ref, wb_ref,
                 acc_ref,
                 *, alpha, heads, nk, rows_per_step):
    j = pl.program_id(1)
    nt = (((1,), (1,)), ((), ()))

    @pl.when(j == 0)
    def _():
        xb = x_ref[...].astype(BF16)
        xb_ref[...] = xb
        acc_ref[...] = jnp.zeros_like(acc_ref)
        wb_ref[...] = jnp.zeros_like(wb_ref)
        qt_ref[...] = lax.dot_general(wq_ref[...], xb, nt, preferred_element_type=F32).astype(BF16)

        def select_head(h, carry):
            sc = []
            for c in range(2):
                hc = h * 2 + c
                qt = qt_ref[pl.ds(pl.multiple_of(hc * LANES, LANES), LANES), :]
                sc.append(jnp.dot(keys_ref[hc], qt, preferred_element_type=F32))

            def put(c1, phi, r2, psi):
                c1_ref[h] = c1
                phi_ref[h] = phi
                r2_ref[h] = r2.astype(BF16)
                psi_ref[h] = psi.astype(BF16)

            if nk // SUBLANES == PEER_TOPK:
                *sel, bad = _peer_select_fast(sc[0], sc[1], cidx_ref[...])
                put(*sel)

                @pl.when(jnp.max(bad) > 0.0)
                def _():
                    put(*_peer_select(sc[0], sc[1], cidx_ref[...]))
            else:
                put(*_peer_select(sc[0], sc[1], cidx_ref[...]))
            return carry

        lax.fori_loop(0, heads, select_head, 0)
        uta_ref[...] = lax.dot_general(ua_ref[...], xb, nt, preferred_element_type=F32)

    def mix_block(block, ut_ref, w_ref):
        for r in range(rows_per_step):
            n1 = block * rows_per_step + r
            gsum = None
            for h in range(heads):
                c1row = _rows_bf16(c1_ref[h, pl.ds(n1, 1), :], nk)
                phirow = _rows_bf16(phi_ref[h, pl.ds(n1, 1), :], nk)
                term = jnp.where(r2_ref[h] < c1row, phirow * psi_ref[h], jnp.zeros((), BF16))
                gsum = term if gsum is None else gsum + term
            act = _gelu(ut_ref[r * nk:(r + 1) * nk, :]).astype(BF16)
            w_ref[r * nk:(r + 1) * nk, :] = gsum * act

    @pl.when(j > 0)
    def _():
        first = 2 * j - 2
        d, tm = acc_ref.shape
        pack = 2 * SUBLANES
        acc_new = acc_ref[...] + jnp.dot(vtp_ref[...], wb_ref[...], preferred_element_type=F32)
        acc_ref[...] = acc_new
        width = min(LANES, tm)
        done = acc_new[d - pack:d, tm - width:tm] > -jnp.inf
        tile = xb_ref[0:pack, 0:width]
        xb_ref[0:pack, 0:width] = jnp.where(done, tile, -tile)
        xb = xb_ref[...]
        utb_ref[...] = lax.dot_general(ua_ref[...], xb, nt, preferred_element_type=F32)
        mix_block(first, uta_ref, wa_ref)
        uta_ref[...] = lax.dot_general(ub_ref[...], xb, nt, preferred_element_type=F32)
        mix_block(first + 1, utb_ref, wb_ref)
        acc_ref[...] += jnp.dot(vtc_ref[...], wa_ref[...], preferred_element_type=F32)

    @pl.when(j == pl.num_programs(1) - 1)
    def _():
        ffn = (acc_ref[...] + jnp.dot(vtl_ref[...], wb_ref[...], preferred_element_type=F32)).T
        out_ref[...] = _layer_norm(alpha * x_ref[...] + ffn, g_ref[...], b_ref[...])


def _peer(x, wq_t, keys, u_tab, v_tab_t, g, b, *, alpha):
    t, d = x.shape
    heads, _, nk, dh = keys.shape
    n_exp = u_tab.shape[0]
    tm = 512 if t % 512 == 0 else _row_tile(t, 512)
    rows = PEER_EXPERT_ROWS
    eb = rows * nk
    keys2 = keys.reshape(heads * 2, nk, dh)
    cidx = _candidate_index_rows(tm)
    n_blocks = n_exp // eb
    return pl.pallas_call(
        functools.partial(_peer_kernel, alpha=alpha, heads=heads, nk=nk, rows_per_step=rows),
        grid=(t // tm, n_blocks // 2 + 1),
        in_specs=[
            pl.BlockSpec((tm, d), lambda i, j: (i, 0)),
            pl.BlockSpec(wq_t.shape, lambda i, j: (0, 0)),
            pl.BlockSpec(keys2.shape, lambda i, j: (0, 0, 0)),
            pl.BlockSpec(cidx.shape, lambda i, j: (0, 0)),
            pl.BlockSpec((eb, d), lambda i, j: (jnp.maximum(2 * j - 1, 0), 0)),
            pl.BlockSpec((eb, d), lambda i, j: (jnp.minimum(2 * j, n_blocks - 1), 0)),
            pl.BlockSpec((d, eb), lambda i, j: (0, jnp.maximum(2 * j - 3, 0))),
            pl.BlockSpec((d, eb), lambda i, j: (0, jnp.maximum(2 * j - 2, 0))),
            pl.BlockSpec((d, eb), lambda i, j: (0, n_blocks - 1)),
            pl.BlockSpec((1, d), lambda i, j: (0, 0)),
            pl.BlockSpec((1, d), lambda i, j: (0, 0)),
        ],
        out_specs=pl.BlockSpec((tm, d), lambda i, j: (i, 0)),
        out_shape=jax.ShapeDtypeStruct((t, d), F32),
        scratch_shapes=[
            pltpu.VMEM((tm, d), BF16),
            pltpu.VMEM((wq_t.shape[0], tm), BF16),
            pltpu.VMEM((heads, nk, tm), F32), pltpu.VMEM((heads, nk, tm), F32),
            pltpu.VMEM((heads, nk, tm), BF16), pltpu.VMEM((heads, nk, tm), BF16),
            pltpu.VMEM((eb, tm), F32), pltpu.VMEM((eb, tm), F32),
            pltpu.VMEM((eb, tm), BF16), pltpu.VMEM((eb, tm), BF16),
            pltpu.VMEM((d, tm), F32),
        ],
        compiler_params=_cparams("parallel", "arbitrary"),
        name="peer",
    )(x, wq_t, keys2, cidx, u_tab, u_tab, v_tab_t, v_tab_t, v_tab_t, g.reshape(1, d), b.reshape(1, d))


def _pad_tokens(a, length):
    return jnp.pad(a, ((0, 0), (0, length - a.shape[1]), (0, 0)))


def _sequence_mixers(proj3, lb, s_hg, s_re, s_im, s5_prm, d_hg):
    bsz, length, _ = proj3.shape
    lpad = -(-length // SUBLANES) * SUBLANES
    o, s_hg_new = _hgrn(_pad_tokens(proj3, lpad) if lpad != length else proj3, lb, s_hg, valid=length)
    lc = S5_CHUNK if length % S5_CHUNK == 0 else length
    y, s_re_new, s_im_new = _s5(proj3.reshape(bsz * length, -1), s_re, s_im, s5_prm[lc],
                                length=length, lc=lc, u_col=4 * d_hg)
    return o[:, :length], y.reshape(bsz, length, -1), s_hg_new, s_re_new, s_im_new


def kernel(x_prompt, x_sample, state_hgrn, state_ssm_re, state_ssm_im, meta_tokens, ln_emb_g, ln_emb_b,
           lb_logits, w_in, b_in, hg_norm_g, ssm_a_re, ssm_a_im, ssm_log_dt, ssm_b_re, ssm_b_im,
           ssm_c_re, ssm_c_im, ssm_d, w_glu, b_glu, w_out, ln1_g, ln1_b, peer_w_q, peer_keys,
           peer_u, peer_v, ln2_g, ln2_b):
    depth = w_in.shape[0]
    alpha = (2.0 * depth) ** 0.25
    bp, seq, d = x_prompt.shape
    bs, dseq, _ = x_sample.shape
    n_meta = meta_tokens.shape[0]
    heads = state_hgrn.shape[2]
    d_hg = heads * LANES
    groups, pstate = state_ssm_re.shape[2], state_ssm_re.shape[3]
    lbs = jnp.cumsum(jax.nn.softmax(lb_logits.astype(F32), axis=0), axis=0)

    xp = x_prompt.astype(F32).reshape(bp * seq, d)
    xm = meta_tokens.astype(F32)
    xs = x_sample.astype(F32).reshape(bs * dseq, d)
    n_s = bs * dseq
    hg_p, re_p, im_p, hg_s, re_s, im_s = [], [], [], [], [], []
    for l in range(depth):
        last = l == depth - 1
        w_in_b = w_in[l].astype(BF16)
        lb = lbs[l].reshape(1, d_hg)
        s5_prm = {lc: _s5_params(ssm_a_re[l], ssm_a_im[l], ssm_log_dt[l], ssm_b_re[l], ssm_b_im[l],
                                 ssm_c_re[l], ssm_c_im[l], ssm_d[l], lc)
                  for lc in {S5_CHUNK if n % S5_CHUNK == 0 else n for n in (n_meta, seq, dseq)}}
        wglu = w_glu[l].astype(BF16)
        wo_hg = w_out[l, :d_hg].astype(BF16)
        wo_ssm = w_out[l, d_hg:].astype(BF16)
        wq_t = peer_w_q[l].T.astype(BF16)
        keys = peer_keys[l].astype(BF16)
        u_tab = peer_u[l].astype(BF16)
        v_tab_t = peer_v[l].T.astype(BF16)

        xsm = jnp.concatenate([xs, xm], axis=0)
        x0_p, proj_p = _inproj(xp, ln_emb_g, ln_emb_b, w_in_b, b_in[l], apply_ln=(l == 0))
        x0_sm, proj_sm = _inproj(xsm, ln_emb_g, ln_emb_b, w_in_b, b_in[l], apply_ln=(l == 0))
        d_in = proj_p.shape[1]

        proj_m = proj_sm[n_s:].reshape(1, n_meta, d_in)
        proj_m8 = jnp.broadcast_to(proj_m, (SUBLANES, n_meta, d_in))
        zero_hg = jnp.zeros((SUBLANES, heads, LANES, LANES), F32)
        zero_ss = jnp.zeros((SUBLANES, groups, pstate), F32)
        o_m, y_m, hg_m, re_m, im_m = _sequence_mixers(proj_m8, lb, zero_hg, zero_ss, zero_ss, s5_prm, d_hg)

        o_p, y_p, shg, sre, sim = _sequence_mixers(
            proj_p.reshape(bp, seq, d_in), lb,
            jnp.broadcast_to(hg_m[:1], (bp,) + hg_m.shape[1:]),
            jnp.broadcast_to(re_m[:1], (bp,) + re_m.shape[1:]),
            jnp.broadcast_to(im_m[:1], (bp,) + im_m.shape[1:]), s5_prm, d_hg)
        hg_p.append(shg)
        re_p.append(sre)
        im_p.append(sim)

        o_s, y_s, shg, sre, sim = _sequence_mixers(
            proj_sm[:n_s].reshape(bs, dseq, d_in), lb, state_hgrn[l].astype(F32),
            state_ssm_re[l].astype(F32), state_ssm_im[l].astype(F32), s5_prm, d_hg)
        hg_s.append(shg)
        re_s.append(sre)
        im_s.append(sim)

        mix_args = (hg_norm_g[l], wglu, b_glu[l], wo_hg, wo_ssm, ln1_g[l], ln1_b[l])
        peer_args = (wq_t, keys, u_tab, v_tab_t, ln2_g[l], ln2_b[l])
        x1_p = _mix(x0_p, o_p.reshape(bp * seq, d_hg), proj_p, y_p.reshape(bp * seq, -1), *mix_args, alpha=alpha)
        xp = _peer(x1_p, *peer_args, alpha=alpha)
        x1_s = _mix(x0_sm[:n_s], o_s.reshape(n_s, d_hg), proj_sm[:n_s], y_s.reshape(n_s, -1), *mix_args, alpha=alpha)
        xs = _peer(x1_s, *peer_args, alpha=alpha)
        if not last:
            x1_m = _mix(jnp.broadcast_to(x0_sm[n_s:], (n_meta, d)), o_m[0], proj_sm[n_s:], y_m[0], *mix_args, alpha=alpha)
            xm = _peer(x1_m, *peer_args, alpha=alpha)

    y_prompt = xp.reshape(bp, seq, d).astype(x_prompt.dtype)
    y_sample = xs.reshape(bs, dseq, d).astype(x_sample.dtype)
    return (y_prompt, y_sample, jnp.stack(hg_p), jnp.stack(re_p), jnp.stack(im_p),
            jnp.stack(hg_s), jnp.stack(re_s), jnp.stack(im_s))
```

```python
import functools
import math

import jax
import jax.numpy as jnp
from jax import lax
from jax.experimental import pallas as pl
from jax.experimental.pallas import tpu as pltpu

F32 = jnp.float32
BF16 = jnp.bfloat16

LN_EPS = 1e-5
RMS_EPS = 1e-6
HG_CHUNK = 16
HG_CHUNKS_PER_ITER = 32
HG_SEQS_PER_STEP = 16
SSM_GROUP = 16
S5_CHUNK = 16
S5_GROUP_TILE = 8
S5_SEQS_PER_STEP = 4
PEER_TOPK = 16
LANES = 128
SUBLANES = 8
VMEM_LIMIT = 60 * 1024 * 1024
PEER_EXPERT_ROWS = 8


def _cparams(*sem):
    return pltpu.CompilerParams(dimension_semantics=sem, vmem_limit_bytes=VMEM_LIMIT)


def _layer_norm(x, g, b):
    mu = jnp.mean(x, axis=-1, keepdims=True)
    xc = x - mu
    var = jnp.mean(xc * xc, axis=-1, keepdims=True)
    return xc * lax.rsqrt(var + LN_EPS) * g + b


def _sigmoid(x):
    return 1.0 / (1.0 + jnp.exp(-x))


def _gelu(x):
    c1 = -2.0 * math.sqrt(2.0 / math.pi)
    return x / (1.0 + jnp.exp(x * (c1 + (c1 * 0.044715) * (x * x))))


def _row_tile(t, target):
    best = None
    for cand in range(SUBLANES, min(t, target) + 1, SUBLANES):
        if t % cand == 0:
            best = cand
    return best if best is not None else t


def _inproj_kernel(x_ref, g_ref, b_ref, w_ref, bi_ref, x0_ref, proj_ref, *, apply_ln):
    x = x_ref[...]
    if apply_ln:
        x = _layer_norm(x, g_ref[...], b_ref[...])
    x0_ref[...] = x
    proj_ref[...] = jnp.dot(x.astype(BF16), w_ref[...], preferred_element_type=F32) + bi_ref[...]


def _inproj(x, g, b, w_bf16, bias, *, apply_ln):
    t, d = x.shape
    d_in = w_bf16.shape[1]
    tm = _row_tile(t, 512)
    return pl.pallas_call(
        functools.partial(_inproj_kernel, apply_ln=apply_ln),
        grid=(t // tm,),
        in_specs=[
            pl.BlockSpec((tm, d), lambda i: (i, 0)),
            pl.BlockSpec((1, d), lambda i: (0, 0)),
            pl.BlockSpec((1, d), lambda i: (0, 0)),
            pl.BlockSpec((d, d_in), lambda i: (0, 0)),
            pl.BlockSpec((1, d_in), lambda i: (0, 0)),
        ],
        out_specs=[
            pl.BlockSpec((tm, d), lambda i: (i, 0)),
            pl.BlockSpec((tm, d_in), lambda i: (i, 0)),
        ],
        out_shape=[jax.ShapeDtypeStruct((t, d), F32), jax.ShapeDtypeStruct((t, d_in), F32)],
        compiler_params=_cparams("parallel"),
        name="inproj",
    )(x, g.reshape(1, d), b.reshape(1, d), w_bf16, bias.reshape(1, d_in))


def _hgrn_kernel(q_ref, f_ref, v_ref, lb_ref, s0_ref, o_ref, s_ref, st_ref,
                 *, chunk, n_iters, per_iter, seqs, valid):
    lb = lb_ref[...]
    ones = jnp.ones((LANES, LANES), BF16)
    groups = seqs * per_iter
    shape = (groups, chunk, LANES)
    row = lax.broadcasted_iota(jnp.int32, shape, 1)
    span = per_iter * chunk
    nt = (((1,), (1,)), ((), ()))
    tn = (((0,), (0,)), ((), ()))

    for b in range(seqs):
        st_ref[b] = s0_ref[b].T

    def take(ref, r0):
        return jnp.concatenate([ref[b, pl.ds(r0, span), :].reshape(per_iter, chunk, LANES)
                                for b in range(seqs)], axis=0)

    def step(i, carry):
        r0 = pl.multiple_of(i * span, span)
        q = take(q_ref, r0)
        fp = take(f_ref, r0)
        v = take(v_ref, r0)
        f = lb + (1.0 - lb) * _sigmoid(fp)
        logf = jnp.log(f)
        kk = (1.0 - lb) * _sigmoid(-fp)
        if valid < chunk:
            live = row < valid
            logf = jnp.where(live, logf, 0.0)
            kk = jnp.where(live, kk, 0.0)
            q = jnp.where(live, q, 0.0)
            v = jnp.where(live, v, 0.0)
        pick = lambda a, s: jnp.broadcast_to(a[:, s:s + 1, :], shape)
        bc = jnp.zeros(shape, F32)
        for s in range(chunk):
            bc = bc + jnp.where(row >= s, pick(logf, s), 0.0)
        split = chunk // 2 if chunk % (2 * SUBLANES) == 0 else 0
        lows = [split if s >= split else 0 for s in range(chunk)]
        prods = []
        for s, lo in zip(range(chunk), lows):
            part = (groups, chunk - lo, LANES)
            near = lambda a: jnp.broadcast_to(a[:, s:s + 1, :], part)
            e = jnp.where(row[:, lo:, :] >= s, jnp.exp(bc[:, lo:, :] - near(bc)), 0.0)
            prods.append((q[:, lo:, :] * e * near(kk)).reshape(groups * (chunk - lo), LANES))
        p_all = jnp.concatenate(prods, axis=0).astype(BF16)
        att = jnp.dot(p_all, ones, preferred_element_type=F32)
        o = jnp.zeros(shape, F32)
        o_low = jnp.zeros((groups, chunk - split, LANES), F32)
        off = 0
        for s, lo in zip(range(chunk), lows):
            part = (groups, chunk - lo, LANES)
            n = groups * (chunk - lo)
            term = att[off:off + n, :].reshape(part) * jnp.broadcast_to(v[:, s:s + 1, :], part)
            off += n
            if lo == 0:
                o = o + term
            else:
                o_low = o_low + term
        if split:
            o = o + jnp.concatenate([jnp.zeros((groups, split, LANES), F32), o_low], axis=1)
        qs = (q * jnp.exp(bc)).astype(BF16)
        kt = (kk * jnp.exp(pick(bc, chunk - 1) - bc)).astype(BF16)
        vb = v.astype(BF16)
        decay = jnp.exp(bc[:, chunk - 1:chunk, :])
        ds = [lax.dot_general(vb[g], kt[g], tn, preferred_element_type=F32) for g in range(groups)]
        outs = []
        for b in range(seqs):
            st = st_ref[b]
            before = []
            for n in range(per_iter):
                g = b * per_iter + n
                before.append(st.astype(BF16))
                st = st * decay[g] + ds[g]
            st_ref[b] = st
            for n in range(per_iter):
                g = b * per_iter + n
                outs.append(o[g] + lax.dot_general(qs[g], before[n], nt, preferred_element_type=F32))
        for b in range(seqs):
            ob = jnp.concatenate(outs[b * per_iter:(b + 1) * per_iter], axis=0)
            ob = ob * lax.rsqrt(jnp.mean(ob * ob, axis=-1, keepdims=True) + RMS_EPS)
            o_ref[b, pl.ds(r0, span), :] = ob
        return carry

    lax.fori_loop(0, n_iters, step, 0)
    for b in range(seqs):
        s_ref[b] = st_ref[b].T


def _hgrn(proj3, lb, s0, *, valid):
    bsz, length, _ = proj3.shape
    heads = s0.shape[1]
    chunk = HG_CHUNK if length % HG_CHUNK == 0 else length
    n_chunks = length // chunk
    divisor = lambda n, cap: max(c for c in range(1, cap + 1) if n % c == 0)
    if n_chunks > 1:
        seqs, per_iter = 1, divisor(n_chunks, HG_CHUNKS_PER_ITER)
    else:
        seqs, per_iter = divisor(bsz, HG_SEQS_PER_STEP), 1
    seq_spec = lambda off: pl.BlockSpec((seqs, length, LANES), lambda b, h: (b, 0, off + h))
    state_spec = pl.BlockSpec((seqs, None, LANES, LANES), lambda b, h: (b, h, 0, 0))
    return pl.pallas_call(
        functools.partial(_hgrn_kernel, chunk=chunk, n_iters=n_chunks // per_iter, per_iter=per_iter,
                          seqs=seqs, valid=min(valid, chunk)),
        grid=(bsz // seqs, heads),
        in_specs=[
            seq_spec(0), seq_spec(heads), seq_spec(2 * heads),
            pl.BlockSpec((1, LANES), lambda b, h: (0, h)),
            state_spec,
        ],
        out_specs=[
            pl.BlockSpec((seqs, length, LANES), lambda b, h: (b, 0, h)),
            state_spec,
        ],
        out_shape=[jax.ShapeDtypeStruct((bsz, length, heads * LANES), F32),
                   jax.ShapeDtypeStruct(s0.shape, F32)],
        scratch_shapes=[pltpu.VMEM((seqs, LANES, LANES), F32)],
        compiler_params=_cparams("parallel", "parallel"),
        name="hgrn2",
    )(proj3, proj3, proj3, lb, s0)


def _s5_kernel(u_ref, bb_ref, c_ref, are_ref, aim_ref, alre_ref, alim_ref, d_ref, h0re_ref, h0im_ref,
               y_ref, hre_ref, him_ref, hpre_ref, hpim_ref, *, lc, n_chunks, seqs):
    rows = seqs * n_chunks
    half = are_ref.shape[-1]
    a_re = are_ref[...]
    a_im = aim_ref[...]
    bb = bb_ref[...]

    def drive(t):
        u_t = u_ref[pl.ds(t, rows, stride=lc), :]
        return u_t, jnp.dot(u_t.astype(BF16), bb, preferred_element_type=F32)

    def advance(h_re, h_im, bu):
        return (a_re * h_re - a_im * h_im + bu[:, :half], a_re * h_im + a_im * h_re + bu[:, half:])

    h_re = jnp.zeros((rows, half), F32)
    h_im = jnp.zeros((rows, half), F32)
    for t in range(lc):
        h_re, h_im = advance(h_re, h_im, drive(t)[1])

    n_tiles = half // LANES

    def put(ref, idx, val):
        for k in range(n_tiles):
            ref[k, idx, :] = val[:, k * LANES:(k + 1) * LANES]

    def get(ref, idx):
        return jnp.concatenate([ref[k, idx, :] for k in range(n_tiles)], axis=-1)

    put(hpre_ref, slice(None), h_re)
    put(hpim_ref, slice(None), h_im)

    al_re = alre_ref[...]
    al_im = alim_ref[...]

    def carry_step(c, carry):
        c_re, c_im = carry
        idx = pl.ds(c, seqs, stride=n_chunks)
        l_re = get(hpre_ref, idx)
        l_im = get(hpim_ref, idx)
        put(hpre_ref, idx, c_re)
        put(hpim_ref, idx, c_im)
        return (al_re * c_re - al_im * c_im + l_re, al_re * c_im + al_im * c_re + l_im)

    c_re, c_im = lax.fori_loop(0, n_chunks, carry_step, (h0re_ref[...], h0im_ref[...]))
    hre_ref[...] = c_re
    him_ref[...] = c_im

    h_re = get(hpre_ref, slice(None))
    h_im = get(hpim_ref, slice(None))
    c_w = c_ref[...]
    d_vec = d_ref[...]
    for t in range(lc):
        u_t, bu = drive(t)
        h_re, h_im = advance(h_re, h_im, bu)
        y = (jnp.dot(h_re.astype(BF16), c_w[:half], preferred_element_type=F32)
             + jnp.dot(h_im.astype(BF16), c_w[half:], preferred_element_type=F32) + d_vec * u_t)
        y_ref[pl.ds(t, rows, stride=lc), :] = y


def _s5_params(a_re, a_im, log_dt, b_re, b_im, c_re, c_im, d, lc):
    groups, pstate = a_re.shape
    dt = jnp.exp(log_dt.astype(F32))[:, None]
    a_re = a_re.astype(F32)
    a_im = a_im.astype(F32)
    zr, zi = a_re * dt, a_im * dt
    mag = jnp.exp(zr)
    ab_re, ab_im = mag * jnp.cos(zi), mag * jnp.sin(zi)
    den = a_re * a_re + a_im * a_im
    nr = ab_re - 1.0
    coef_re = (nr * a_re + ab_im * a_im) / den
    coef_im = (ab_im * a_re - nr * a_im) / den
    b_re = b_re.astype(F32)
    b_im = b_im.astype(F32)
    bb_re = coef_re[..., None] * b_re - coef_im[..., None] * b_im
    bb_im = coef_re[..., None] * b_im + coef_im[..., None] * b_re
    pm = jnp.exp(zr * lc)
    al_re, al_im = pm * jnp.cos(zi * lc), pm * jnp.sin(zi * lc)
    gt = S5_GROUP_TILE
    tiles = groups // gt
    swap = lambda x: jnp.transpose(x, (0, 2, 1))
    bb = jnp.concatenate([_block_diag(swap(bb_re), gt), _block_diag(swap(bb_im), gt)], axis=-1)
    cw = jnp.concatenate([_block_diag(swap(c_re.astype(F32)), gt), -_block_diag(swap(c_im.astype(F32)), gt)], axis=1)
    lane = lambda x: x.reshape(tiles, 1, gt * pstate)
    return dict(bb=bb.astype(BF16), cw=cw.astype(BF16), are=lane(ab_re), aim=lane(ab_im),
                alre=lane(al_re), alim=lane(al_im), d=d.astype(F32).reshape(1, groups * SSM_GROUP))


def _block_diag(x, gm):
    if gm == 1:
        return x
    g, a, b = x.shape
    x = x.reshape(g // gm, gm, a, b)
    eye = jnp.eye(gm, dtype=x.dtype)
    return (x[:, :, :, None, :] * eye[None, :, None, :, None]).reshape(g // gm, gm * a, gm * b)


def _s5(proj, h0_re, h0_im, prm, *, length, lc, u_col):
    bsz, n_groups, pstate = h0_re.shape
    d_ssm = n_groups * SSM_GROUP
    tiles = n_groups // S5_GROUP_TILE
    half = S5_GROUP_TILE * pstate
    n_chunks = length // lc
    seqs = bsz if n_chunks == 1 else min(bsz, S5_SEQS_PER_STEP)
    steps = bsz // seqs
    rows = seqs * n_chunks
    h0r = h0_re.reshape(steps, seqs, n_groups * pstate)
    h0i = h0_im.reshape(steps, seqs, n_groups * pstate)
    tspec = lambda a, b: pl.BlockSpec((None, a, b), lambda i, g: (g, 0, 0))
    sspec = pl.BlockSpec((None, seqs, half), lambda i, g: (i, 0, g))
    y, hre, him = pl.pallas_call(
        functools.partial(_s5_kernel, lc=lc, n_chunks=n_chunks, seqs=seqs),
        grid=(steps, tiles),
        in_specs=[
            pl.BlockSpec((seqs * length, LANES), lambda i, g: (i, u_col // LANES + g)),
            tspec(LANES, 2 * half), tspec(2 * half, LANES),
            tspec(1, half), tspec(1, half), tspec(1, half), tspec(1, half),
            pl.BlockSpec((1, LANES), lambda i, g: (0, g)),
            sspec, sspec,
        ],
        out_specs=[
            pl.BlockSpec((seqs * length, LANES), lambda i, g: (i, g)),
            sspec, sspec,
        ],
        out_shape=[jax.ShapeDtypeStruct((bsz * length, d_ssm), F32),
                   jax.ShapeDtypeStruct(h0r.shape, F32),
                   jax.ShapeDtypeStruct(h0r.shape, F32)],
        scratch_shapes=[pltpu.VMEM((half // LANES, rows, LANES), F32),
                        pltpu.VMEM((half // LANES, rows, LANES), F32)],
        compiler_params=_cparams("parallel", "parallel"),
        name="s5",
    )(proj, prm['bb'], prm['cw'], prm['are'], prm['aim'], prm['alre'], prm['alim'], prm['d'], h0r, h0i)
    return y, hre.reshape(h0_re.shape), him.reshape(h0_re.shape)


def _mix_kernel(x_ref, o_ref, gate_ref, y_ref, hg_g_ref, wglu_ref, bglu_ref, wo_hg_ref, wo_ssm_ref,
                g_ref, b_ref, out_ref, *, alpha):
    gate = gate_ref[...]
    o_hg = o_ref[...] * hg_g_ref[...] * (gate * _sigmoid(gate))
    z = _gelu(y_ref[...])
    glu = jnp.dot(z.astype(BF16), wglu_ref[...], preferred_element_type=F32) + bglu_ref[...]
    o_ssm = z * _sigmoid(glu)
    mix = (jnp.dot(o_hg.astype(BF16), wo_hg_ref[...], preferred_element_type=F32)
           + jnp.dot(o_ssm.astype(BF16), wo_ssm_ref[...], preferred_element_type=F32))
    out_ref[...] = _layer_norm(alpha * x_ref[...] + mix, g_ref[...], b_ref[...])


def _mix(x0, o_hg, proj, y_ssm, hg_g, wglu, bglu, wo_hg, wo_ssm, g, b, *, alpha):
    t, d = x0.shape
    d_hg = o_hg.shape[1]
    d_ssm = y_ssm.shape[1]
    tm = _row_tile(t, 512)
    gate_block = (3 * d_hg) // d_hg
    row = lambda w: pl.BlockSpec((tm, w), lambda i: (i, 0))
    full = lambda a, bb: pl.BlockSpec((a, bb), lambda i: (0, 0))
    return pl.pallas_call(
        functools.partial(_mix_kernel, alpha=alpha),
        grid=(t // tm,),
        in_specs=[
            row(d), row(d_hg),
            pl.BlockSpec((tm, d_hg), lambda i: (i, gate_block)),
            row(d_ssm),
            full(1, d_hg), full(d_ssm, d_ssm), full(1, d_ssm), full(d_hg, d), full(d_ssm, d),
            full(1, d), full(1, d),
        ],
        out_specs=row(d),
        out_shape=jax.ShapeDtypeStruct((t, d), F32),
        compiler_params=_cparams("parallel"),
        name="mix",
    )(x0, o_hg, proj, y_ssm, hg_g.reshape(1, d_hg), wglu, bglu.reshape(1, d_ssm), wo_hg, wo_ssm,
      g.reshape(1, d), b.reshape(1, d))


def _top_rows(s, k):
    n, t = s.shape
    iota = lax.broadcasted_iota(jnp.int32, (n, t), 0).astype(F32)
    krow = lax.broadcasted_iota(jnp.int32, (k, t), 0)
    rank = jnp.full((n, t), float(k), F32)
    vals = jnp.zeros((k, t), F32)
    for a in range(k):
        m = jnp.max(s, axis=0, keepdims=True)
        idx = jnp.min(jnp.where(s == m, iota, float(n)), axis=0, keepdims=True)
        hit = iota == idx
        rank = jnp.where(hit, float(a), rank)
        s = jnp.where(hit, -jnp.inf, s)
        vals = jnp.where(krow == a, jnp.broadcast_to(m, (k, t)), vals)
    return vals, rank


def _candidate_pieces(t1, t2):
    k = PEER_TOPK
    t = t1.shape[1]
    bc = lambda r, n: jnp.broadcast_to(r, (n, t))
    pieces = [(bc(t1[0:1], k), t2, [b for b in range(k)], [True] * k)]
    half = k // 2
    for a in range(1, half):
        nb = k // (a + 1)
        pieces.append((bc(t1[a:a + 1], half), t2[0:half], [a * k + b for b in range(half)],
                       [b < nb for b in range(half)]))
    pieces.append((t1[half:k], bc(t2[0:1], half), [(half + r) * k for r in range(half)], [True] * half))
    return pieces


def _candidate_index_rows(t):
    k = PEER_TOPK
    dummy = jnp.zeros((k, 1), F32)
    vals = [i if ok else k * k for _, _, idx, val in _candidate_pieces(dummy, dummy) for i, ok in zip(idx, val)]
    return jnp.broadcast_to(jnp.asarray(vals, F32)[:, None], (len(vals), t))


def _peer_select(s1, s2, cidx):
    k = PEER_TOPK
    t = s1.shape[1]
    big = float(k * k)
    top1, rank1 = _top_rows(s1, k)
    top2, rank2 = _top_rows(s2, k)
    pieces = _candidate_pieces(top1, top2)
    cand = jnp.concatenate([x + y for x, y, _, _ in pieces], axis=0)
    cand = jnp.where(cidx < big, cand, -jnp.inf)
    chosen = jnp.zeros(cand.shape, F32)
    for _ in range(k):
        m = jnp.max(cand, axis=0, keepdims=True)
        idx = jnp.min(jnp.where(cand == m, cidx, big), axis=0, keepdims=True)
        hit = cidx == idx
        chosen = jnp.where(hit, 1.0, chosen)
        cand = jnp.where(hit, -jnp.inf, cand)
    e1 = jnp.exp(top1 - jnp.broadcast_to(top1[0:1], (k, t)))
    e2 = jnp.exp(top2 - jnp.broadcast_to(top2[0:1], (k, t)))
    ecand = jnp.concatenate([x * y for x, y, _, _ in _candidate_pieces(e1, e2)], axis=0)
    z = jnp.sum(chosen * ecand, axis=0, keepdims=True)
    counts = []
    off = 0
    half = k // 2
    for pi, (x, _, _, _) in enumerate(pieces):
        rows = x.shape[0]
        blk = chosen[off:off + rows]
        off += rows
        if pi < len(pieces) - 1:
            counts.append(jnp.sum(blk, axis=0, keepdims=True))
        else:
            counts.extend(blk[r:r + 1] for r in range(half))
    nk = s1.shape[0]
    c1 = jnp.zeros((nk, t), F32)
    for a in range(k):
        c1 = jnp.where(rank1 == float(a), jnp.broadcast_to(counts[a], (nk, t)), c1)
    phi = jnp.exp(s1 - jnp.broadcast_to(top1[0:1], (nk, t))) * jnp.broadcast_to(1.0 / z, (nk, t))
    psi = jnp.exp(s2 - jnp.broadcast_to(top2[0:1], (nk, t)))
    return c1, phi, rank2, psi


def _sort_network(n):
    out, p = [], 1
    while p < n:
        k = p
        while k >= 1:
            for j in range(k % p, n - k, 2 * k):
                for i in range(min(k, n - j - k)):
                    if (i + j) // (2 * p) == (i + j + k) // (2 * p):
                        out.append((i + j, i + j + k))
            k //= 2
        p *= 2
    return out


def _sorted_top(s):
    n = s.shape[0] // SUBLANES
    x = [s[i * SUBLANES:(i + 1) * SUBLANES, :] for i in range(n)]
    for i, j in _sort_network(n):
        x[i], x[j] = jnp.maximum(x[i], x[j]), jnp.minimum(x[i], x[j])
    shift = SUBLANES // 2
    while shift >= 1:
        r = [pltpu.roll(v, shift, axis=0) for v in x]
        x = [jnp.maximum(x[i], r[n - 1 - i]) for i in range(n)]
        d = n // 2
        while d >= 1:
            for i in range(n):
                if (i & d) == 0:
                    x[i], x[i + d] = jnp.maximum(x[i], x[i + d]), jnp.minimum(x[i], x[i + d])
            d //= 2
        shift //= 2
    return x


def _peer_select_fast(s1, s2, cidx):
    k = PEER_TOPK
    nk, t = s1.shape
    nblk = nk // SUBLANES
    big = float(k * k)
    l1 = _sorted_top(s1)
    l2 = _sorted_top(s2)
    sub = lax.broadcasted_iota(jnp.int32, (SUBLANES, t), 0)

    def stack(lst):
        halves = []
        for base in (0, SUBLANES):
            blk = lst[base]
            for a in range(1, SUBLANES):
                blk = jnp.where(sub == a, lst[base + a], blk)
            halves.append(blk)
        return jnp.concatenate(halves, axis=0)

    top1 = stack(l1)
    top2 = stack(l2)
    pieces = _candidate_pieces(top1, top2)
    cand = jnp.concatenate([x + y for x, y, _, _ in pieces], axis=0)
    cand = jnp.where(cidx < big, cand, -jnp.inf)
    chosen = jnp.zeros(cand.shape, F32)
    for _ in range(k):
        hit = cand == jnp.max(cand, axis=0, keepdims=True)
        chosen = jnp.where(hit, 1.0, chosen)
        cand = jnp.where(hit, -jnp.inf, cand)
    e1 = jnp.exp(top1 - jnp.broadcast_to(top1[0:1], (k, t)))
    e2 = jnp.exp(top2 - jnp.broadcast_to(top2[0:1], (k, t)))
    ecand = jnp.concatenate([x * y for x, y, _, _ in _candidate_pieces(e1, e2)], axis=0)
    z = jnp.sum(chosen * ecand, axis=0, keepdims=True)
    counts = []
    off = 0
    half = k // 2
    for pi, (x, _, _, _) in enumerate(pieces):
        rows = x.shape[0]
        blk = chosen[off:off + rows]
        off += rows
        if pi < len(pieces) - 1:
            counts.append(jnp.sum(blk, axis=0, keepdims=True))
        else:
            counts.extend(blk[r:r + 1] for r in range(half))
    c1_blocks, r2_blocks = [], []
    cnt1 = jnp.zeros((SUBLANES, t), F32)
    cnt2 = jnp.zeros((SUBLANES, t), F32)
    for i in range(nblk):
        x1 = s1[i * SUBLANES:(i + 1) * SUBLANES, :]
        x2 = s2[i * SUBLANES:(i + 1) * SUBLANES, :]
        c1b = jnp.zeros((SUBLANES, t), F32)
        r2b = jnp.zeros((SUBLANES, t), F32)
        for a in range(k):
            c1b = jnp.where(x1 == l1[a], jnp.broadcast_to(counts[a], (SUBLANES, t)), c1b)
            r2b = jnp.where(x2 < l2[a], float(a + 1), r2b)
        c1_blocks.append(c1b)
        r2_blocks.append(r2b)
        cnt1 = cnt1 + jnp.where(x1 >= l1[k - 1], 1.0, 0.0)
        cnt2 = cnt2 + jnp.where(x2 >= l2[k - 1], 1.0, 0.0)
    c1 = jnp.concatenate(c1_blocks, axis=0)
    r2 = jnp.concatenate(r2_blocks, axis=0)
    phi = jnp.exp(s1 - jnp.broadcast_to(top1[0:1], (nk, t))) * jnp.broadcast_to(1.0 / z, (nk, t))
    psi = jnp.exp(s2 - jnp.broadcast_to(top2[0:1], (nk, t)))
    bad = jnp.zeros((SUBLANES, t), F32)
    for a in range(k - 1):
        bad = jnp.where(l1[a] <= l1[a + 1], 1.0, bad)
        bad = jnp.where(l2[a] <= l2[a + 1], 1.0, bad)
    fk = float(k)
    bad = jnp.max(bad, axis=0, keepdims=True)
    bad = jnp.where(jnp.sum(cnt1, axis=0, keepdims=True) != fk, 1.0, bad)
    bad = jnp.where(jnp.sum(cnt2, axis=0, keepdims=True) != fk, 1.0, bad)
    bad = jnp.where(jnp.sum(chosen, axis=0, keepdims=True) != fk, 1.0, bad)
    return c1, phi, r2, psi, bad


def _rows_bf16(row, n):
    pack = 2 * SUBLANES
    tile = jnp.broadcast_to(row, (pack, row.shape[1])).astype(BF16)
    return jnp.concatenate([tile] * (n // pack), axis=0)


def _peer_kernel(x_ref, wq_ref, keys_ref, cidx_ref, u0_ref, u1_ref, u2_ref, vt_ref, g_ref, b_ref,
                 out_ref, xb_ref, qt_ref, c1_ref, phi_ref, r2_ref, psi_ref, uta_ref, utb_ref, wa_ref, wb_ref,
                 acc_ref,
                 *, alpha, heads, nk, rows_per_step):
    j = pl.program_id(1)
    nt = (((1,), (1,)), ((), ()))
    eb = rows_per_step * nk

    @pl.when(j == 0)
    def _():
        xb = x_ref[...].astype(BF16)
        xb_ref[...] = xb
        acc_ref[...] = jnp.zeros_like(acc_ref)
        qt_ref[...] = lax.dot_general(wq_ref[...], xb, nt, preferred_element_type=F32).astype(BF16)

        def select_head(h, carry):
            sc = []
            for c in range(2):
                hc = h * 2 + c
                qt = qt_ref[pl.ds(pl.multiple_of(hc * LANES, LANES), LANES), :]
                sc.append(jnp.dot(keys_ref[hc], qt, preferred_element_type=F32))

            def put(c1, phi, r2, psi):
                c1_ref[h] = c1
                phi_ref[h] = phi
                r2_ref[h] = r2.astype(BF16)
                psi_ref[h] = psi.astype(BF16)

            if nk // SUBLANES == PEER_TOPK:
                *sel, bad = _peer_select_fast(sc[0], sc[1], cidx_ref[...])
                put(*sel)

                @pl.when(jnp.max(bad) > 0.0)
                def _():
                    put(*_peer_select(sc[0], sc[1], cidx_ref[...]))
            else:
                put(*_peer_select(sc[0], sc[1], cidx_ref[...]))
            return carry

        lax.fori_loop(0, heads, select_head, 0)
        uta_ref[...] = lax.dot_general(u0_ref[...], xb, nt, preferred_element_type=F32)

    xb = xb_ref[...]

    def mix_block(block, ut_ref, w_ref):
        for r in range(rows_per_step):
            n1 = block * rows_per_step + r
            gsum = None
            for h in range(heads):
                c1row = _rows_bf16(c1_ref[h, pl.ds(n1, 1), :], nk)
                phirow = _rows_bf16(phi_ref[h, pl.ds(n1, 1), :], nk)
                term = jnp.where(r2_ref[h] < c1row, phirow * psi_ref[h], jnp.zeros((), BF16))
                gsum = term if gsum is None else gsum + term
            act = _gelu(ut_ref[r * nk:(r + 1) * nk, :]).astype(BF16)
            w_ref[r * nk:(r + 1) * nk, :] = gsum * act

    utb_ref[...] = lax.dot_general(u1_ref[...], xb, nt, preferred_element_type=F32)
    mix_block(2 * j, uta_ref, wa_ref)
    acc_ref[...] += jnp.dot(vt_ref[:, :eb], wa_ref[...], preferred_element_type=F32)
    uta_ref[...] = lax.dot_general(u2_ref[...], xb, nt, preferred_element_type=F32)
    mix_block(2 * j + 1, utb_ref, wb_ref)
    acc_ref[...] += jnp.dot(vt_ref[:, eb:], wb_ref[...], preferred_element_type=F32)

    @pl.when(j == pl.num_programs(1) - 1)
    def _():
        ffn = acc_ref[...].T
        out_ref[...] = _layer_norm(alpha * x_ref[...] + ffn, g_ref[...], b_ref[...])


def _peer(x, wq_t, keys, u_tab, v_tab_t, g, b, *, alpha):
    t, d = x.shape
    heads, _, nk, dh = keys.shape
    n_exp = u_tab.shape[0]
    tm = 512 if t % 512 == 0 else _row_tile(t, 512)
    rows = PEER_EXPERT_ROWS
    eb = rows * nk
    keys2 = keys.reshape(heads * 2, nk, dh)
    cidx = _candidate_index_rows(tm)
    n_blocks = n_exp // eb
    return pl.pallas_call(
        functools.partial(_peer_kernel, alpha=alpha, heads=heads, nk=nk, rows_per_step=rows),
        grid=(t // tm, n_blocks // 2),
        in_specs=[
            pl.BlockSpec((tm, d), lambda i, j: (i, 0)),
            pl.BlockSpec(wq_t.shape, lambda i, j: (0, 0)),
            pl.BlockSpec(keys2.shape, lambda i, j: (0, 0, 0)),
            pl.BlockSpec(cidx.shape, lambda i, j: (0, 0)),
            pl.BlockSpec((eb, d), lambda i, j: (0, 0)),
            pl.BlockSpec((eb, d), lambda i, j: (2 * j + 1, 0)),
            pl.BlockSpec((eb, d), lambda i, j: ((2 * j + 2) % n_blocks, 0)),
            pl.BlockSpec((d, 2 * eb), lambda i, j: (0, j)),
            pl.BlockSpec((1, d), lambda i, j: (0, 0)),
            pl.BlockSpec((1, d), lambda i, j: (0, 0)),
        ],
        out_specs=pl.BlockSpec((tm, d), lambda i, j: (i, 0)),
        out_shape=jax.ShapeDtypeStruct((t, d), F32),
        scratch_shapes=[
            pltpu.VMEM((tm, d), BF16),
            pltpu.VMEM((wq_t.shape[0], tm), BF16),
            pltpu.VMEM((heads, nk, tm), F32), pltpu.VMEM((heads, nk, tm), F32),
            pltpu.VMEM((heads, nk, tm), BF16), pltpu.VMEM((heads, nk, tm), BF16),
            pltpu.VMEM((eb, tm), F32), pltpu.VMEM((eb, tm), F32),
            pltpu.VMEM((eb, tm), BF16), pltpu.VMEM((eb, tm), BF16),
            pltpu.VMEM((d, tm), F32),
        ],
        compiler_params=_cparams("parallel", "arbitrary"),
        name="peer",
    )(x, wq_t, keys2, cidx, u_tab, u_tab, u_tab, v_tab_t, g.reshape(1, d), b.reshape(1, d))


def _pad_tokens(a, length):
    return jnp.pad(a, ((0, 0), (0, length - a.shape[1]), (0, 0)))


def _sequence_mixers(proj3, lb, s_hg, s_re, s_im, s5_prm, d_hg):
    bsz, length, _ = proj3.shape
    lpad = -(-length // SUBLANES) * SUBLANES
    o, s_hg_new = _hgrn(_pad_tokens(proj3, lpad) if lpad != length else proj3, lb, s_hg, valid=length)
    lc = S5_CHUNK if length % S5_CHUNK == 0 else length
    y, s_re_new, s_im_new = _s5(proj3.reshape(bsz * length, -1), s_re, s_im, s5_prm[lc],
                                length=length, lc=lc, u_col=4 * d_hg)
    return o[:, :length], y.reshape(bsz, length, -1), s_hg_new, s_re_new, s_im_new


def kernel(x_prompt, x_sample, state_hgrn, state_ssm_re, state_ssm_im, meta_tokens, ln_emb_g, ln_emb_b,
           lb_logits, w_in, b_in, hg_norm_g, ssm_a_re, ssm_a_im, ssm_log_dt, ssm_b_re, ssm_b_im,
           ssm_c_re, ssm_c_im, ssm_d, w_glu, b_glu, w_out, ln1_g, ln1_b, peer_w_q, peer_keys,
           peer_u, peer_v, ln2_g, ln2_b):
    depth = w_in.shape[0]
    alpha = (2.0 * depth) ** 0.25
    bp, seq, d = x_prompt.shape
    bs, dseq, _ = x_sample.shape
    n_meta = meta_tokens.shape[0]
    heads = state_hgrn.shape[2]
    d_hg = heads * LANES
    groups, pstate = state_ssm_re.shape[2], state_ssm_re.shape[3]
    lbs = jnp.cumsum(jax.nn.softmax(lb_logits.astype(F32), axis=0), axis=0)

    xp = x_prompt.astype(F32).reshape(bp * seq, d)
    xm = meta_tokens.astype(F32)
    xs = x_sample.astype(F32).reshape(bs * dseq, d)
    n_s = bs * dseq
    hg_p, re_p, im_p, hg_s, re_s, im_s = [], [], [], [], [], []
    for l in range(depth):
        last = l == depth - 1
        w_in_b = w_in[l].astype(BF16)
        lb = lbs[l].reshape(1, d_hg)
        s5_prm = {lc: _s5_params(ssm_a_re[l], ssm_a_im[l], ssm_log_dt[l], ssm_b_re[l], ssm_b_im[l],
                                 ssm_c_re[l], ssm_c_im[l], ssm_d[l], lc)
                  for lc in {S5_CHUNK if n % S5_CHUNK == 0 else n for n in (n_meta, seq, dseq)}}
        wglu = w_glu[l].astype(BF16)
        wo_hg = w_out[l, :d_hg].astype(BF16)
        wo_ssm = w_out[l, d_hg:].astype(BF16)
        wq_t = peer_w_q[l].T.astype(BF16)
        keys = peer_keys[l].astype(BF16)
        u_tab = peer_u[l].astype(BF16)
        v_tab_t = peer_v[l].T.astype(BF16)

        xsm = jnp.concatenate([xs, xm], axis=0)
        x0_p, proj_p = _inproj(xp, ln_emb_g, ln_emb_b, w_in_b, b_in[l], apply_ln=(l == 0))
        x0_sm, proj_sm = _inproj(xsm, ln_emb_g, ln_emb_b, w_in_b, b_in[l], apply_ln=(l == 0))
        d_in = proj_p.shape[1]

        proj_m = proj_sm[n_s:].reshape(1, n_meta, d_in)
        proj_m8 = jnp.broadcast_to(proj_m, (SUBLANES, n_meta, d_in))
        zero_hg = jnp.zeros((SUBLANES, heads, LANES, LANES), F32)
        zero_ss = jnp.zeros((SUBLANES, groups, pstate), F32)
        o_m, y_m, hg_m, re_m, im_m = _sequence_mixers(proj_m8, lb, zero_hg, zero_ss, zero_ss, s5_prm, d_hg)

        o_p, y_p, shg, sre, sim = _sequence_mixers(
            proj_p.reshape(bp, seq, d_in), lb,
            jnp.broadcast_to(hg_m[:1], (bp,) + hg_m.shape[1:]),
            jnp.broadcast_to(re_m[:1], (bp,) + re_m.shape[1:]),
            jnp.broadcast_to(im_m[:1], (bp,) + im_m.shape[1:]), s5_prm, d_hg)
        hg_p.append(shg)
        re_p.append(sre)
        im_p.append(sim)

        o_s, y_s, shg, sre, sim = _sequence_mixers(
            proj_sm[:n_s].reshape(bs, dseq, d_in), lb, state_hgrn[l].astype(F32),
            state_ssm_re[l].astype(F32), state_ssm_im[l].astype(F32), s5_prm, d_hg)
        hg_s.append(shg)
        re_s.append(sre)
        im_s.append(sim)

        mix_args = (hg_norm_g[l], wglu, b_glu[l], wo_hg, wo_ssm, ln1_g[l], ln1_b[l])
        peer_args = (wq_t, keys, u_tab, v_tab_t, ln2_g[l], ln2_b[l])
        x1_p = _mix(x0_p, o_p.reshape(bp * seq, d_hg), proj_p, y_p.reshape(bp * seq, -1), *mix_args, alpha=alpha)
        xp = _peer(x1_p, *peer_args, alpha=alpha)
        x1_s = _mix(x0_sm[:n_s], o_s.reshape(n_s, d_hg), proj_sm[:n_s], y_s.reshape(n_s, -1), *mix_args, alpha=alpha)
        xs = _peer(x1_s, *peer_args, alpha=alpha)
        if not last:
            x1_m = _mix(jnp.broadcast_to(x0_sm[n_s:], (n_meta, d)), o_m[0], proj_sm[n_s:], y_m[0], *mix_args, alpha=alpha)
            xm = _peer(x1_m, *peer_args, alpha=alpha)

    y_prompt = xp.reshape(bp, seq, d).astype(x_prompt.dtype)
    y_sample = xs.reshape(bs, dseq, d).astype(x_sample.dtype)
    return (y_prompt, y_sample, jnp.stack(hg_p), jnp.stack(re_p), jnp.stack(im_p),
            jnp.stack(hg_s), jnp.stack(re_s), jnp.stack(im_s))
```

```python
import functools
import math

import jax
import jax.numpy as jnp
from jax import lax
from jax.experimental import pallas as pl
from jax.experimental.pallas import tpu as pltpu

F32 = jnp.float32
BF16 = jnp.bfloat16

LN_EPS = 1e-5
RMS_EPS = 1e-6
HG_CHUNK = 16
HG_CHUNKS_PER_ITER = 32
HG_SEQS_PER_STEP = 16
SSM_GROUP = 16
S5_CHUNK = 16
S5_GROUP_TILE = 8
S5_SEQS_PER_STEP = 8
PEER_TOPK = 16
LANES = 128
SUBLANES = 8
VMEM_LIMIT = 60 * 1024 * 1024
PEER_EXPERT_ROWS = 8


def _cparams(*sem):
    return pltpu.CompilerParams(dimension_semantics=sem, vmem_limit_bytes=VMEM_LIMIT)


def _layer_norm(x, g, b):
    mu = jnp.mean(x, axis=-1, keepdims=True)
    xc = x - mu
    var = jnp.mean(xc * xc, axis=-1, keepdims=True)
    return xc * lax.rsqrt(var + LN_EPS) * g + b


def _sigmoid(x):
    return 1.0 / (1.0 + jnp.exp(-x))


def _gelu(x):
    c1 = -2.0 * math.sqrt(2.0 / math.pi)
    return x / (1.0 + jnp.exp(x * (c1 + (c1 * 0.044715) * (x * x))))


def _row_tile(t, target):
    best = None
    for cand in range(SUBLANES, min(t, target) + 1, SUBLANES):
        if t % cand == 0:
            best = cand
    return best if best is not None else t


def _inproj_kernel(x_ref, g_ref, b_ref, w_ref, bi_ref, x0_ref, proj_ref, *, apply_ln):
    x = x_ref[...]
    if apply_ln:
        x = _layer_norm(x, g_ref[...], b_ref[...])
    x0_ref[...] = x
    proj_ref[...] = jnp.dot(x.astype(BF16), w_ref[...], preferred_element_type=F32) + bi_ref[...]


def _inproj(x, g, b, w_bf16, bias, *, apply_ln):
    t, d = x.shape
    d_in = w_bf16.shape[1]
    tm = _row_tile(t, 512)
    return pl.pallas_call(
        functools.partial(_inproj_kernel, apply_ln=apply_ln),
        grid=(t // tm,),
        in_specs=[
            pl.BlockSpec((tm, d), lambda i: (i, 0)),
            pl.BlockSpec((1, d), lambda i: (0, 0)),
            pl.BlockSpec((1, d), lambda i: (0, 0)),
            pl.BlockSpec((d, d_in), lambda i: (0, 0)),
            pl.BlockSpec((1, d_in), lambda i: (0, 0)),
        ],
        out_specs=[
            pl.BlockSpec((tm, d), lambda i: (i, 0)),
            pl.BlockSpec((tm, d_in), lambda i: (i, 0)),
        ],
        out_shape=[jax.ShapeDtypeStruct((t, d), F32), jax.ShapeDtypeStruct((t, d_in), F32)],
        compiler_params=_cparams("parallel"),
        name="inproj",
    )(x, g.reshape(1, d), b.reshape(1, d), w_bf16, bias.reshape(1, d_in))


def _hgrn_kernel(q_ref, f_ref, v_ref, lb_ref, s0_ref, o_ref, s_ref, st_ref,
                 *, chunk, n_iters, per_iter, seqs, valid):
    lb = lb_ref[...]
    ones = jnp.ones((LANES, LANES), BF16)
    groups = seqs * per_iter
    shape = (groups, chunk, LANES)
    row = lax.broadcasted_iota(jnp.int32, shape, 1)
    span = per_iter * chunk
    nt = (((1,), (1,)), ((), ()))
    tn = (((0,), (0,)), ((), ()))

    for b in range(seqs):
        st_ref[b] = s0_ref[b].T

    def take(ref, r0):
        return jnp.concatenate([ref[b, pl.ds(r0, span), :].reshape(per_iter, chunk, LANES)
                                for b in range(seqs)], axis=0)

    def step(i, carry):
        r0 = pl.multiple_of(i * span, span)
        q = take(q_ref, r0)
        fp = take(f_ref, r0)
        v = take(v_ref, r0)
        f = lb + (1.0 - lb) * _sigmoid(fp)
        logf = jnp.log(f)
        kk = (1.0 - lb) * _sigmoid(-fp)
        if valid < chunk:
            live = row < valid
            logf = jnp.where(live, logf, 0.0)
            kk = jnp.where(live, kk, 0.0)
            q = jnp.where(live, q, 0.0)
            v = jnp.where(live, v, 0.0)
        pick = lambda a, s: jnp.broadcast_to(a[:, s:s + 1, :], shape)
        bc = jnp.zeros(shape, F32)
        for s in range(chunk):
            bc = bc + jnp.where(row >= s, pick(logf, s), 0.0)
        split = chunk // 2 if chunk % (2 * SUBLANES) == 0 else 0
        lows = [split if s >= split else 0 for s in range(chunk)]
        prods = []
        for s, lo in zip(range(chunk), lows):
            part = (groups, chunk - lo, LANES)
            near = lambda a: jnp.broadcast_to(a[:, s:s + 1, :], part)
            e = jnp.where(row[:, lo:, :] >= s, jnp.exp(bc[:, lo:, :] - near(bc)), 0.0)
            prods.append((q[:, lo:, :] * e * near(kk)).reshape(groups * (chunk - lo), LANES))
        p_all = jnp.concatenate(prods, axis=0).astype(BF16)
        att = jnp.dot(p_all, ones, preferred_element_type=F32)
        o = jnp.zeros(shape, F32)
        o_low = jnp.zeros((groups, chunk - split, LANES), F32)
        off = 0
        for s, lo in zip(range(chunk), lows):
            part = (groups, chunk - lo, LANES)
            n = groups * (chunk - lo)
            term = att[off:off + n, :].reshape(part) * jnp.broadcast_to(v[:, s:s + 1, :], part)
            off += n
            if lo == 0:
                o = o + term
            else:
                o_low = o_low + term
        if split:
            o = o + jnp.concatenate([jnp.zeros((groups, split, LANES), F32), o_low], axis=1)
        qs = (q * jnp.exp(bc)).astype(BF16)
        kt = (kk * jnp.exp(pick(bc, chunk - 1) - bc)).astype(BF16)
        vb = v.astype(BF16)
        decay = jnp.exp(bc[:, chunk - 1:chunk, :])
        ds = [lax.dot_general(vb[g], kt[g], tn, preferred_element_type=F32) for g in range(groups)]
        outs = []
        for b in range(seqs):
            st = st_ref[b]
            before = []
            for n in range(per_iter):
                g = b * per_iter + n
                before.append(st.astype(BF16))
                st = st * decay[g] + ds[g]
            st_ref[b] = st
            for n in range(per_iter):
                g = b * per_iter + n
                outs.append(o[g] + lax.dot_general(qs[g], before[n], nt, preferred_element_type=F32))
        for b in range(seqs):
            ob = jnp.concatenate(outs[b * per_iter:(b + 1) * per_iter], axis=0)
            ob = ob * lax.rsqrt(jnp.mean(ob * ob, axis=-1, keepdims=True) + RMS_EPS)
            o_ref[b, pl.ds(r0, span), :] = ob
        return carry

    lax.fori_loop(0, n_iters, step, 0)
    for b in range(seqs):
        s_ref[b] = st_ref[b].T


def _hgrn(proj3, lb, s0, *, valid):
    bsz, length, _ = proj3.shape
    heads = s0.shape[1]
    chunk = HG_CHUNK if length % HG_CHUNK == 0 else length
    n_chunks = length // chunk
    divisor = lambda n, cap: max(c for c in range(1, cap + 1) if n % c == 0)
    if n_chunks > 1:
        seqs, per_iter = 1, divisor(n_chunks, HG_CHUNKS_PER_ITER)
    else:
        seqs, per_iter = divisor(bsz, HG_SEQS_PER_STEP), 1
    seq_spec = lambda off: pl.BlockSpec((seqs, length, LANES), lambda b, h: (b, 0, off + h))
    state_spec = pl.BlockSpec((seqs, None, LANES, LANES), lambda b, h: (b, h, 0, 0))
    return pl.pallas_call(
        functools.partial(_hgrn_kernel, chunk=chunk, n_iters=n_chunks // per_iter, per_iter=per_iter,
                          seqs=seqs, valid=min(valid, chunk)),
        grid=(bsz // seqs, heads),
        in_specs=[
            seq_spec(0), seq_spec(heads), seq_spec(2 * heads),
            pl.BlockSpec((1, LANES), lambda b, h: (0, h)),
            state_spec,
        ],
        out_specs=[
            pl.BlockSpec((seqs, length, LANES), lambda b, h: (b, 0, h)),
            state_spec,
        ],
        out_shape=[jax.ShapeDtypeStruct((bsz, length, heads * LANES), F32),
                   jax.ShapeDtypeStruct(s0.shape, F32)],
        scratch_shapes=[pltpu.VMEM((seqs, LANES, LANES), F32)],
        compiler_params=_cparams("parallel", "parallel"),
        name="hgrn2",
    )(proj3, proj3, proj3, lb, s0)


def _s5_kernel(u_ref, bb_ref, c_ref, are_ref, aim_ref, alre_ref, alim_ref, d_ref, h0re_ref, h0im_ref,
               y_ref, hre_ref, him_ref, hpre_ref, hpim_ref, *, lc, n_chunks, seqs):
    rows = seqs * n_chunks
    half = are_ref.shape[-1]
    a_re = are_ref[...]
    a_im = aim_ref[...]
    bb = bb_ref[...]

    def drive(t):
        u_t = u_ref[pl.ds(t, rows, stride=lc), :]
        return u_t, jnp.dot(u_t.astype(BF16), bb, preferred_element_type=F32)

    def advance(h_re, h_im, bu):
        return (a_re * h_re - a_im * h_im + bu[:, :half], a_re * h_im + a_im * h_re + bu[:, half:])

    h_re = jnp.zeros((rows, half), F32)
    h_im = jnp.zeros((rows, half), F32)
    for t in range(lc):
        h_re, h_im = advance(h_re, h_im, drive(t)[1])

    n_tiles = half // LANES

    def put(ref, idx, val):
        for k in range(n_tiles):
            ref[k, idx, :] = val[:, k * LANES:(k + 1) * LANES]

    def get(ref, idx):
        return jnp.concatenate([ref[k, idx, :] for k in range(n_tiles)], axis=-1)

    put(hpre_ref, slice(None), h_re)
    put(hpim_ref, slice(None), h_im)

    al_re = alre_ref[...]
    al_im = alim_ref[...]

    def carry_step(c, carry):
        c_re, c_im = carry
        idx = pl.ds(c, seqs, stride=n_chunks)
        l_re = get(hpre_ref, idx)
        l_im = get(hpim_ref, idx)
        put(hpre_ref, idx, c_re)
        put(hpim_ref, idx, c_im)
        return (al_re * c_re - al_im * c_im + l_re, al_re * c_im + al_im * c_re + l_im)

    c_re, c_im = lax.fori_loop(0, n_chunks, carry_step, (h0re_ref[...], h0im_ref[...]))
    hre_ref[...] = c_re
    him_ref[...] = c_im

    h_re = get(hpre_ref, slice(None))
    h_im = get(hpim_ref, slice(None))
    c_w = c_ref[...]
    d_vec = d_ref[...]
    for t in range(lc):
        u_t, bu = drive(t)
        h_re, h_im = advance(h_re, h_im, bu)
        y = (jnp.dot(h_re.astype(BF16), c_w[:half], preferred_element_type=F32)
             + jnp.dot(h_im.astype(BF16), c_w[half:], preferred_element_type=F32) + d_vec * u_t)
        y_ref[pl.ds(t, rows, stride=lc), :] = y


def _s5_params(a_re, a_im, log_dt, b_re, b_im, c_re, c_im, d, lc):
    groups, pstate = a_re.shape
    dt = jnp.exp(log_dt.astype(F32))[:, None]
    a_re = a_re.astype(F32)
    a_im = a_im.astype(F32)
    zr, zi = a_re * dt, a_im * dt
    mag = jnp.exp(zr)
    ab_re, ab_im = mag * jnp.cos(zi), mag * jnp.sin(zi)
    den = a_re * a_re + a_im * a_im
    nr = ab_re - 1.0
    coef_re = (nr * a_re + ab_im * a_im) / den
    coef_im = (ab_im * a_re - nr * a_im) / den
    b_re = b_re.astype(F32)
    b_im = b_im.astype(F32)
    bb_re = coef_re[..., None] * b_re - coef_im[..., None] * b_im
    bb_im = coef_re[..., None] * b_im + coef_im[..., None] * b_re
    pm = jnp.exp(zr * lc)
    al_re, al_im = pm * jnp.cos(zi * lc), pm * jnp.sin(zi * lc)
    gt = S5_GROUP_TILE
    tiles = groups // gt
    swap = lambda x: jnp.transpose(x, (0, 2, 1))
    bb = jnp.concatenate([_block_diag(swap(bb_re), gt), _block_diag(swap(bb_im), gt)], axis=-1)
    cw = jnp.concatenate([_block_diag(swap(c_re.astype(F32)), gt), -_block_diag(swap(c_im.astype(F32)), gt)], axis=1)
    lane = lambda x: x.reshape(tiles, 1, gt * pstate)
    return dict(bb=bb.astype(BF16), cw=cw.astype(BF16), are=lane(ab_re), aim=lane(ab_im),
                alre=lane(al_re), alim=lane(al_im), d=d.astype(F32).reshape(1, groups * SSM_GROUP))


def _block_diag(x, gm):
    if gm == 1:
        return x
    g, a, b = x.shape
    x = x.reshape(g // gm, gm, a, b)
    eye = jnp.eye(gm, dtype=x.dtype)
    return (x[:, :, :, None, :] * eye[None, :, None, :, None]).reshape(g // gm, gm * a, gm * b)


def _s5(proj, h0_re, h0_im, prm, *, length, lc, u_col):
    bsz, n_groups, pstate = h0_re.shape
    d_ssm = n_groups * SSM_GROUP
    tiles = n_groups // S5_GROUP_TILE
    half = S5_GROUP_TILE * pstate
    n_chunks = length // lc
    seqs = bsz if n_chunks == 1 else min(bsz, S5_SEQS_PER_STEP)
    steps = bsz // seqs
    rows = seqs * n_chunks
    h0r = h0_re.reshape(steps, seqs, n_groups * pstate)
    h0i = h0_im.reshape(steps, seqs, n_groups * pstate)
    tspec = lambda a, b: pl.BlockSpec((None, a, b), lambda i, g: (g, 0, 0))
    sspec = pl.BlockSpec((None, seqs, half), lambda i, g: (i, 0, g))
    y, hre, him = pl.pallas_call(
        functools.partial(_s5_kernel, lc=lc, n_chunks=n_chunks, seqs=seqs),
        grid=(steps, tiles),
        in_specs=[
            pl.BlockSpec((seqs * length, LANES), lambda i, g: (i, u_col // LANES + g)),
            tspec(LANES, 2 * half), tspec(2 * half, LANES),
            tspec(1, half), tspec(1, half), tspec(1, half), tspec(1, half),
            pl.BlockSpec((1, LANES), lambda i, g: (0, g)),
            sspec, sspec,
        ],
        out_specs=[
            pl.BlockSpec((seqs * length, LANES), lambda i, g: (i, g)),
            sspec, sspec,
        ],
        out_shape=[jax.ShapeDtypeStruct((bsz * length, d_ssm), F32),
                   jax.ShapeDtypeStruct(h0r.shape, F32),
                   jax.ShapeDtypeStruct(h0r.shape, F32)],
        scratch_shapes=[pltpu.VMEM((half // LANES, rows, LANES), F32),
                        pltpu.VMEM((half // LANES, rows, LANES), F32)],
        compiler_params=_cparams("parallel", "parallel"),
        name="s5",
    )(proj, prm['bb'], prm['cw'], prm['are'], prm['aim'], prm['alre'], prm['alim'], prm['d'], h0r, h0i)
    return y, hre.reshape(h0_re.shape), him.reshape(h0_re.shape)


def _mix_kernel(x_ref, o_ref, gate_ref, y_ref, hg_g_ref, wglu_ref, bglu_ref, wo_hg_ref, wo_ssm_ref,
                g_ref, b_ref, out_ref, *, alpha):
    gate = gate_ref[...]
    o_hg = o_ref[...] * hg_g_ref[...] * (gate * _sigmoid(gate))
    z = _gelu(y_ref[...])
    glu = jnp.dot(z.astype(BF16), wglu_ref[...], preferred_element_type=F32) + bglu_ref[...]
    o_ssm = z * _sigmoid(glu)
    mix = (jnp.dot(o_hg.astype(BF16), wo_hg_ref[...], preferred_element_type=F32)
           + jnp.dot(o_ssm.astype(BF16), wo_ssm_ref[...], preferred_element_type=F32))
    out_ref[...] = _layer_norm(alpha * x_ref[...] + mix, g_ref[...], b_ref[...])


def _mix(x0, o_hg, proj, y_ssm, hg_g, wglu, bglu, wo_hg, wo_ssm, g, b, *, alpha):
    t, d = x0.shape
    d_hg = o_hg.shape[1]
    d_ssm = y_ssm.shape[1]
    tm = _row_tile(t, 512)
    gate_block = (3 * d_hg) // d_hg
    row = lambda w: pl.BlockSpec((tm, w), lambda i: (i, 0))
    full = lambda a, bb: pl.BlockSpec((a, bb), lambda i: (0, 0))
    return pl.pallas_call(
        functools.partial(_mix_kernel, alpha=alpha),
        grid=(t // tm,),
        in_specs=[
            row(d), row(d_hg),
            pl.BlockSpec((tm, d_hg), lambda i: (i, gate_block)),
            row(d_ssm),
            full(1, d_hg), full(d_ssm, d_ssm), full(1, d_ssm), full(d_hg, d), full(d_ssm, d),
            full(1, d), full(1, d),
        ],
        out_specs=row(d),
        out_shape=jax.ShapeDtypeStruct((t, d), F32),
        compiler_params=_cparams("parallel"),
        name="mix",
    )(x0, o_hg, proj, y_ssm, hg_g.reshape(1, d_hg), wglu, bglu.reshape(1, d_ssm), wo_hg, wo_ssm,
      g.reshape(1, d), b.reshape(1, d))


def _top_rows(s, k):
    n, t = s.shape
    iota = lax.broadcasted_iota(jnp.int32, (n, t), 0).astype(F32)
    krow = lax.broadcasted_iota(jnp.int32, (k, t), 0)
    rank = jnp.full((n, t), float(k), F32)
    vals = jnp.zeros((k, t), F32)
    for a in range(k):
        m = jnp.max(s, axis=0, keepdims=True)
        idx = jnp.min(jnp.where(s == m, iota, float(n)), axis=0, keepdims=True)
        hit = iota == idx
        rank = jnp.where(hit, float(a), rank)
        s = jnp.where(hit, -jnp.inf, s)
        vals = jnp.where(krow == a, jnp.broadcast_to(m, (k, t)), vals)
    return vals, rank


def _candidate_pieces(t1, t2):
    k = PEER_TOPK
    t = t1.shape[1]
    bc = lambda r, n: jnp.broadcast_to(r, (n, t))
    pieces = [(bc(t1[0:1], k), t2, [b for b in range(k)], [True] * k)]
    half = k // 2
    for a in range(1, half):
        nb = k // (a + 1)
        pieces.append((bc(t1[a:a + 1], half), t2[0:half], [a * k + b for b in range(half)],
                       [b < nb for b in range(half)]))
    pieces.append((t1[half:k], bc(t2[0:1], half), [(half + r) * k for r in range(half)], [True] * half))
    return pieces


def _candidate_index_rows(t):
    k = PEER_TOPK
    dummy = jnp.zeros((k, 1), F32)
    vals = [i if ok else k * k for _, _, idx, val in _candidate_pieces(dummy, dummy) for i, ok in zip(idx, val)]
    return jnp.broadcast_to(jnp.asarray(vals, F32)[:, None], (len(vals), t))


def _peer_select(s1, s2, cidx):
    k = PEER_TOPK
    t = s1.shape[1]
    big = float(k * k)
    top1, rank1 = _top_rows(s1, k)
    top2, rank2 = _top_rows(s2, k)
    pieces = _candidate_pieces(top1, top2)
    cand = jnp.concatenate([x + y for x, y, _, _ in pieces], axis=0)
    cand = jnp.where(cidx < big, cand, -jnp.inf)
    chosen = jnp.zeros(cand.shape, F32)
    for _ in range(k):
        m = jnp.max(cand, axis=0, keepdims=True)
        idx = jnp.min(jnp.where(cand == m, cidx, big), axis=0, keepdims=True)
        hit = cidx == idx
        chosen = jnp.where(hit, 1.0, chosen)
        cand = jnp.where(hit, -jnp.inf, cand)
    e1 = jnp.exp(top1 - jnp.broadcast_to(top1[0:1], (k, t)))
    e2 = jnp.exp(top2 - jnp.broadcast_to(top2[0:1], (k, t)))
    ecand = jnp.concatenate([x * y for x, y, _, _ in _candidate_pieces(e1, e2)], axis=0)
    z = jnp.sum(chosen * ecand, axis=0, keepdims=True)
    counts = []
    off = 0
    half = k // 2
    for pi, (x, _, _, _) in enumerate(pieces):
        rows = x.shape[0]
        blk = chosen[off:off + rows]
        off += rows
        if pi < len(pieces) - 1:
            counts.append(jnp.sum(blk, axis=0, keepdims=True))
        else:
            counts.extend(blk[r:r + 1] for r in range(half))
    nk = s1.shape[0]
    c1 = jnp.zeros((nk, t), F32)
    for a in range(k):
        c1 = jnp.where(rank1 == float(a), jnp.broadcast_to(counts[a], (nk, t)), c1)
    phi = jnp.exp(s1 - jnp.broadcast_to(top1[0:1], (nk, t))) * jnp.broadcast_to(1.0 / z, (nk, t))
    psi = jnp.exp(s2 - jnp.broadcast_to(top2[0:1], (nk, t)))
    return c1, phi, rank2, psi


def _sort_network(n):
    out, p = [], 1
    while p < n:
        k = p
        while k >= 1:
            for j in range(k % p, n - k, 2 * k):
                for i in range(min(k, n - j - k)):
                    if (i + j) // (2 * p) == (i + j + k) // (2 * p):
                        out.append((i + j, i + j + k))
            k //= 2
        p *= 2
    return out


def _sorted_top(s):
    n = s.shape[0] // SUBLANES
    x = [s[i * SUBLANES:(i + 1) * SUBLANES, :] for i in range(n)]
    for i, j in _sort_network(n):
        x[i], x[j] = jnp.maximum(x[i], x[j]), jnp.minimum(x[i], x[j])
    shift = SUBLANES // 2
    while shift >= 1:
        r = [pltpu.roll(v, shift, axis=0) for v in x]
        x = [jnp.maximum(x[i], r[n - 1 - i]) for i in range(n)]
        d = n // 2
        while d >= 1:
            for i in range(n):
                if (i & d) == 0:
                    x[i], x[i + d] = jnp.maximum(x[i], x[i + d]), jnp.minimum(x[i], x[i + d])
            d //= 2
        shift //= 2
    return x


def _peer_select_fast(s1, s2, cidx):
    k = PEER_TOPK
    nk, t = s1.shape
    nblk = nk // SUBLANES
    big = float(k * k)
    l1 = _sorted_top(s1)
    l2 = _sorted_top(s2)
    sub = lax.broadcasted_iota(jnp.int32, (SUBLANES, t), 0)

    def stack(lst):
        halves = []
        for base in (0, SUBLANES):
            blk = lst[base]
            for a in range(1, SUBLANES):
                blk = jnp.where(sub == a, lst[base + a], blk)
            halves.append(blk)
        return jnp.concatenate(halves, axis=0)

    top1 = stack(l1)
    top2 = stack(l2)
    pieces = _candidate_pieces(top1, top2)
    cand = jnp.concatenate([x + y for x, y, _, _ in pieces], axis=0)
    cand = jnp.where(cidx < big, cand, -jnp.inf)
    chosen = jnp.zeros(cand.shape, F32)
    for _ in range(k):
        hit = cand == jnp.max(cand, axis=0, keepdims=True)
        chosen = jnp.where(hit, 1.0, chosen)
        cand = jnp.where(hit, -jnp.inf, cand)
    e1 = jnp.exp(top1 - jnp.broadcast_to(top1[0:1], (k, t)))
    e2 = jnp.exp(top2 - jnp.broadcast_to(top2[0:1], (k, t)))
    ecand = jnp.concatenate([x * y for x, y, _, _ in _candidate_pieces(e1, e2)], axis=0)
    z = jnp.sum(chosen * ecand, axis=0, keepdims=True)
    counts = []
    off = 0
    half = k // 2
    for pi, (x, _, _, _) in enumerate(pieces):
        rows = x.shape[0]
        blk = chosen[off:off + rows]
        off += rows
        if pi < len(pieces) - 1:
            counts.append(jnp.sum(blk, axis=0, keepdims=True))
        else:
            counts.extend(blk[r:r + 1] for r in range(half))
    c1_blocks, r2_blocks = [], []
    cnt1 = jnp.zeros((SUBLANES, t), F32)
    cnt2 = jnp.zeros((SUBLANES, t), F32)
    for i in range(nblk):
        x1 = s1[i * SUBLANES:(i + 1) * SUBLANES, :]
        x2 = s2[i * SUBLANES:(i + 1) * SUBLANES, :]
        c1b = jnp.zeros((SUBLANES, t), F32)
        r2b = jnp.zeros((SUBLANES, t), F32)
        for a in range(k):
            c1b = jnp.where(x1 == l1[a], jnp.broadcast_to(counts[a], (SUBLANES, t)), c1b)
            r2b = jnp.where(x2 < l2[a], float(a + 1), r2b)
        c1_blocks.append(c1b)
        r2_blocks.append(r2b)
        cnt1 = cnt1 + jnp.where(x1 >= l1[k - 1], 1.0, 0.0)
        cnt2 = cnt2 + jnp.where(x2 >= l2[k - 1], 1.0, 0.0)
    c1 = jnp.concatenate(c1_blocks, axis=0)
    r2 = jnp.concatenate(r2_blocks, axis=0)
    phi = jnp.exp(s1 - jnp.broadcast_to(top1[0:1], (nk, t))) * jnp.broadcast_to(1.0 / z, (nk, t))
    psi = jnp.exp(s2 - jnp.broadcast_to(top2[0:1], (nk, t)))
    bad = jnp.zeros((SUBLANES, t), F32)
    for a in range(k - 1):
        bad = jnp.where(l1[a] <= l1[a + 1], 1.0, bad)
        bad = jnp.where(l2[a] <= l2[a + 1], 1.0, bad)
    fk = float(k)
    bad = jnp.max(bad, axis=0, keepdims=True)
    bad = jnp.where(jnp.sum(cnt1, axis=0, keepdims=True) != fk, 1.0, bad)
    bad = jnp.where(jnp.sum(cnt2, axis=0, keepdims=True) != fk, 1.0, bad)
    bad = jnp.where(jnp.sum(chosen, axis=0, keepdims=True) != fk, 1.0, bad)
    return c1, phi, r2, psi, bad


def _rows_bf16(row, n):
    pack = 2 * SUBLANES
    tile = jnp.broadcast_to(row, (pack, row.shape[1])).astype(BF16)
    return jnp.concatenate([tile] * (n // pack), axis=0)


def _peer_kernel(x_ref, wq_ref, keys_ref, cidx_ref, u0_ref, u1_ref, u2_ref, vt_ref, g_ref, b_ref,
                 out_ref, xb_ref, qt_ref, c1_ref, phi_ref, r2_ref, psi_ref, uta_ref, utb_ref, wa_ref, wb_ref,
                 acc_ref,
                 *, alpha, heads, nk, rows_per_step):
    j = pl.program_id(1)
    nt = (((1,), (1,)), ((), ()))
    eb = rows_per_step * nk

    @pl.when(j == 0)
    def _():
        xb = x_ref[...].astype(BF16)
        xb_ref[...] = xb
        acc_ref[...] = jnp.zeros_like(acc_ref)
        qt_ref[...] = lax.dot_general(wq_ref[...], xb, nt, preferred_element_type=F32).astype(BF16)

        def select_head(h, carry):
            sc = []
            for c in range(2):
                hc = h * 2 + c
                qt = qt_ref[pl.ds(pl.multiple_of(hc * LANES, LANES), LANES), :]
                sc.append(jnp.dot(keys_ref[hc], qt, preferred_element_type=F32))

            def put(c1, phi, r2, psi):
                c1_ref[h] = c1
                phi_ref[h] = phi
                r2_ref[h] = r2.astype(BF16)
                psi_ref[h] = psi.astype(BF16)

            if nk // SUBLANES == PEER_TOPK:
                *sel, bad = _peer_select_fast(sc[0], sc[1], cidx_ref[...])
                put(*sel)

                @pl.when(jnp.max(bad) > 0.0)
                def _():
                    put(*_peer_select(sc[0], sc[1], cidx_ref[...]))
            else:
                put(*_peer_select(sc[0], sc[1], cidx_ref[...]))
            return carry

        lax.fori_loop(0, heads, select_head, 0)
        uta_ref[...] = lax.dot_general(u0_ref[...], xb, nt, preferred_element_type=F32)

    xb = xb_ref[...]

    def mix_block(block, ut_ref, w_ref):
        for r in range(rows_per_step):
            n1 = block * rows_per_step + r
            gsum = None
            for h in range(heads):
                c1row = _rows_bf16(c1_ref[h, pl.ds(n1, 1), :], nk)
                phirow = _rows_bf16(phi_ref[h, pl.ds(n1, 1), :], nk)
                term = jnp.where(r2_ref[h] < c1row, phirow * psi_ref[h], jnp.zeros((), BF16))
                gsum = term if gsum is None else gsum + term
            act = _gelu(ut_ref[r * nk:(r + 1) * nk, :]).astype(BF16)
            w_ref[r * nk:(r + 1) * nk, :] = gsum * act

    utb_ref[...] = lax.dot_general(u1_ref[...], xb, nt, preferred_element_type=F32)
    mix_block(2 * j, uta_ref, wa_ref)
    acc_ref[...] += jnp.dot(vt_ref[:, :eb], wa_ref[...], preferred_element_type=F32)
    uta_ref[...] = lax.dot_general(u2_ref[...], xb, nt, preferred_element_type=F32)
    mix_block(2 * j + 1, utb_ref, wb_ref)
    acc_ref[...] += jnp.dot(vt_ref[:, eb:], wb_ref[...], preferred_element_type=F32)

    @pl.when(j == pl.num_programs(1) - 1)
    def _():
        ffn = acc_ref[...].T
        out_ref[...] = _layer_norm(alpha * x_ref[...] + ffn, g_ref[...], b_ref[...])


def _peer(x, wq_t, keys, u_tab, v_tab_t, g, b, *, alpha):
    t, d = x.shape
    heads, _, nk, dh = keys.shape
    n_exp = u_tab.shape[0]
    tm = 512 if t % 512 == 0 else _row_tile(t, 512)
    rows = PEER_EXPERT_ROWS
    eb = rows * nk
    keys2 = keys.reshape(heads * 2, nk, dh)
    cidx = _candidate_index_rows(tm)
    n_blocks = n_exp // eb
    return pl.pallas_call(
        functools.partial(_peer_kernel, alpha=alpha, heads=heads, nk=nk, rows_per_step=rows),
        grid=(t // tm, n_blocks // 2),
        in_specs=[
            pl.BlockSpec((tm, d), lambda i, j: (i, 0)),
            pl.BlockSpec(wq_t.shape, lambda i, j: (0, 0)),
            pl.BlockSpec(keys2.shape, lambda i, j: (0, 0, 0)),
            pl.BlockSpec(cidx.shape, lambda i, j: (0, 0)),
            pl.BlockSpec((eb, d), lambda i, j: (0, 0)),
            pl.BlockSpec((eb, d), lambda i, j: (2 * j + 1, 0)),
            pl.BlockSpec((eb, d), lambda i, j: ((2 * j + 2) % n_blocks, 0)),
            pl.BlockSpec((d, 2 * eb), lambda i, j: (0, j)),
            pl.BlockSpec((1, d), lambda i, j: (0, 0)),
            pl.BlockSpec((1, d), lambda i, j: (0, 0)),
        ],
        out_specs=pl.BlockSpec((tm, d), lambda i, j: (i, 0)),
        out_shape=jax.ShapeDtypeStruct((t, d), F32),
        scratch_shapes=[
            pltpu.VMEM((tm, d), BF16),
            pltpu.VMEM((wq_t.shape[0], tm), BF16),
            pltpu.VMEM((heads, nk, tm), F32), pltpu.VMEM((heads, nk, tm), F32),
            pltpu.VMEM((heads, nk, tm), BF16), pltpu.VMEM((heads, nk, tm), BF16),
            pltpu.VMEM((eb, tm), F32), pltpu.VMEM((eb, tm), F32),
            pltpu.VMEM((eb, tm), BF16), pltpu.VMEM((eb, tm), BF16),
            pltpu.VMEM((d, tm), F32),
        ],
        compiler_params=_cparams("parallel", "arbitrary"),
        name="peer",
    )(x, wq_t, keys2, cidx, u_tab, u_tab, u_tab, v_tab_t, g.reshape(1, d), b.reshape(1, d))


def _pad_tokens(a, length):
    return jnp.pad(a, ((0, 0), (0, length - a.shape[1]), (0, 0)))


def _sequence_mixers(proj3, lb, s_hg, s_re, s_im, s5_prm, d_hg):
    bsz, length, _ = proj3.shape
    lpad = -(-length // SUBLANES) * SUBLANES
    o, s_hg_new = _hgrn(_pad_tokens(proj3, lpad) if lpad != length else proj3, lb, s_hg, valid=length)
    lc = S5_CHUNK if length % S5_CHUNK == 0 else length
    y, s_re_new, s_im_new = _s5(proj3.reshape(bsz * length, -1), s_re, s_im, s5_prm[lc],
                                length=length, lc=lc, u_col=4 * d_hg)
    return o[:, :length], y.reshape(bsz, length, -1), s_hg_new, s_re_new, s_im_new


def kernel(x_prompt, x_sample, state_hgrn, state_ssm_re, state_ssm_im, meta_tokens, ln_emb_g, ln_emb_b,
           lb_logits, w_in, b_in, hg_norm_g, ssm_a_re, ssm_a_im, ssm_log_dt, ssm_b_re, ssm_b_im,
           ssm_c_re, ssm_c_im, ssm_d, w_glu, b_glu, w_out, ln1_g, ln1_b, peer_w_q, peer_keys,
           peer_u, peer_v, ln2_g, ln2_b):
    depth = w_in.shape[0]
    alpha = (2.0 * depth) ** 0.25
    bp, seq, d = x_prompt.shape
    bs, dseq, _ = x_sample.shape
    n_meta = meta_tokens.shape[0]
    heads = state_hgrn.shape[2]
    d_hg = heads * LANES
    groups, pstate = state_ssm_re.shape[2], state_ssm_re.shape[3]
    lbs = jnp.cumsum(jax.nn.softmax(lb_logits.astype(F32), axis=0), axis=0)

    xp = x_prompt.astype(F32).reshape(bp * seq, d)
    xm = meta_tokens.astype(F32)
    xs = x_sample.astype(F32).reshape(bs * dseq, d)
    n_s = bs * dseq
    hg_p, re_p, im_p, hg_s, re_s, im_s = [], [], [], [], [], []
    for l in range(depth):
        last = l == depth - 1
        w_in_b = w_in[l].astype(BF16)
        lb = lbs[l].reshape(1, d_hg)
        s5_prm = {lc: _s5_params(ssm_a_re[l], ssm_a_im[l], ssm_log_dt[l], ssm_b_re[l], ssm_b_im[l],
                                 ssm_c_re[l], ssm_c_im[l], ssm_d[l], lc)
                  for lc in {S5_CHUNK if n % S5_CHUNK == 0 else n for n in (n_meta, seq, dseq)}}
        wglu = w_glu[l].astype(BF16)
        wo_hg = w_out[l, :d_hg].astype(BF16)
        wo_ssm = w_out[l, d_hg:].astype(BF16)
        wq_t = peer_w_q[l].T.astype(BF16)
        keys = peer_keys[l].astype(BF16)
        u_tab = peer_u[l].astype(BF16)
        v_tab_t = peer_v[l].T.astype(BF16)

        xsm = jnp.concatenate([xs, xm], axis=0)
        x0_p, proj_p = _inproj(xp, ln_emb_g, ln_emb_b, w_in_b, b_in[l], apply_ln=(l == 0))
        x0_sm, proj_sm = _inproj(xsm, ln_emb_g, ln_emb_b, w_in_b, b_in[l], apply_ln=(l == 0))
        d_in = proj_p.shape[1]

        proj_m = proj_sm[n_s:].reshape(1, n_meta, d_in)
        proj_m8 = jnp.broadcast_to(proj_m, (SUBLANES, n_meta, d_in))
        zero_hg = jnp.zeros((SUBLANES, heads, LANES, LANES), F32)
        zero_ss = jnp.zeros((SUBLANES, groups, pstate), F32)
        o_m, y_m, hg_m, re_m, im_m = _sequence_mixers(proj_m8, lb, zero_hg, zero_ss, zero_ss, s5_prm, d_hg)

        o_p, y_p, shg, sre, sim = _sequence_mixers(
            proj_p.reshape(bp, seq, d_in), lb,
            jnp.broadcast_to(hg_m[:1], (bp,) + hg_m.shape[1:]),
            jnp.broadcast_to(re_m[:1], (bp,) + re_m.shape[1:]),
            jnp.broadcast_to(im_m[:1], (bp,) + im_m.shape[1:]), s5_prm, d_hg)
        hg_p.append(shg)
        re_p.append(sre)
        im_p.append(sim)

        o_s, y_s, shg, sre, sim = _sequence_mixers(
            proj_sm[:n_s].reshape(bs, dseq, d_in), lb, state_hgrn[l].astype(F32),
            state_ssm_re[l].astype(F32), state_ssm_im[l].astype(F32), s5_prm, d_hg)
        hg_s.append(shg)
        re_s.append(sre)
        im_s.append(sim)

        mix_args = (hg_norm_g[l], wglu, b_glu[l], wo_hg, wo_ssm, ln1_g[l], ln1_b[l])
        peer_args = (wq_t, keys, u_tab, v_tab_t, ln2_g[l], ln2_b[l])
        x1_p = _mix(x0_p, o_p.reshape(bp * seq, d_hg), proj_p, y_p.reshape(bp * seq, -1), *mix_args, alpha=alpha)
        xp = _peer(x1_p, *peer_args, alpha=alpha)
        x1_s = _mix(x0_sm[:n_s], o_s.reshape(n_s, d_hg), proj_sm[:n_s], y_s.reshape(n_s, -1), *mix_args, alpha=alpha)
        xs = _peer(x1_s, *peer_args, alpha=alpha)
        if not last:
            x1_m = _mix(jnp.broadcast_to(x0_sm[n_s:], (n_meta, d)), o_m[0], proj_sm[n_s:], y_m[0], *mix_args, alpha=alpha)
            xm = _peer(x1_m, *peer_args, alpha=alpha)

    y_prompt = xp.reshape(bp, seq, d).astype(x_prompt.dtype)
    y_sample = xs.reshape(bs, dseq, d).astype(x_sample.dtype)
    return (y_prompt, y_sample, jnp.stack(hg_p), jnp.stack(re_p), jnp.stack(im_p),
            jnp.stack(hg_s), jnp.stack(re_s), jnp.stack(im_s))
```
